```python
import jax
import jax.numpy as jnp
from jax import lax
import numpy as np

D_MODEL = 1024
BATCH = 4
SEQ = 4096
DEPTH = 2

CTX_LEN = 256
GRID_W = 64
N_GROUPS = 4
GROUP_W = D_MODEL // N_GROUPS
HEAD_DIM = 64
N_GROUP_HEADS = GROUP_W // HEAD_DIM
NORM_EPS = 1e-6
ROPE_BASE = 10000.0
HG_CHUNK = 64
HG_F_FLOOR = 1e-30
RW_DECAY_LORA = 64
RW_A_LORA = 64
RW_GATE_LORA = 128
RW_LN_EPS = 64e-5
SWA_WINDOW = 128
SWA_KV_HEADS = 2
MLA_Q_RANK = 192
MLA_KV_RANK = 128
MLA_NOPE = 64
MLA_ROPE = 32
MLA_V = 64
MLA_Q_BLOCK = 128
D_FF = 2816
N_EXPERTS = 8
TOP_K = 2
D_FF_EXPERT = 1408

HG_COLS = 5 * GROUP_W
RW_COLS = 3 * GROUP_W + 2 * RW_DECAY_LORA + RW_A_LORA + RW_GATE_LORA
SWA_COLS = GROUP_W + 2 * SWA_KV_HEADS * HEAD_DIM
MLA_COLS = MLA_Q_RANK + MLA_KV_RANK + MLA_ROPE
GROUP_COLS = (HG_COLS, RW_COLS, SWA_COLS, MLA_COLS)
IN_COLS = HG_COLS + RW_COLS + SWA_COLS + MLA_COLS

F32 = jnp.float32

kernel_name = 'hybrid_diffusion_prefix_block'


def split_cols(p, sizes):
    idx = np.cumsum(sizes)[:-1].tolist()
    return jnp.split(p, idx, axis=-1)


def rms_norm(x, g):
    xf = x.astype(F32)
    y = xf * lax.rsqrt(jnp.mean(xf * xf, axis=-1, keepdims=True) + NORM_EPS)
    return (y * g.astype(F32)).astype(x.dtype)


def to_heads(x, n):
    B, T, _ = x.shape
    return x.reshape(B, T, n, -1).transpose(0, 2, 1, 3)


def from_heads(x):
    B, n, T, d = x.shape
    return x.transpose(0, 2, 1, 3).reshape(B, T, n * d)


def axial_rope_tables(n_tok, rot_dim):
    rows = n_tok // GRID_W
    row = jnp.repeat(jnp.arange(rows, dtype=F32), GRID_W)
    col = jnp.tile(jnp.arange(GRID_W, dtype=F32), rows)
    n_freq = rot_dim // 4
    inv = ROPE_BASE ** (-jnp.arange(n_freq, dtype=F32) / n_freq)
    ang = jnp.stack([row[:, None] * inv, col[:, None] * inv], axis=1)
    return jnp.cos(ang), jnp.sin(ang)


def apply_axial_rope(x, cos, sin):
    shp = x.shape
    xf = x.astype(F32).reshape(shp[:-1] + (2, 2, shp[-1] // 4))
    x1, x2 = xf[..., 0, :], xf[..., 1, :]
    out = jnp.stack([x1 * cos - x2 * sin, x1 * sin + x2 * cos], axis=-2)
    return out.reshape(shp)


def gla_chunked(q, k, v, logf, s0):
    B, H, T, dk = q.shape
    dv = v.shape[-1]
    C = HG_CHUNK
    n = T // C

    def blocks(a):
        return a.astype(F32).reshape(B, H, n, C, a.shape[-1]).transpose(2, 0, 1, 3, 4)

    qc, kc, vc = blocks(q), blocks(k), blocks(v)
    bc = jnp.cumsum(blocks(logf), axis=-2)
    causal = jnp.tril(jnp.ones((C, C), dtype=bool))[:, :, None]

    def step(S, inp):
        qb, kb, vb, bb = inp
        inter = jnp.einsum('bhtk,bhkv->bhtv', qb * jnp.exp(bb), S)
        rel = jnp.where(causal, bb[:, :, :, None, :] - bb[:, :, None, :, :], -jnp.inf)
        att = jnp.einsum('bhtk,bhsk,bhtsk->bhts', qb, kb, jnp.exp(rel))
        o = inter + jnp.einsum('bhts,bhsv->bhtv', att, vb)
        b_end = bb[:, :, -1:, :]
        S = jnp.exp(b_end[:, :, 0, :, None]) * S + jnp.einsum('bhsk,bhsv->bhkv', kb * jnp.exp(b_end - bb), vb)
        return S, o

    S, o = lax.scan(step, s0, (qc, kc, vc, bc))
    return o.transpose(1, 2, 0, 3, 4).reshape(B, H, T, dv), S


def hgrn2_mixer(p_lat, p_ctx, lb, norm_g, ctx_out):
    H = N_GROUP_HEADS

    def prep(p):
        q, zf, zb, i, g = split_cols(p.astype(F32), (GROUP_W,) * 5)
        dirs = []
        for d, z in enumerate((zf, zb)):
            f = lb[d] + (1.0 - lb[d]) * jax.nn.sigmoid(z)
            logf = jnp.log(jnp.maximum(f, HG_F_FLOOR))
            key_in = (1.0 - lb[d]) * jax.nn.sigmoid(-z)
            dirs.append((to_heads(logf, H), to_heads(key_in, H)))
        return to_heads(jax.nn.silu(q), H), to_heads(i, H), dirs, g

    def readout(o, g):
        o = o * lax.rsqrt(jnp.mean(o * o, axis=-1, keepdims=True) + NORM_EPS)
        return from_heads(o) * norm_g * jax.nn.silu(g)

    def flip(a):
        return a[:, :, ::-1]

    qc, ic, dc, gc = prep(p_ctx)
    ql, il, dl, gl = prep(p_lat)
    s0 = jnp.zeros(qc.shape[:2] + (HEAD_DIM, HEAD_DIM), F32)
    (lfc, kc), (lfl, kl) = dc[0], dl[0]
    oc_f, sc_f = gla_chunked(qc, kc, ic, lfc, s0)
    ol_f, _ = gla_chunked(ql, kl, il, lfl, sc_f)
    (lfc, kc), (lfl, kl) = dc[1], dl[1]
    oc_b, sc_b = gla_chunked(flip(qc), flip(kc), flip(ic), flip(lfc), s0)
    ol_b, _ = gla_chunked(flip(ql), flip(kl), flip(il), flip(lfl), sc_b)
    out_lat = readout(ol_f + flip(ol_b), gl)
    out_ctx = readout(oc_f + flip(oc_b), gc) if ctx_out else None
    return out_lat, out_ctx


def token_shift(p, mu):
    prev = jnp.pad(p, ((0, 0), (1, 0), (0, 0)))[:, :-1]
    nxt = jnp.pad(p, ((0, 0), (0, 1), (0, 0)))[:, 1:]
    return p + mu[0] * (prev - p) + mu[1] * (nxt - p)


def rwkv7_scan(r, w, k, v, kk, b, s0, reverse):
    def step(S, inp):
        r_t, w_t, k_t, v_t, kk_t, b_t = inp
        sa = jnp.einsum('bhvk,bhk->bhv', S, kk_t)
        S = S * w_t[:, :, None, :] - sa[..., None] * b_t[:, :, None, :] + v_t[..., None] * k_t[:, :, None, :]
        return S, jnp.einsum('bhvk,bhk->bhv', S, r_t)

    xs = tuple(a.transpose(1, 0, 2, 3) for a in (r, w, k, v, kk, b))
    S, o = lax.scan(step, s0, xs, reverse=reverse)
    return o.transpose(1, 0, 2, 3), S


def rwkv7_mixer(p_lat, p_ctx, mu, w0, w2, a0, a2, g2, k_k, k_a, r_k, ln_g, ln_b, ctx_out):
    H, N = N_GROUP_HEADS, HEAD_DIM
    sizes = (GROUP_W, GROUP_W, GROUP_W, RW_DECAY_LORA, RW_DECAY_LORA, RW_A_LORA, RW_GATE_LORA)

    def prep(p):
        B, T, _ = p.shape

        def hd(a):
            return a.reshape(B, T, H, N)

        r, k, v, yw_f, yw_b, ya, yg = split_cols(token_shift(p.astype(F32), mu), sizes)
        decays = []
        for d, yw in enumerate((yw_f, yw_b)):
            w_log = -jax.nn.softplus(-(w0[d] + jnp.tanh(yw) @ w2[d])) - 0.5
            decays.append(hd(jnp.exp(-jnp.exp(w_log))))
        a = hd(jax.nn.sigmoid(a0 + ya @ a2))
        kk = hd(k * k_k)
        kk = kk / jnp.maximum(jnp.sqrt(jnp.sum(kk * kk, axis=-1, keepdims=True)), 1e-12)
        k = hd(k) * (1.0 + (a - 1.0) * k_a.reshape(H, N))
        g = jax.nn.sigmoid(yg) @ g2
        return hd(r), k, hd(v), kk, kk * a, decays, g

    def readout(o, r, k, v, g):
        B, T = o.shape[:2]
        mean = jnp.mean(o, axis=-1, keepdims=True)
        var = jnp.mean(jnp.square(o - mean), axis=-1, keepdims=True)
        o = ((o - mean) * lax.rsqrt(var + RW_LN_EPS)).reshape(B, T, GROUP_W) * ln_g + ln_b
        bonus = jnp.sum(r * k * r_k, axis=-1, keepdims=True) * v
        return (o + bonus.reshape(B, T, GROUP_W)) * g

    rc, kc, vc, kkc, bc, wc, gc = prep(p_ctx)
    rl, kl, vl, kkl, bl, wl, gl = prep(p_lat)
    s0 = jnp.zeros((rc.shape[0], H, N, N), F32)
    oc_f, sc_f = rwkv7_scan(rc, wc[0], kc, vc, kkc, bc, s0, False)
    ol_f, _ = rwkv7_scan(rl, wl[0], kl, vl, kkl, bl, sc_f, False)
    oc_b, sc_b = rwkv7_scan(rc, wc[1], kc, vc, kkc, bc, s0, True)
    ol_b, _ = rwkv7_scan(rl, wl[1], kl, vl, kkl, bl, sc_b, True)
    out_lat = readout(ol_f + ol_b, rl, kl, vl, gl)
    out_ctx = readout(oc_f + oc_b, rc, kc, vc, gc) if ctx_out else None
    return out_lat, out_ctx


def swa_mixer(p_lat, p_ctx, sink, cos, sin, ctx_out):
    H, KV, d, W = N_GROUP_HEADS, SWA_KV_HEADS, HEAD_DIM, SWA_WINDOW
    G = H // KV
    scale = HEAD_DIM ** -0.5

    def prep(p):
        q, k, v = split_cols(p, (GROUP_W, KV * d, KV * d))
        return to_heads(q, H), to_heads(k, KV), to_heads(v, KV)

    ql, kl, vl = prep(p_lat)
    qc, kc, vc = prep(p_ctx)
    ql = apply_axial_rope(ql, cos, sin)
    kl = apply_axial_rope(kl, cos, sin)
    B, _, T, _ = ql.shape
    Lc = kc.shape[2]
    nb = T // W
    qb = ql.reshape(B, KV, G, nb, W, d)

    def band(a):
        ap = jnp.pad(a, ((0, 0), (0, 0), (W, W), (0, 0))).reshape(B, KV, nb + 2, W, d)
        return jnp.concatenate([ap[:, :, :nb], ap[:, :, 1:nb + 1], ap[:, :, 2:]], axis=3)

    kw, vw = band(kl), band(vl)
    i = jnp.arange(W)[:, None]
    j = jnp.arange(3 * W)[None, :]
    s_abs = (jnp.arange(nb)[:, None, None] - 1) * W + j[None]
    mask = (jnp.abs(W + i - j) <= W)[None] & (s_abs >= 0) & (s_abs < T)
    s_loc = jnp.einsum('bkgnqd,bknsd->bkgnqs', qb, kw).astype(F32) * scale
    s_ctx = jnp.einsum('bkgnqd,bkcd->bkgnqc', qb, kc).astype(F32) * scale
    sink_l = jnp.broadcast_to(sink.astype(F32).reshape(1, KV, G, 1, 1, 1), (B, KV, G, nb, W, 1))
    logits = jnp.concatenate([jnp.where(mask, s_loc, -jnp.inf), s_ctx, sink_l], axis=-1)
    prob = jax.nn.softmax(logits, axis=-1)
    o = (jnp.einsum('bkgnqs,bknsd->bkgnqd', prob[..., :3 * W], vw)
         + jnp.einsum('bkgnqc,bkcd->bkgnqd', prob[..., 3 * W:3 * W + Lc], vc))
    out_lat = o.transpose(0, 3, 4, 1, 2, 5).reshape(B, T, GROUP_W)
    out_ctx = None
    if ctx_out:
        qg = qc.reshape(B, KV, G, Lc, d)
        s = jnp.einsum('bkgqd,bkcd->bkgqc', qg, kc).astype(F32) * scale
        sink_c = jnp.broadcast_to(sink.astype(F32).reshape(1, KV, G, 1, 1), (B, KV, G, Lc, 1))
        prob_c = jax.nn.softmax(jnp.concatenate([s, sink_c], axis=-1), axis=-1)[..., :Lc]
        oc = jnp.einsum('bkgqc,bkcd->bkgqd', prob_c, vc)
        out_ctx = oc.transpose(0, 3, 1, 2, 4).reshape(B, Lc, GROUP_W)
    return out_lat, out_ctx


def mla_mixer(p_lat, p_ctx, qnorm_g, wuq, kvnorm_g, wukv, cos, sin, ctx_out):
    H = N_GROUP_HEADS
    scale = (MLA_NOPE + MLA_ROPE) ** -0.5

    def prep(p):
        cq, ckv, kr = split_cols(p, (MLA_Q_RANK, MLA_KV_RANK, MLA_ROPE))
        q = to_heads(rms_norm(cq, qnorm_g) @ wuq, H)
        kv = to_heads(rms_norm(ckv, kvnorm_g) @ wukv, H)
        return q[..., :MLA_NOPE], q[..., MLA_NOPE:], kv[..., :MLA_NOPE], kv[..., MLA_NOPE:], kr

    qn_l, qr_l, kn_l, v_l, kr_l = prep(p_lat)
    qn_c, qr_c, kn_c, v_c, kr_c = prep(p_ctx)
    qr_l = apply_axial_rope(qr_l, cos, sin)
    kr_l = apply_axial_rope(kr_l, cos, sin)
    kn = jnp.concatenate([kn_l, kn_c], axis=2)
    vv = jnp.concatenate([v_l, v_c], axis=2)
    kr = jnp.concatenate([kr_l, kr_c.astype(kr_l.dtype)], axis=1)
    B, _, T, _ = qn_l.shape
    nb = T // MLA_Q_BLOCK

    def blocks(a):
        return a.reshape(B, H, nb, MLA_Q_BLOCK, a.shape[-1]).transpose(2, 0, 1, 3, 4)

    def attend(qs):
        qn, qr = qs
        s = jnp.einsum('bhqd,bhkd->bhqk', qn, kn) + jnp.einsum('bhqd,bkd->bhqk', qr, kr)
        prob = jax.nn.softmax(s.astype(F32) * scale, axis=-1)
        return jnp.einsum('bhqk,bhkd->bhqd', prob, vv)

    o = lax.map(attend, (blocks(qn_l), blocks(qr_l)))
    out_lat = o.transpose(1, 0, 3, 2, 4).reshape(B, T, H * MLA_V)
    out_ctx = None
    if ctx_out:
        s = jnp.einsum('bhqd,bhkd->bhqk', qn_c, kn_c) + jnp.einsum('bhqd,bkd->bhqk', qr_c, kr_c)
        prob = jax.nn.softmax(s.astype(F32) * scale, axis=-1)
        out_ctx = from_heads(jnp.einsum('bhqk,bhkd->bhqd', prob, v_c))
    return out_lat, out_ctx


def swiglu(h, w1, w3, w2):
    return (jax.nn.silu(h @ w1) * (h @ w3)) @ w2


def moe_swiglu(h, router, w1, w3, w2):
    logits = (h @ router).astype(F32)
    top_val, top_idx = lax.top_k(logits, TOP_K)
    gates = jax.nn.softmax(top_val, axis=-1)
    combine = jnp.sum(jax.nn.one_hot(top_idx, N_EXPERTS, dtype=F32) * gates[..., None], axis=-2)
    out = jnp.zeros(h.shape, F32)
    for e in range(N_EXPERTS):
        out = out + combine[..., e:e + 1] * swiglu(h, w1[e], w3[e], w2[e])
    return out


def setup_inputs(seed: int = 0) -> dict:
    key = jax.random.key(seed)
    ks = iter(jax.random.split(key, 40))

    def nrm(shape, std):
        return std * jax.random.normal(next(ks), shape, F32)

    L = DEPTH
    ND = (DEPTH + 1) // 2
    NM = DEPTH // 2
    H = N_GROUP_HEADS
    D = D_MODEL
    return {
        'x': nrm((BATCH, SEQ, D), 1.0),
        'c': nrm((BATCH, D), 1.0),
        'ctx': nrm((BATCH, CTX_LEN, D), 1.0),
        'c_ctx': nrm((D,), 1.0),
        'w_ada': nrm((L, D, 6 * D), 0.5 * D ** -0.5),
        'b_ada': nrm((L, 6 * D), 0.02),
        'norm1_g': 1.0 + nrm((L, D), 0.1),
        'norm2_g': 1.0 + nrm((L, D), 0.1),
        'w_in': nrm((L, D, IN_COLS), D ** -0.5),
        'hg_lb': nrm((L, 2, GROUP_W), 1.0),
        'hg_norm_g': 1.0 + nrm((L, GROUP_W), 0.1),
        'rw_mu': 0.3 + nrm((L, 2, RW_COLS), 0.1),
        'rw_w0': -1.0 + nrm((L, 2, GROUP_W), 0.5),
        'rw_w2': nrm((L, 2, RW_DECAY_LORA, GROUP_W), 0.5 * RW_DECAY_LORA ** -0.5),
        'rw_a0': nrm((L, GROUP_W), 0.3),
        'rw_a2': nrm((L, RW_A_LORA, GROUP_W), RW_A_LORA ** -0.5),
        'rw_g2': nrm((L, RW_GATE_LORA, GROUP_W), RW_GATE_LORA ** -0.5),
        'rw_k_k': 0.85 + nrm((L, GROUP_W), 0.1),
        'rw_k_a': 1.0 + nrm((L, GROUP_W), 0.1),
        'rw_r_k': nrm((L, H, HEAD_DIM), 0.1),
        'rw_ln_g': 1.0 + nrm((L, GROUP_W), 0.1),
        'rw_ln_b': nrm((L, GROUP_W), 0.02),
        'swa_sink': nrm((L, H), 0.5),
        'mla_qnorm_g': 1.0 + nrm((L, MLA_Q_RANK), 0.1),
        'mla_wuq': nrm((L, MLA_Q_RANK, H * (MLA_NOPE + MLA_ROPE)), MLA_Q_RANK ** -0.5),
        'mla_kvnorm_g': 1.0 + nrm((L, MLA_KV_RANK), 0.1),
        'mla_wukv': nrm((L, MLA_KV_RANK, H * (MLA_NOPE + MLA_V)), MLA_KV_RANK ** -0.5),
        'w_out': nrm((L, N_GROUPS * GROUP_W, D), (N_GROUPS * GROUP_W) ** -0.5),
        'ffn_w1': nrm((ND, D, D_FF), D ** -0.5),
        'ffn_w3': nrm((ND, D, D_FF), D ** -0.5),
        'ffn_w2': nrm((ND, D_FF, D), D_FF ** -0.5),
        'moe_router': nrm((NM, D, N_EXPERTS), D ** -0.5),
        'moe_w1': nrm((NM, N_EXPERTS, D, D_FF_EXPERT), D ** -0.5),
        'moe_w3': nrm((NM, N_EXPERTS, D, D_FF_EXPERT), D ** -0.5),
        'moe_w2': nrm((NM, N_EXPERTS, D_FF_EXPERT, D), D_FF_EXPERT ** -0.5),
        'final_norm_g': 1.0 + nrm((D,), 0.1),
    }


def reference(x, c, ctx, c_ctx, w_ada, b_ada, norm1_g, norm2_g, w_in, hg_lb, hg_norm_g,
              rw_mu, rw_w0, rw_w2, rw_a0, rw_a2, rw_g2, rw_k_k, rw_k_a, rw_r_k, rw_ln_g, rw_ln_b,
              swa_sink, mla_qnorm_g, mla_wuq, mla_kvnorm_g, mla_wukv, w_out,
              ffn_w1, ffn_w3, ffn_w2, moe_router, moe_w1, moe_w3, moe_w2, final_norm_g):
    T = x.shape[1]
    cos_h, sin_h = axial_rope_tables(T, HEAD_DIM)
    cos_m, sin_m = axial_rope_tables(T, MLA_ROPE)
    lb_w = jax.nn.softmax(hg_lb.astype(F32), axis=0)
    lb_all = jnp.cumsum(lb_w, axis=0) - lb_w[:1]

    def channel_mix(h, l):
        if l % 2 == 0:
            e = l // 2
            return swiglu(h, ffn_w1[e], ffn_w3[e], ffn_w2[e])
        e = l // 2
        return moe_swiglu(h, moe_router[e], moe_w1[e], moe_w3[e], moe_w2[e])

    x_lat, x_ctx = x, ctx
    for l in range(DEPTH):
        ctx_out = l < DEPTH - 1
        mod = jax.nn.silu(c) @ w_ada[l] + b_ada[l]
        mod_c = jax.nn.silu(c_ctx) @ w_ada[l] + b_ada[l]
        sh1, sc1, g1, sh2, sc2, g2 = jnp.split(mod[:, None, :], 6, axis=-1)
        csh1, csc1, cg1, csh2, csc2, cg2 = jnp.split(mod_c, 6, axis=-1)

        h_l = rms_norm(x_lat, norm1_g[l]) * (1.0 + sc1) + sh1
        h_c = rms_norm(x_ctx, norm1_g[l]) * (1.0 + csc1) + csh1
        pl = split_cols(h_l @ w_in[l], GROUP_COLS)
        pc = split_cols(h_c @ w_in[l], GROUP_COLS)

        hg_l, hg_c = hgrn2_mixer(pl[0], pc[0], lb_all[l], hg_norm_g[l], ctx_out)
        rw_l, rw_c = rwkv7_mixer(pl[1], pc[1], rw_mu[l], rw_w0[l], rw_w2[l], rw_a0[l], rw_a2[l], rw_g2[l],
                                 rw_k_k[l], rw_k_a[l], rw_r_k[l], rw_ln_g[l], rw_ln_b[l], ctx_out)
        sw_l, sw_c = swa_mixer(pl[2], pc[2], swa_sink[l], cos_h, sin_h, ctx_out)
        ml_l, ml_c = mla_mixer(pl[3], pc[3], mla_qnorm_g[l], mla_wuq[l], mla_kvnorm_g[l], mla_wukv[l],
                               cos_m, sin_m, ctx_out)

        x_lat = x_lat + g1 * (jnp.concatenate([hg_l, rw_l, sw_l, ml_l], axis=-1) @ w_out[l])
        h2 = rms_norm(x_lat, norm2_g[l]) * (1.0 + sc2) + sh2
        x_lat = x_lat + g2 * channel_mix(h2, l)
        if ctx_out:
            x_ctx = x_ctx + cg1 * (jnp.concatenate([hg_c, rw_c, sw_c, ml_c], axis=-1) @ w_out[l])
            h2c = rms_norm(x_ctx, norm2_g[l]) * (1.0 + csc2) + csh2
            x_ctx = x_ctx + cg2 * channel_mix(h2c, l)

    return rms_norm(x_lat, final_norm_g).astype(x.dtype)
```

```python
import functools

import jax
import jax.numpy as jnp
import numpy as np
from jax import lax
from jax.experimental import pallas as pl
from jax.experimental.pallas import tpu as pltpu

F32 = jnp.float32
BF16 = jnp.bfloat16

D = 1024
GW = 256
HD = 64
NH = 4
GRID_W = 64
NORM_EPS = 1e-6
ROPE_BASE = 10000.0
HG_F_FLOOR = 1e-30
RW_LN_EPS = 64e-5
SWA_W = 128
MLA_Q_RANK = 192
MLA_KV_RANK = 128
MLA_NOPE = 64
MLA_ROPE = 32
D_FF = 2816
N_EXPERTS = 8
D_FF_EXPERT = 1408

TM = 256
CH = 64
HB = 16
VMEM_LIMIT = 56 * 1024 * 1024

_NN = (((1,), (0,)), ((), ()))
_NT = (((1,), (1,)), ((), ()))
_TN = (((0,), (0,)), ((), ()))


def _mm(a, b, dims=_NN, mode="bf16"):
    if mode == "bf16":
        return lax.dot_general(a.astype(BF16), b.astype(BF16), dims, preferred_element_type=F32)
    if mode == "x3":
        ah = a.astype(BF16)
        al = (a - ah.astype(F32)).astype(BF16)
        bh = b.astype(BF16)
        bl = (b - bh.astype(F32)).astype(BF16)
        d = lambda x, y: lax.dot_general(x, y, dims, preferred_element_type=F32)
        return d(ah, bh) + (d(ah, bl) + d(al, bh))
    return lax.dot_general(a, b, dims, precision=lax.Precision.HIGHEST, preferred_element_type=F32)


def _sigmoid(x):
    return 1.0 / (1.0 + jnp.exp(-x))


def _silu(x):
    return x / (1.0 + jnp.exp(-x))


def _iota(shape, dim):
    return lax.broadcasted_iota(jnp.int32, shape, dim)


def _head_block_mask(n):
    return (_iota((n, n), 0) // HD) == (_iota((n, n), 1) // HD)


def _cparams(sem):
    return pltpu.CompilerParams(dimension_semantics=sem, vmem_limit_bytes=VMEM_LIMIT)


def _ada_body(c_ref, w_ref, b_ref, o_ref):
    s = _silu(c_ref[...])
    o_ref[0] = _mm(s, w_ref[0]) + b_ref[0]


def _ada(c8, w_ada, b_ada):
    L = w_ada.shape[0]
    r8 = c8.shape[0]
    tn = 1536
    return pl.pallas_call(
        _ada_body,
        grid=(L, 6 * D // tn),
        in_specs=[
            pl.BlockSpec((r8, D), lambda l, j: (0, 0)),
            pl.BlockSpec((1, D, tn), lambda l, j: (l, 0, j)),
            pl.BlockSpec((1, 1, tn), lambda l, j: (l, 0, j)),
        ],
        out_specs=pl.BlockSpec((1, r8, tn), lambda l, j: (l, 0, j)),
        out_shape=jax.ShapeDtypeStruct((L, r8, 6 * D), F32),
        compiler_params=_cparams(("parallel", "parallel")),
        name="adaln",
    )(c8, w_ada, b_ada.reshape(L, 1, 6 * D))


def _proj_body(*refs, pending, lc_tiles, nb):
    if pending:
        (x_ref, y_ref, modp_ref, mod_ref, g_ref, whg, wrw, wswa, wmla, cs_s, sn_s, cs_m, sn_m,
         xo_ref, phg, prw, pswa, pmla) = refs
    else:
        (x_ref, mod_ref, g_ref, whg, wrw, wswa, wmla, cs_s, sn_s, cs_m, sn_m,
         phg, prw, pswa, pmla) = refs
    b = pl.program_id(0)
    i = pl.program_id(1)
    row = jnp.where(i < lc_tiles, nb, b)
    x = x_ref[0]
    if pending:
        gate = modp_ref[pl.ds(row, 1), 5 * D:6 * D]
        x = x + gate * y_ref[0]
        xo_ref[0] = x
    sh = mod_ref[pl.ds(row, 1), 0:D]
    sc = mod_ref[pl.ds(row, 1), D:2 * D]
    ms = jnp.mean(x * x, axis=-1, keepdims=True)
    h = (x * lax.rsqrt(ms + NORM_EPS) * g_ref[...]) * (1.0 + sc) + sh
    hb = h.astype(BF16)
    phg[0] = _mm(hb, whg[...])
    prw[0] = _mm(hb, wrw[...])
    s = _mm(hb, wswa[...])
    qk = s[:, 0:2 * GW] * cs_s[...] + s[:, 3 * GW:5 * GW] * sn_s[...]
    pswa[0, :, 0:2 * GW] = qk.astype(BF16)
    pswa[0, :, 2 * GW:3 * GW] = s[:, 2 * GW:3 * GW].astype(BF16)
    m = _mm(hb, wmla[...])
    pmla[0, :, 0:384] = m[:, 0:384]
    pmla[0, :, 384:512] = m[:, 384:512] * cs_m[...] + m[:, 512:640] * sn_m[...]


def _proj(x, pend, mod, g, w, tabs, lc):
    B, S, _ = x.shape
    nt = S // TM
    pending = pend is not None
    row_spec = lambda n: pl.BlockSpec((1, TM, n), lambda b, i: (b, i, 0))
    full = lambda a: pl.BlockSpec(a.shape, lambda b, i: (0,) * a.ndim)
    tab_spec = lambda a: pl.BlockSpec((TM, a.shape[1]), lambda b, i: (i, 0))
    ins = [x]
    in_specs = [row_spec(D)]
    if pending:
        ins += [pend[0], pend[1]]
        in_specs += [row_spec(D), full(pend[1])]
    ins += [mod, g, w["hg"], w["rw"], w["swa"], w["mla"], tabs["cs_s"], tabs["sn_s"], tabs["cs_m"], tabs["sn_m"]]
    in_specs += [full(mod), full(g), full(w["hg"]), full(w["rw"]), full(w["swa"]), full(w["mla"]),
                 tab_spec(tabs["cs_s"]), tab_spec(tabs["sn_s"]), tab_spec(tabs["cs_m"]), tab_spec(tabs["sn_m"])]
    out_shape = [jax.ShapeDtypeStruct((B, S, 5 * GW), F32), jax.ShapeDtypeStruct((B, S, 5 * GW), F32),
                 jax.ShapeDtypeStruct((B, S, 3 * GW), BF16), jax.ShapeDtypeStruct((B, S, 512), F32)]
    out_specs = [row_spec(5 * GW), row_spec(5 * GW), row_spec(3 * GW), row_spec(512)]
    if pending:
        out_shape = [jax.ShapeDtypeStruct((B, S, D), F32)] + out_shape
        out_specs = [row_spec(D)] + out_specs
    outs = pl.pallas_call(
        functools.partial(_proj_body, pending=pending, lc_tiles=lc // TM, nb=B),
        grid=(B, nt), in_specs=in_specs, out_specs=out_specs, out_shape=out_shape,
        compiler_params=_cparams(("parallel", "parallel")),
        name="proj",
    )(*ins)
    if pending:
        return tuple(outs)
    return (x,) + tuple(outs)


def _chunk_of_step(d, s, nc, nc_ctx):
    bwd = jnp.where(s < nc_ctx, nc_ctx - 1 - s, nc - 1 - (s - nc_ctx))
    return jnp.where(d == 0, s, bwd)


def _hg_body(p_ref, lbraw_ref, o_ref, st_ref, q_s, k_s, i_s, b_s, *, layer):
    d = pl.program_id(1)
    s = pl.program_id(2)

    @pl.when(s == 0)
    def _():
        st_ref[...] = jnp.zeros_like(st_ref)

    raw = lbraw_ref[...]
    nl = raw.shape[0]
    mx = raw[0]
    for j in range(1, nl):
        mx = jnp.maximum(mx, raw[j])
    ex = [jnp.exp(raw[j] - mx) for j in range(nl)]
    den = ex[0]
    for j in range(1, nl):
        den = den + ex[j]
    lb2 = jnp.zeros_like(mx)
    for j in range(1, layer + 1):
        lb2 = lb2 + ex[j] / den
    lb = jnp.where(d == 0, lb2[0:1], lb2[1:2])

    p = p_ref[0]
    q = p[:, 0:GW]
    z = jnp.where(d == 0, p[:, GW:2 * GW], p[:, 2 * GW:3 * GW])
    f = lb + (1.0 - lb) * _sigmoid(z)
    logf = jnp.log(jnp.maximum(f, HG_F_FLOOR))
    key = (1.0 - lb) * _sigmoid(-z)

    ti = _iota((CH, CH), 0)
    si = _iota((CH, CH), 1)
    sgn = 1 - 2 * d
    tri = ((ti // HB) == (si // HB)) & (sgn * (ti - si) >= 0)
    q_s[...] = _silu(q)
    k_s[...] = key
    i_s[...] = p[:, 3 * GW:4 * GW]
    b_s[...] = _mm(tri.astype(F32), logf, mode="f32")

    bd = _head_block_mask(GW)
    bdf = bd.astype(BF16)
    rows = _iota((HB, 1), 0)
    last = jnp.where(d == 0, HB - 1, 0)
    for j in range(CH // HB):
        blk = jnp.where(d == 0, j, CH // HB - 1 - j)
        r0 = pl.multiple_of(blk * HB, HB)
        qb = q_s[pl.ds(r0, HB), :]
        kb = k_s[pl.ds(r0, HB), :]
        ib = i_s[pl.ds(r0, HB), :]
        bb = b_s[pl.ds(r0, HB), :]
        tot = b_s[pl.ds(r0 + last, 1), :]
        st = st_ref[...]
        o_blk = _mm(qb * jnp.exp(bb), st, _NT)
        es = []
        for sidx in range(HB):
            m = sgn * (rows - sidx) >= 0
            dec = jnp.exp(jnp.where(m, bb - bb[sidx:sidx + 1], -jnp.inf))
            es.append(qb * kb[sidx:sidx + 1] * dec)
        att = _mm(jnp.concatenate(es, axis=0), bdf)
        for sidx in range(HB):
            o_blk = o_blk + att[sidx * HB:(sidx + 1) * HB] * ib[sidx:sidx + 1]
        o_ref[0, 0, pl.ds(r0, HB), :] = o_blk
        outer = _mm(ib, kb * jnp.exp(tot - bb), _TN)
        st_ref[...] = jnp.where(bd, st * jnp.exp(tot) + outer, 0.0)


def _hgrn2(p_hg, hg_lb, layer, lc):
    B, S, _ = p_hg.shape
    nc = S // CH
    nc_ctx = lc // CH
    cmap = lambda b, d, s: (b, _chunk_of_step(d, s, nc, nc_ctx), 0)
    return pl.pallas_call(
        functools.partial(_hg_body, layer=layer),
        grid=(B, 2, nc),
        in_specs=[pl.BlockSpec((1, CH, 5 * GW), cmap),
                  pl.BlockSpec(hg_lb.shape, lambda b, d, s: (0, 0, 0))],
        out_specs=pl.BlockSpec((1, 1, CH, GW), lambda b, d, s: (b, d, _chunk_of_step(d, s, nc, nc_ctx), 0)),
        out_shape=jax.ShapeDtypeStruct((B, 2, S, GW), F32),
        scratch_shapes=[pltpu.VMEM((GW, GW), F32)] + [pltpu.VMEM((CH, GW), F32)] * 4,
        compiler_params=_cparams(("parallel", "parallel", "arbitrary")),
        name="hgrn2",
    )(p_hg, hg_lb)


def _rwprep_body(cur_ref, prv_ref, nxt_ref, mu_ref, w0_ref, w2_ref, a0_ref, a2_ref, g2_ref, kk_ref, ka_ref,
                 o_ref, buf, *, lc_tiles):
    i = pl.program_id(1)
    nt = pl.num_programs(1)
    first = (i == 0) | (i == lc_tiles)
    lastt = (i == lc_tiles - 1) | (i == nt - 1)
    buf[0:8, :] = jnp.where(first, 0.0, prv_ref[0])
    buf[8:8 + TM, :] = cur_ref[0]
    buf[8 + TM:16 + TM, :] = jnp.where(lastt, 0.0, nxt_ref[0])
    cur = buf[8:8 + TM, :]
    prev = buf[7:7 + TM, :]
    nxt = buf[9:9 + TM, :]
    sft = cur + mu_ref[0:1, :] * (prev - cur) + mu_ref[1:2, :] * (nxt - cur)
    r = sft[:, 0:GW]
    k = sft[:, GW:2 * GW]
    v = sft[:, 2 * GW:3 * GW]
    ywf = sft[:, 768:896]
    ywb = sft[:, 896:1024]
    ya = sft[:, 1024:1152]
    yg = sft[:, 1152:1280]
    c = float(np.exp(-0.5))
    lwf = -c * _sigmoid(w0_ref[0:1, :] + _mm(jnp.tanh(ywf), w2_ref[0]))
    lwb = -c * _sigmoid(w0_ref[1:2, :] + _mm(jnp.tanh(ywb), w2_ref[1]))
    a = _sigmoid(a0_ref[...] + _mm(ya, a2_ref[...]))
    kk0 = k * kk_ref[...]
    ss = _mm(kk0 * kk0, _head_block_mask(GW).astype(F32), mode="x3")
    kk = kk0 / jnp.maximum(jnp.sqrt(ss), 1e-12)
    kp = k * (1.0 + (a - 1.0) * ka_ref[...])
    g = _mm(_sigmoid(yg), g2_ref[...])
    o_ref[0, :, 0:GW] = r
    o_ref[0, :, GW:2 * GW] = lwf
    o_ref[0, :, 2 * GW:3 * GW] = lwb
    o_ref[0, :, 3 * GW:4 * GW] = kp
    o_ref[0, :, 4 * GW:5 * GW] = v
    o_ref[0, :, 5 * GW:6 * GW] = kk
    o_ref[0, :, 6 * GW:7 * GW] = kk * a
    o_ref[0, :, 7 * GW:8 * GW] = g


def _rwprep(p_rw, w, lc):
    B, S, n = p_rw.shape
    nt = S // TM
    t8 = TM // 8
    full = lambda a: pl.BlockSpec(a.shape, lambda b, i: (0,) * a.ndim)
    small = [w["mu"], w["w0"], w["w2"], w["a0"], w["a2"], w["g2"], w["k_k"], w["k_a"]]
    return pl.pallas_call(
        functools.partial(_rwprep_body, lc_tiles=lc // TM),
        grid=(B, nt),
        in_specs=[pl.BlockSpec((1, TM, n), lambda b, i: (b, i, 0)),
                  pl.BlockSpec((1, 8, n), lambda b, i: (b, jnp.maximum(i * t8 - 1, 0), 0)),
                  pl.BlockSpec((1, 8, n), lambda b, i: (b, jnp.minimum((i + 1) * t8, S // 8 - 1), 0))]
                 + [full(a) for a in small],
        out_specs=pl.BlockSpec((1, TM, 8 * GW), lambda b, i: (b, i, 0)),
        out_shape=jax.ShapeDtypeStruct((B, S, 8 * GW), F32),
        scratch_shapes=[pltpu.VMEM((TM + 16, n), F32)],
        compiler_params=_cparams(("parallel", "parallel")),
        name="rwkv_prep",
    )(p_rw, p_rw, p_rw, *small)


def _rw_body(x_ref, o_ref, h_ref):
    d = pl.program_id(1)
    s = pl.program_id(2)

    @pl.when(s == 0)
    def _():
        h_ref[...] = jnp.zeros_like(h_ref)

    x = x_ref[0]
    r = x[:, 0:GW]
    lw = jnp.where(d == 0, x[:, GW:2 * GW], x[:, 2 * GW:3 * GW])
    k = x[:, 3 * GW:4 * GW]
    v = x[:, 4 * GW:5 * GW]
    kk = x[:, 5 * GW:6 * GW]
    bb = x[:, 6 * GW:7 * GW]

    ti = _iota((CH, CH), 0)
    si = _iota((CH, CH), 1)
    sgn = 1 - 2 * d
    tri = (sgn * (ti - si) >= 0).astype(F32)
    incl = _mm(tri, lw, mode="f32")
    tot = jnp.where(d == 0, incl[CH - 1:CH], incl[0:1])
    e_in = jnp.exp(incl)
    e_ni = jnp.exp(-incl)
    e_tt = jnp.exp(tot - incl)
    km = kk * jnp.exp(incl - lw)
    rt = r * e_in
    bh = bb * e_ni
    kh = k * e_ni
    bt = bb * e_tt
    kt = k * e_tt

    n = NH * CH
    hm = (_iota((n, GW), 0) // CH) == (_iota((n, GW), 1) // HD)
    tile4 = lambda a: jnp.concatenate([a] * NH, axis=0)
    expand = lambda a: jnp.where(hm, tile4(a), 0.0)
    km_e = expand(km)
    rt_e = expand(rt)
    v_e = expand(v)
    bt_e = expand(bt)
    kt_e = expand(kt)
    bh_t = tile4(bh)
    kh_t = tile4(kh)

    rr = _iota((n, n), 0)
    cc = _iota((n, n), 1)
    same = (rr // CH) == (cc // CH)
    rt_i = rr % CH
    ct_i = cc % CH
    before = same & (sgn * (rt_i - ct_i) > 0)
    upto = same & (sgn * (rt_i - ct_i) >= 0)
    eye = rr == cc

    hp = "x3"
    a_m = jnp.where(before, _mm(km_e, bh_t, _NT, hp), 0.0)
    b_m = jnp.where(before, _mm(km_e, kh_t, _NT, hp), 0.0)
    mrb = jnp.where(upto, _mm(rt_e, bh_t, _NT, hp), 0.0)
    mrk = jnp.where(upto, _mm(rt_e, kh_t, _NT, hp), 0.0)

    nm = -a_m
    tm = jnp.where(eye, 1.0, 0.0) + nm
    xp = nm
    for _ in range(5):
        xp = _mm(xp, xp, mode=hp)
        tm = tm + _mm(tm, xp, mode=hp)

    p1 = _mm(tm, km_e, mode=hp)
    q1 = _mm(tm, _mm(b_m, v_e, mode=hp), mode=hp)
    p2e = rt_e - _mm(mrb, p1, mode=hp)
    q2e = _mm(mrk, v_e, mode=hp) - _mm(mrb, q1, mode=hp)
    fold = lambda a: a[0:CH] + a[CH:2 * CH] + a[2 * CH:3 * CH] + a[3 * CH:4 * CH]
    p2 = fold(p2e)
    q2 = fold(q2e)
    g_m = jnp.where(eye, jnp.exp(tot), 0.0) - _mm(bt_e, p1, _TN, hp)
    f_m = _mm(kt_e, v_e, _TN, hp) - _mm(bt_e, q1, _TN, hp)

    h = h_ref[...]
    o_ref[0, 0] = _mm(p2, h, mode=hp) + q2
    h_ref[...] = _mm(g_m, h, mode=hp) + f_m


def _rwscan(prep, lc):
    B, S, n = prep.shape
    nc = S // CH
    nc_ctx = lc // CH
    return pl.pallas_call(
        _rw_body,
        grid=(B, 2, nc),
        in_specs=[pl.BlockSpec((1, CH, n), lambda b, d, s: (b, _chunk_of_step(d, s, nc, nc_ctx), 0))],
        out_specs=pl.BlockSpec((1, 1, CH, GW), lambda b, d, s: (b, d, _chunk_of_step(d, s, nc, nc_ctx), 0)),
        out_shape=jax.ShapeDtypeStruct((B, 2, S, GW), F32),
        scratch_shapes=[pltpu.VMEM((GW, GW), F32)],
        compiler_params=_cparams(("parallel", "parallel", "arbitrary")),
        name="rwkv_scan",
    )(prep)


def _swa_body(cur_ref, prv_ref, nxt_ref, ctx_ref, sink_ref, o_ref, *, lc_blocks, nbl):
    j = pl.program_id(1)
    w = SWA_W
    is_lat = j >= lc_blocks
    nblk = j - lc_blocks
    lane_h = _iota((w, GW), 1) // HD
    q = cur_ref[0, :, 0:GW]
    qs = jnp.concatenate([jnp.where(lane_h == h, q, jnp.zeros_like(q)) for h in range(NH)], axis=0)
    kloc = jnp.concatenate([prv_ref[0, :, GW:2 * GW], cur_ref[0, :, GW:2 * GW], nxt_ref[0, :, GW:2 * GW]], axis=0)
    vloc = jnp.concatenate([prv_ref[0, :, 2 * GW:3 * GW], cur_ref[0, :, 2 * GW:3 * GW],
                            nxt_ref[0, :, 2 * GW:3 * GW]], axis=0)
    kc = ctx_ref[0, :, GW:2 * GW]
    vc = ctx_ref[0, :, 2 * GW:3 * GW]
    scale = HD ** -0.5
    s_loc = _mm(qs, kloc, _NT) * scale
    s_ctx = _mm(qs, kc, _NT) * scale
    qi = _iota((NH * w, 3 * w), 0) % w
    col = _iota((NH * w, 3 * w), 1)
    seg = col // w
    kj = col % w
    valid = ((seg == 0) & (nblk >= 1) & (kj >= qi)) | (seg == 1) | ((seg == 2) & (nblk < nbl - 1) & (kj <= qi))
    valid = valid & is_lat
    s_loc = jnp.where(valid, s_loc, -jnp.inf)
    hrow = _iota((NH * w, 1), 0) // w
    sink = jnp.zeros((NH * w, 1), F32)
    for h in range(NH):
        sink = jnp.where(hrow == h, sink_ref[h:h + 1, 0:1], sink)
    m = jnp.maximum(jnp.maximum(jnp.max(s_loc, axis=-1, keepdims=True), jnp.max(s_ctx, axis=-1, keepdims=True)), sink)
    p_loc = jnp.exp(s_loc - m)
    p_ctx = jnp.exp(s_ctx - m)
    den = jnp.sum(p_loc, axis=-1, keepdims=True) + jnp.sum(p_ctx, axis=-1, keepdims=True) + jnp.exp(sink - m)
    of = (_mm(p_loc, vloc) + _mm(p_ctx, vc)) / den
    out = jnp.zeros((w, GW), F32)
    for h in range(NH):
        out = out + jnp.where(lane_h == h, of[h * w:(h + 1) * w], 0.0)
    o_ref[0] = out


def _swa(p_swa, sink8, lc):
    B, S, n = p_swa.shape
    w = SWA_W
    nb = S // w
    lcb = lc // w
    return pl.pallas_call(
        functools.partial(_swa_body, lc_blocks=lcb, nbl=nb - lcb),
        grid=(B, nb),
        in_specs=[pl.BlockSpec((1, w, n), lambda b, j: (b, j, 0)),
                  pl.BlockSpec((1, w, n), lambda b, j: (b, jnp.maximum(j - 1, 0), 0)),
                  pl.BlockSpec((1, w, n), lambda b, j: (b, jnp.minimum(j + 1, nb - 1), 0)),
                  pl.BlockSpec((1, lc, n), lambda b, j: (b, 0, 0)),
                  pl.BlockSpec(sink8.shape, lambda b, j: (0, 0))],
        out_specs=pl.BlockSpec((1, w, GW), lambda b, j: (b, j, 0)),
        out_shape=jax.ShapeDtypeStruct((B, S, GW), F32),
        compiler_params=_cparams(("parallel", "parallel")),
        name="swa",
    )(p_swa, p_swa, p_swa, p_swa, sink8)


def _mlaprep_body(p_ref, qg_ref, kg_ref, wq, wqr, wk, wv, cs_ref, sn_ref, q_ref, k_ref, v_ref):
    p = p_ref[0]
    cq = p[:, 0:256]
    ckv = p[:, 256:384]
    krs = p[:, 384:512]
    qn = cq * lax.rsqrt(jnp.sum(cq * cq, axis=-1, keepdims=True) / MLA_Q_RANK + NORM_EPS) * qg_ref[...]
    kvn = ckv * lax.rsqrt(jnp.mean(ckv * ckv, axis=-1, keepdims=True) + NORM_EPS) * kg_ref[...]
    scale = (MLA_NOPE + MLA_ROPE) ** -0.5
    cs = cs_ref[...]
    sn = sn_ref[...]
    for h in range(NH):
        qh = _mm(qn, wq[h]) * cs + _mm(qn, wqr[h]) * sn
        q_ref[0, h] = (qh * scale).astype(BF16)
        k_ref[0, h] = (_mm(kvn, wk[h]) + krs).astype(BF16)
        v_ref[0, h] = _mm(kvn, wv[h]).astype(BF16)


def _mlaprep(p_mla, w, tabs):
    B, S, n = p_mla.shape
    nt = S // TM
    full = lambda a: pl.BlockSpec(a.shape, lambda b, i: (0,) * a.ndim)
    ws = [w["qg"], w["kg"], w["wq"], w["wqr"], w["wk"], w["wv"]]
    hspec = lambda c: pl.BlockSpec((1, NH, TM, c), lambda b, i: (b, 0, i, 0))
    return pl.pallas_call(
        _mlaprep_body,
        grid=(B, nt),
        in_specs=[pl.BlockSpec((1, TM, n), lambda b, i: (b, i, 0))] + [full(a) for a in ws]
                 + [pl.BlockSpec((TM, 128), lambda b, i: (i, 0))] * 2,
        out_specs=[hspec(128), hspec(128), hspec(HD)],
        out_shape=[jax.ShapeDtypeStruct((B, NH, S, 128), BF16), jax.ShapeDtypeStruct((B, NH, S, 128), BF16),
                   jax.ShapeDtypeStruct((B, NH, S, HD), BF16)],
        compiler_params=_cparams(("parallel", "parallel")),
        name="mla_prep",
    )(p_mla, *ws, tabs["cs_q"], tabs["sn_q"])


def _mla_body(q_ref, k_ref, v_ref, o_ref, *, lc):
    i = pl.program_id(2)
    s = _mm(q_ref[0, 0], k_ref[0, 0], _NT)
    col = _iota(s.shape, 1)
    s = jnp.where((i * TM < lc) & (col >= lc), -jnp.inf, s)
    m = jnp.max(s, axis=-1, keepdims=True)
    p = jnp.exp(s - m)
    l = jnp.sum(p, axis=-1, keepdims=True)
    o_ref[0, 0] = _mm(p, v_ref[0, 0]) / l


def _mla(q, k, v, lc):
    B, H, S, _ = q.shape
    nt = S // TM
    return pl.pallas_call(
        functools.partial(_mla_body, lc=lc),
        grid=(B, H, nt),
        in_specs=[pl.BlockSpec((1, 1, TM, 128), lambda b, h, i: (b, h, i, 0)),
                  pl.BlockSpec((1, 1, S, 128), lambda b, h, i: (b, h, 0, 0)),
                  pl.BlockSpec((1, 1, S, HD), lambda b, h, i: (b, h, 0, 0))],
        out_specs=pl.BlockSpec((1, 1, TM, HD), lambda b, h, i: (b, h, i, 0)),
        out_shape=jax.ShapeDtypeStruct((B, H, S, HD), F32),
        compiler_params=_cparams(("parallel", "parallel", "parallel")),
        name="mla_attn",
    )(q, k, v)


def _outproj_body(*refs, lc_tiles, nb, moe):
    if moe:
        (x_ref, mod_ref, n2_ref, hgo_ref, hgg_ref, hgn_ref, rwo_ref, rwp_ref, rk_ref, lng_ref, lnb_ref,
         swo_ref, mlo_ref, wout_ref, rt_ref, xo_ref, h2_ref, lg_ref) = refs
    else:
        (x_ref, mod_ref, n2_ref, hgo_ref, hgg_ref, hgn_ref, rwo_ref, rwp_ref, rk_ref, lng_ref, lnb_ref,
         swo_ref, mlo_ref, wout_ref, xo_ref, h2_ref) = refs
    b = pl.program_id(0)
    i = pl.program_id(1)
    row = jnp.where(i < lc_tiles, nb, b)
    bd = _head_block_mask(GW).astype(F32)
    hsum = lambda a: _mm(a, bd, mode="x3")

    o = hgo_ref[0, 0] + hgo_ref[0, 1]
    hg = o * lax.rsqrt(hsum(o * o) / HD + NORM_EPS) * hgn_ref[...] * _silu(hgg_ref[0])

    o = rwo_ref[0, 0] + rwo_ref[0, 1]
    c = o - hsum(o) / HD
    on = c * lax.rsqrt(hsum(c * c) / HD + RW_LN_EPS) * lng_ref[...] + lnb_ref[...]
    r = rwp_ref[0, :, 0:GW]
    kp = rwp_ref[0, :, 3 * GW:4 * GW]
    v = rwp_ref[0, :, 4 * GW:5 * GW]
    g = rwp_ref[0, :, 7 * GW:8 * GW]
    rw = (on + hsum(r * kp * rk_ref[...]) * v) * g

    acc = _mm(hg, wout_ref[0:GW, :]) + _mm(rw, wout_ref[GW:2 * GW, :]) + _mm(swo_ref[0], wout_ref[2 * GW:3 * GW, :])
    for h in range(NH):
        acc = acc + _mm(mlo_ref[0, h], wout_ref[3 * GW + h * HD:3 * GW + (h + 1) * HD, :])

    x = x_ref[0] + mod_ref[pl.ds(row, 1), 2 * D:3 * D] * acc
    xo_ref[0] = x
    sh = mod_ref[pl.ds(row, 1), 3 * D:4 * D]
    sc = mod_ref[pl.ds(row, 1), 4 * D:5 * D]
    ms = jnp.mean(x * x, axis=-1, keepdims=True)
    h2 = (x * lax.rsqrt(ms + NORM_EPS) * n2_ref[...]) * (1.0 + sc) + sh
    h2_ref[0] = h2.astype(BF16)
    if moe:
        lg_ref[0] = _mm(h2, rt_ref[...], mode="f32")


def _outproj(x, mod, n2g, hgo, p_hg, hgn, rwo, rwp, rk, lng, lnb, swo, mlo, wout, router, lc):
    B, S, _ = x.shape
    nt = S // TM
    moe = router is not None
    row = lambda n: pl.BlockSpec((1, TM, n), lambda b, i: (b, i, 0))
    full = lambda a: pl.BlockSpec(a.shape, lambda b, i: (0,) * a.ndim)
    two = pl.BlockSpec((1, 2, TM, GW), lambda b, i: (b, 0, i, 0))
    ins = [x, mod, n2g, hgo, p_hg, hgn, rwo, rwp, rk, lng, lnb, swo, mlo, wout]
    in_specs = [row(D), full(mod), full(n2g), two, pl.BlockSpec((1, TM, GW), lambda b, i: (b, i, 4)), full(hgn),
                two, row(8 * GW), full(rk), full(lng), full(lnb), row(GW),
                pl.BlockSpec((1, NH, TM, HD), lambda b, i: (b, 0, i, 0)), full(wout)]
    out_shape = [jax.ShapeDtypeStruct((B, S, D), F32), jax.ShapeDtypeStruct((B, S, D), BF16)]
    out_specs = [row(D), row(D)]
    if moe:
        ins.append(router)
        in_specs.append(full(router))
        out_shape.append(jax.ShapeDtypeStruct((B, S, 128), F32))
        out_specs.append(row(128))
    return pl.pallas_call(
        functools.partial(_outproj_body, lc_tiles=lc // TM, nb=B, moe=moe),
        grid=(B, nt), in_specs=in_specs, out_specs=out_specs, out_shape=out_shape,
        compiler_params=_cparams(("parallel", "parallel")),
        name="outproj",
    )(*ins)


def _ffn_body(h_ref, w1_ref, w3_ref, w2_ref, o_ref, acc_ref):
    j = pl.program_id(1)

    @pl.when(j == 0)
    def _():
        acc_ref[...] = jnp.zeros_like(acc_ref)

    h = h_ref[...]
    a = _mm(h, w1_ref[...])
    z = _silu(a) * _mm(h, w3_ref[...])
    acc_ref[...] += _mm(z, w2_ref[...])

    @pl.when(j == pl.num_programs(1) - 1)
    def _():
        o_ref[...] = acc_ref[...]


def _ffn(h2, w1, w3, w2):
    M = h2.shape[0]
    tm = 1024
    tf = D_FF // 2
    return pl.pallas_call(
        _ffn_body,
        grid=(M // tm, D_FF // tf),
        in_specs=[pl.BlockSpec((tm, D), lambda i, j: (i, 0)),
                  pl.BlockSpec((D, tf), lambda i, j: (0, j)),
                  pl.BlockSpec((D, tf), lambda i, j: (0, j)),
                  pl.BlockSpec((tf, D), lambda i, j: (j, 0))],
        out_specs=pl.BlockSpec((tm, D), lambda i, j: (i, 0)),
        out_shape=jax.ShapeDtypeStruct((M, D), F32),
        scratch_shapes=[pltpu.VMEM((tm, D), F32)],
        compiler_params=_cparams(("parallel", "arbitrary")),
        name="ffn",
    )(h2, w1, w3, w2)


def _moe_body(h_ref, lg_ref, w1_ref, w3_ref, w2_ref, o_ref, acc_ref, comb_ref):
    e = pl.program_id(1)
    lane = _iota(comb_ref.shape, 1)

    @pl.when(e == 0)
    def _():
        acc_ref[...] = jnp.zeros_like(acc_ref)
        lg = jnp.where(lane < N_EXPERTS, lg_ref[...], -jnp.inf)
        v1 = jnp.max(lg, axis=-1, keepdims=True)
        i1 = jnp.min(jnp.where(lg == v1, lane, 128), axis=-1, keepdims=True)
        lg2 = jnp.where(lane == i1, -jnp.inf, lg)
        v2 = jnp.max(lg2, axis=-1, keepdims=True)
        i2 = jnp.min(jnp.where(lg2 == v2, lane, 128), axis=-1, keepdims=True)
        ex = jnp.exp(v2 - v1)
        comb_ref[...] = jnp.where(lane == i1, 1.0 / (1.0 + ex), 0.0) + jnp.where(lane == i2, ex / (1.0 + ex), 0.0)

    ce = jnp.sum(jnp.where(lane == e, comb_ref[...], 0.0), axis=-1, keepdims=True)
    h = h_ref[...]
    a = _mm(h, w1_ref[0])
    z = _silu(a) * _mm(h, w3_ref[0])
    acc_ref[...] += ce * _mm(z, w2_ref[0])

    @pl.when(e == pl.num_programs(1) - 1)
    def _():
        o_ref[...] = acc_ref[...]


def _moe(h2, logits, w1, w3, w2):
    M = h2.shape[0]
    tm = 1024
    f = w1.shape[2]
    return pl.pallas_call(
        _moe_body,
        grid=(M // tm, N_EXPERTS),
        in_specs=[pl.BlockSpec((tm, D), lambda i, e: (i, 0)),
                  pl.BlockSpec((tm, 128), lambda i, e: (i, 0)),
                  pl.BlockSpec((1, D, f), lambda i, e: (e, 0, 0)),
                  pl.BlockSpec((1, D, f), lambda i, e: (e, 0, 0)),
                  pl.BlockSpec((1, f, D), lambda i, e: (e, 0, 0))],
        out_specs=pl.BlockSpec((tm, D), lambda i, e: (i, 0)),
        out_shape=jax.ShapeDtypeStruct((M, D), F32),
        scratch_shapes=[pltpu.VMEM((tm, D), F32), pltpu.VMEM((tm, 128), F32)],
        compiler_params=_cparams(("parallel", "arbitrary")),
        name="moe",
    )(h2, logits, w1, w3, w2)


def _final_body(x_ref, y_ref, mod_ref, g_ref, o_ref):
    b = pl.program_id(0)
    x = x_ref[0] + mod_ref[pl.ds(b, 1), 5 * D:6 * D] * y_ref[0]
    ms = jnp.mean(x * x, axis=-1, keepdims=True)
    o_ref[0] = x * lax.rsqrt(ms + NORM_EPS) * g_ref[...]


def _final(x, y, mod, g, lc):
    B, S, _ = x.shape
    T = S - lc
    off = lc // TM
    return pl.pallas_call(
        _final_body,
        grid=(B, T // TM),
        in_specs=[pl.BlockSpec((1, TM, D), lambda b, i: (b, i + off, 0)),
                  pl.BlockSpec((1, TM, D), lambda b, i: (b, i + off, 0)),
                  pl.BlockSpec(mod.shape, lambda b, i: (0, 0)),
                  pl.BlockSpec(g.shape, lambda b, i: (0, 0))],
        out_specs=pl.BlockSpec((1, TM, D), lambda b, i: (b, i, 0)),
        out_shape=jax.ShapeDtypeStruct((B, T, D), F32),
        compiler_params=_cparams(("parallel", "parallel")),
        name="final_norm",
    )(x, y, mod, g)


def _rot_perm(width, group):
    j = np.arange(width)
    half = group // 2
    return np.where(j % group < half, j + half, j - half)


def _layer_weights(l, w_in, rw_mu, rw_w0, rw_w2, rw_a0, rw_a2, rw_g2, rw_k_k, rw_k_a, rw_r_k, rw_ln_g, rw_ln_b,
                   mla_qnorm_g, mla_wuq, mla_kvnorm_g, mla_wukv):
    wi = w_in[l]
    zc = lambda n: jnp.zeros((D, n), F32)
    o = 0
    w_hg = wi[:, o:o + 5 * GW]
    o += 5 * GW
    rwc = wi[:, o:o + 1088]
    o += 1088
    swc = wi[:, o:o + 512]
    o += 512
    mlc = wi[:, o:o + 352]

    def pad_rw(a, axis):
        z = lambda n: jnp.zeros(a.shape[:axis] + (n,) + a.shape[axis + 1:], a.dtype)
        sl = lambda s, e: lax.slice_in_dim(a, s, e, axis=axis)
        return jnp.concatenate([sl(0, 768), sl(768, 832), z(64), sl(832, 896), z(64), sl(896, 960), z(64),
                                sl(960, 1088)], axis=axis)

    w_rw = pad_rw(rwc, 1)
    mu = pad_rw(rw_mu[l], 1)
    pad_rows = lambda a: jnp.concatenate([a, jnp.zeros((64, a.shape[1]), a.dtype)], axis=0)
    w2 = jnp.stack([pad_rows(rw_w2[l, 0]), pad_rows(rw_w2[l, 1])]).astype(BF16)
    a2 = pad_rows(rw_a2[l]).astype(BF16)

    q = swc[:, 0:GW]
    kx = jnp.concatenate([swc[:, 256:320], swc[:, 256:320], swc[:, 320:384], swc[:, 320:384]], axis=1)
    vx = jnp.concatenate([swc[:, 384:448], swc[:, 384:448], swc[:, 448:512], swc[:, 448:512]], axis=1)
    perm = _rot_perm(GW, 32)
    w_swa = jnp.concatenate([q, kx, vx, q[:, perm], kx[:, perm]], axis=1)

    kr = mlc[:, 320:352]
    krp = kr[:, _rot_perm(MLA_ROPE, 16)]
    w_mla = jnp.concatenate([mlc[:, 0:192], zc(64), mlc[:, 192:320], zc(64), kr, zc(32), zc(64), krp, zc(32)], axis=1)

    wuq = mla_wuq[l]
    wukv = mla_wukv[l]
    rp = _rot_perm(MLA_ROPE, 16)
    wq, wqr, wk, wv = [], [], [], []
    zq = lambda r, n: jnp.zeros((r, n), F32)
    for h in range(NH):
        qh = wuq[:, 96 * h:96 * (h + 1)]
        a = jnp.concatenate([qh, zq(192, 32)], axis=1)
        ar = jnp.concatenate([zq(192, 64), qh[:, 64:96][:, rp], zq(192, 32)], axis=1)
        wq.append(jnp.concatenate([a, zq(64, 128)], axis=0))
        wqr.append(jnp.concatenate([ar, zq(64, 128)], axis=0))
        wk.append(jnp.concatenate([wukv[:, 128 * h:128 * h + 64], zq(128, 64)], axis=1))
        wv.append(wukv[:, 128 * h + 64:128 * (h + 1)])
    row = lambda a: a.reshape(1, -1)
    return {
        "proj": {"hg": w_hg.astype(BF16), "rw": w_rw.astype(BF16), "swa": w_swa.astype(BF16),
                 "mla": w_mla.astype(BF16)},
        "rw": {"mu": mu, "w0": rw_w0[l], "w2": w2, "a0": row(rw_a0[l]), "a2": a2, "g2": rw_g2[l].astype(BF16),
               "k_k": row(rw_k_k[l]), "k_a": row(rw_k_a[l])},
        "rk": row(rw_r_k[l]), "lng": row(rw_ln_g[l]), "lnb": row(rw_ln_b[l]),
        "mla": {"qg": jnp.concatenate([row(mla_qnorm_g[l]), jnp.zeros((1, 64), F32)], axis=1),
                "kg": row(mla_kvnorm_g[l]),
                "wq": jnp.stack(wq).astype(BF16), "wqr": jnp.stack(wqr).astype(BF16),
                "wk": jnp.stack(wk).astype(BF16), "wv": jnp.stack(wv).astype(BF16)},
    }


def _rope_tables(T, lc):
    rows = T // GRID_W
    rowp = jnp.repeat(jnp.arange(rows, dtype=F32), GRID_W)
    colp = jnp.tile(jnp.arange(GRID_W, dtype=F32), rows)

    def table(rot_dim, width, lane0):
        nf = rot_dim // 4
        inv = ROPE_BASE ** (-jnp.arange(nf, dtype=F32) / nf)
        j = np.arange(rot_dim)
        ax = j // (2 * nf)
        fr = j % nf
        sign = np.where(j % (2 * nf) < nf, -1.0, 1.0).astype(np.float32)
        pos = jnp.stack([rowp, colp], axis=1)
        ang = pos[:, ax] * inv[fr][None, :]
        cos = jnp.concatenate([jnp.ones((lc, rot_dim), F32), jnp.cos(ang)], axis=0)
        sin = jnp.concatenate([jnp.zeros((lc, rot_dim), F32), jnp.sin(ang) * sign[None, :]], axis=0)
        reps = (width - lane0) // rot_dim if lane0 == 0 else 1
        cos = jnp.tile(cos, (1, reps))
        sin = jnp.tile(sin, (1, reps))
        return cos, sin

    cs_s, sn_s = table(HD, 2 * GW, 0)
    c32, s32 = table(MLA_ROPE, MLA_ROPE, 0)
    S = lc + T
    z = lambda n: jnp.zeros((S, n), F32)
    o = lambda n: jnp.ones((S, n), F32)
    cs_m = jnp.concatenate([z(64), c32, z(32)], axis=1)
    sn_m = jnp.concatenate([z(64), s32, z(32)], axis=1)
    cs_q = jnp.concatenate([o(64), c32, z(32)], axis=1)
    sn_q = jnp.concatenate([z(64), s32, z(32)], axis=1)
    return {"cs_s": cs_s, "sn_s": sn_s, "cs_m": cs_m, "sn_m": sn_m, "cs_q": cs_q, "sn_q": sn_q}


def kernel(x, c, ctx, c_ctx, w_ada, b_ada, norm1_g, norm2_g, w_in, hg_lb, hg_norm_g, rw_mu, rw_w0, rw_w2, rw_a0,
           rw_a2, rw_g2, rw_k_k, rw_k_a, rw_r_k, rw_ln_g, rw_ln_b, swa_sink, mla_qnorm_g, mla_wuq, mla_kvnorm_g,
           mla_wukv, w_out, ffn_w1, ffn_w3, ffn_w2, moe_router, moe_w1, moe_w3, moe_w2, final_norm_g):
    B, T, _ = x.shape
    lc = ctx.shape[1]
    S = lc + T
    L = w_in.shape[0]
    assert lc % TM == 0 and T % TM == 0 and lc % SWA_W == 0 and T % GRID_W == 0 and (B * S) % 1024 == 0

    r8 = -(-(B + 1) // 8) * 8
    c8 = jnp.concatenate([c, c_ctx[None, :], jnp.zeros((r8 - B - 1, D), F32)], axis=0)
    mod_all = _ada(c8, w_ada, b_ada)
    tabs = _rope_tables(T, lc)
    xs = jnp.concatenate([ctx, x], axis=1)
    row = lambda a: a.reshape(1, -1)

    pend = None
    for l in range(L):
        w = _layer_weights(l, w_in, rw_mu, rw_w0, rw_w2, rw_a0, rw_a2, rw_g2, rw_k_k, rw_k_a, rw_r_k, rw_ln_g,
                           rw_ln_b, mla_qnorm_g, mla_wuq, mla_kvnorm_g, mla_wukv)
        mod = mod_all[l]
        xs, p_hg, p_rw, p_swa, p_mla = _proj(xs, pend, mod, row(norm1_g[l]), w["proj"], tabs, lc)

        hgo = _hgrn2(p_hg, hg_lb, l, lc)
        rwp = _rwprep(p_rw, w["rw"], lc)
        rwo = _rwscan(rwp, lc)
        sink8 = jnp.zeros((8, 128), F32).at[0:NH, :].set(jnp.broadcast_to(swa_sink[l][:, None], (NH, 128)))
        swo = _swa(p_swa, sink8, lc)
        mq, mk, mv = _mlaprep(p_mla, w["mla"], tabs)
        mlo = _mla(mq, mk, mv, lc)

        moe = l % 2 == 1
        router = None
        if moe:
            router = jnp.concatenate([moe_router[l // 2], jnp.zeros((D, 128 - N_EXPERTS), F32)], axis=1)
        outs = _outproj(xs, mod, row(norm2_g[l]), hgo, p_hg, row(hg_norm_g[l]), rwo, rwp, w["rk"], w["lng"],
                        w["lnb"], swo, mlo, w_out[l].astype(BF16), router, lc)
        xs, h2 = outs[0], outs[1]
        h2f = h2.reshape(B * S, D)
        if moe:
            e = l // 2
            y = _moe(h2f, outs[2].reshape(B * S, 128), moe_w1[e].astype(BF16), moe_w3[e].astype(BF16),
                     moe_w2[e].astype(BF16))
        else:
            e = l // 2
            y = _ffn(h2f, ffn_w1[e].astype(BF16), ffn_w3[e].astype(BF16), ffn_w2[e].astype(BF16))
        pend = (y.reshape(B, S, D), mod)

    return _final(xs, pend[0], pend[1], row(final_norm_g), lc)
```

```python
import functools

import jax
import jax.numpy as jnp
import numpy as np
from jax import lax
from jax.experimental import pallas as pl
from jax.experimental.pallas import tpu as pltpu

F32 = jnp.float32
BF16 = jnp.bfloat16

D = 1024
GW = 256
HD = 64
NH = 4
GRID_W = 64
NORM_EPS = 1e-6
ROPE_BASE = 10000.0
HG_F_FLOOR = 1e-30
RW_LN_EPS = 64e-5
SWA_W = 128
MLA_Q_RANK = 192
MLA_KV_RANK = 128
MLA_NOPE = 64
MLA_ROPE = 32
D_FF = 2816
N_EXPERTS = 8
D_FF_EXPERT = 1408

TM = 256
CH = 64
HB = 16
MLA_HPS = 2
MLA_KCHUNK = 256
VMEM_LIMIT = 56 * 1024 * 1024
RW_GRAM_MODE = "bf16"
RW_BASE_MODE = "bf16"
RW_MERGE_MODE = "bf16"
RW_STATE_MODE = "x3"

_NN = (((1,), (0,)), ((), ()))
_NT = (((1,), (1,)), ((), ()))
_TN = (((0,), (0,)), ((), ()))


def _mm(a, b, dims=_NN, mode="bf16"):
    if mode == "bf16":
        return lax.dot_general(a.astype(BF16), b.astype(BF16), dims, preferred_element_type=F32)
    if mode == "x3":
        ah = a.astype(BF16)
        al = (a - ah.astype(F32)).astype(BF16)
        bh = b.astype(BF16)
        bl = (b - bh.astype(F32)).astype(BF16)
        d = lambda x, y: lax.dot_general(x, y, dims, preferred_element_type=F32)
        return d(ah, bh) + (d(ah, bl) + d(al, bh))
    return lax.dot_general(a, b, dims, precision=lax.Precision.HIGHEST, preferred_element_type=F32)


def _sigmoid(x):
    return 1.0 / (1.0 + jnp.exp(-x))


def _silu(x):
    return x / (1.0 + jnp.exp(-x))


def _iota(shape, dim):
    return lax.broadcasted_iota(jnp.int32, shape, dim)


def _head_block_mask(n):
    return (_iota((n, n), 0) // HD) == (_iota((n, n), 1) // HD)


def _alternate(*stages, skew=0):
    live = list(enumerate(stages))
    rnd = 0
    while live:
        for item in list(live):
            i, g = item
            if rnd < i * skew:
                continue
            try:
                next(g)
            except StopIteration:
                live.remove(item)
        rnd += 1


def _cparams(sem):
    return pltpu.CompilerParams(dimension_semantics=sem, vmem_limit_bytes=VMEM_LIMIT)


def _ada_body(c_ref, w_ref, b_ref, o_ref):
    s = _silu(c_ref[...])
    o_ref[0] = _mm(s, w_ref[0]) + b_ref[0]


def _ada(c8, w_ada, b_ada):
    L = w_ada.shape[0]
    r8 = c8.shape[0]
    tn = 1536
    return pl.pallas_call(
        _ada_body,
        grid=(L, 6 * D // tn),
        in_specs=[
            pl.BlockSpec((r8, D), lambda l, j: (0, 0)),
            pl.BlockSpec((1, D, tn), lambda l, j: (l, 0, j)),
            pl.BlockSpec((1, 1, tn), lambda l, j: (l, 0, j)),
        ],
        out_specs=pl.BlockSpec((1, r8, tn), lambda l, j: (l, 0, j)),
        out_shape=jax.ShapeDtypeStruct((L, r8, 6 * D), F32),
        compiler_params=_cparams(("parallel", "parallel")),
        name="adaln",
    )(c8, w_ada, b_ada.reshape(L, 1, 6 * D))


def _proj_body(*refs, pending, lc_tiles, nb):
    if pending:
        (x_ref, y_ref, modp_ref, mod_ref, g_ref, wall, cs_s, sn_s, cs_m, sn_m,
         xo_ref, phg, prw, pswa, pmla) = refs
    else:
        (x_ref, mod_ref, g_ref, wall, cs_s, sn_s, cs_m, sn_m,
         phg, prw, pswa, pmla) = refs
    b = pl.program_id(0)
    i = pl.program_id(1)
    row = jnp.where(i < lc_tiles, nb, b)
    x = x_ref[0]
    if pending:
        gate = modp_ref[pl.ds(row, 1), 5 * D:6 * D]
        x = x + gate * y_ref[0]
        xo_ref[0] = x
    sh = mod_ref[pl.ds(row, 1), 0:D]
    sc = mod_ref[pl.ds(row, 1), D:2 * D]
    ms = jnp.mean(x * x, axis=-1, keepdims=True)
    h = (x * lax.rsqrt(ms + NORM_EPS) * g_ref[...]) * (1.0 + sc) + sh
    hb = h.astype(BF16)
    n1 = 5 * GW
    phg[0] = _mm(hb, wall[:, 0:n1])
    prw[0] = _mm(hb, wall[:, n1:2 * n1])
    s = _mm(hb, wall[:, 2 * n1:3 * n1])
    qk = s[:, 0:2 * GW] * cs_s[...] + s[:, 3 * GW:5 * GW] * sn_s[...]
    pswa[0, :, 0:2 * GW] = qk.astype(BF16)
    pswa[0, :, 2 * GW:3 * GW] = s[:, 2 * GW:3 * GW].astype(BF16)
    m = _mm(hb, wall[:, 3 * n1:3 * n1 + 640])
    pmla[0, :, 0:384] = m[:, 0:384]
    pmla[0, :, 384:512] = m[:, 384:512] * cs_m[...] + m[:, 512:640] * sn_m[...]


def _proj(x, pend, mod, g, w, tabs, lc):
    B, S, _ = x.shape
    nt = S // TM
    pending = pend is not None
    row_spec = lambda n: pl.BlockSpec((1, TM, n), lambda b, i: (b, i, 0))
    full = lambda a: pl.BlockSpec(a.shape, lambda b, i: (0,) * a.ndim)
    tab_spec = lambda a: pl.BlockSpec((TM, a.shape[1]), lambda b, i: (i, 0))
    ins = [x]
    in_specs = [row_spec(D)]
    if pending:
        ins += [pend[0], pend[1]]
        in_specs += [row_spec(D), full(pend[1])]
    ins += [mod, g, w, tabs["cs_s"], tabs["sn_s"], tabs["cs_m"], tabs["sn_m"]]
    in_specs += [full(mod), full(g), full(w),
                 tab_spec(tabs["cs_s"]), tab_spec(tabs["sn_s"]), tab_spec(tabs["cs_m"]), tab_spec(tabs["sn_m"])]
    out_shape = [jax.ShapeDtypeStruct((B, S, 5 * GW), F32), jax.ShapeDtypeStruct((B, S, 5 * GW), F32),
                 jax.ShapeDtypeStruct((B, S, 3 * GW), BF16), jax.ShapeDtypeStruct((B, S, 512), F32)]
    out_specs = [row_spec(5 * GW), row_spec(5 * GW), row_spec(3 * GW), row_spec(512)]
    if pending:
        out_shape = [jax.ShapeDtypeStruct((B, S, D), F32)] + out_shape
        out_specs = [row_spec(D)] + out_specs
    outs = pl.pallas_call(
        functools.partial(_proj_body, pending=pending, lc_tiles=lc // TM, nb=B),
        grid=(B, nt), in_specs=in_specs, out_specs=out_specs, out_shape=out_shape,
        compiler_params=_cparams(("parallel", "parallel")),
        name="proj",
    )(*ins)
    if pending:
        return tuple(outs)
    return (x,) + tuple(outs)


def _chunk_of_step(d, s, nc, nc_ctx):
    bwd = jnp.where(s < nc_ctx, nc_ctx - 1 - s, nc - 1 - (s - nc_ctx))
    return jnp.where(d == 0, s, bwd)


def _hg_direction(p_ref, lb, o_ref, st_ref, fwd):
    p = p_ref[0]
    z = p[:, GW:2 * GW] if fwd else p[:, 2 * GW:3 * GW]
    f = lb + (1.0 - lb) * _sigmoid(z)
    logf = jnp.log(jnp.maximum(f, HG_F_FLOOR))
    key = (1.0 - lb) * _sigmoid(-z)
    qs = _silu(p[:, 0:GW])
    iv = p[:, 3 * GW:4 * GW]

    ti = _iota((CH, CH), 0)
    si = _iota((CH, CH), 1)
    tri = ((ti // HB) == (si // HB)) & ((si <= ti) if fwd else (si >= ti))
    bcum = _mm(tri.astype(F32), logf, mode="f32")

    bd = _head_block_mask(GW)
    bdf = bd.astype(BF16)
    half = HB // 2
    nblk = CH // HB
    for j in range(nblk):
        r0 = (j if fwd else nblk - 1 - j) * HB
        qb, kb, ib, bb = (a[r0:r0 + HB] for a in (qs, key, iv, bcum))
        tot = bb[HB - 1:HB] if fwd else bb[0:1]
        st = st_ref[...]
        o_blk = _mm(qb * jnp.exp(bb), st, _NT)
        es = []
        spans = []
        for sidx in range(HB):
            if fwd:
                lo, hi = (0, HB) if sidx < half else (half, HB)
            else:
                lo, hi = (0, half) if sidx < half else (0, HB)
            rows = _iota((hi - lo, 1), 0) + lo
            m = (rows >= sidx) if fwd else (rows <= sidx)
            dec = jnp.exp(jnp.where(m, bb[lo:hi] - bb[sidx:sidx + 1], -jnp.inf))
            es.append(qb[lo:hi] * kb[sidx:sidx + 1] * dec)
            spans.append((lo, hi))
        yield
        att = _mm(jnp.concatenate(es, axis=0), bdf)
        top = o_blk[0:half]
        bot = o_blk[half:HB]
        off = 0
        for sidx, (lo, hi) in enumerate(spans):
            a = att[off:off + hi - lo] * ib[sidx:sidx + 1]
            off += hi - lo
            if hi - lo == HB:
                top = top + a[0:half]
                bot = bot + a[half:HB]
            elif lo == 0:
                top = top + a
            else:
                bot = bot + a
        o_ref[0, r0:r0 + half, :] = top
        o_ref[0, r0 + half:r0 + HB, :] = bot
        outer = _mm(ib, kb * jnp.exp(tot - bb), _TN)
        st_ref[...] = jnp.where(bd, st * jnp.exp(tot) + outer, 0.0)
        yield


def _hg_body(pf_ref, pb_ref, lbraw_ref, of_ref, ob_ref, stf_ref, stb_ref, *, layer):
    @pl.when(pl.program_id(1) == 0)
    def _():
        stf_ref[...] = jnp.zeros_like(stf_ref)
        stb_ref[...] = jnp.zeros_like(stb_ref)

    raw = lbraw_ref[...]
    nl = raw.shape[0]
    mx = raw[0]
    for j in range(1, nl):
        mx = jnp.maximum(mx, raw[j])
    ex = [jnp.exp(raw[j] - mx) for j in range(nl)]
    den = ex[0]
    for j in range(1, nl):
        den = den + ex[j]
    lb2 = jnp.zeros_like(mx)
    for j in range(1, layer + 1):
        lb2 = lb2 + ex[j] / den
    _alternate(_hg_direction(pf_ref, lb2[0:1], of_ref, stf_ref, True),
               _hg_direction(pb_ref, lb2[1:2], ob_ref, stb_ref, False))


def _hgrn2(p_hg, hg_lb, layer, lc):
    B, S, _ = p_hg.shape
    nc = S // CH
    nc_ctx = lc // CH
    fmap = lambda b, s: (b, s, 0)
    bmap = lambda b, s: (b, _chunk_of_step(1, s, nc, nc_ctx), 0)
    return pl.pallas_call(
        functools.partial(_hg_body, layer=layer),
        grid=(B, nc),
        in_specs=[pl.BlockSpec((1, CH, 5 * GW), fmap), pl.BlockSpec((1, CH, 5 * GW), bmap),
                  pl.BlockSpec(hg_lb.shape, lambda b, s: (0, 0, 0))],
        out_specs=[pl.BlockSpec((1, CH, GW), fmap), pl.BlockSpec((1, CH, GW), bmap)],
        out_shape=[jax.ShapeDtypeStruct((B, S, GW), F32)] * 2,
        scratch_shapes=[pltpu.VMEM((GW, GW), F32)] * 2,
        compiler_params=_cparams(("parallel", "arbitrary")),
        name="hgrn2",
    )(p_hg, p_hg, hg_lb)


def _rwprep_body(cur_ref, prv_ref, nxt_ref, mu_ref, w0_ref, w2_ref, a0_ref, a2_ref, g2_ref, kk_ref, ka_ref,
                 o_ref, buf, *, lc_tiles):
    i = pl.program_id(1)
    nt = pl.num_programs(1)
    first = (i == 0) | (i == lc_tiles)
    lastt = (i == lc_tiles - 1) | (i == nt - 1)
    buf[0:8, :] = jnp.where(first, 0.0, prv_ref[0])
    buf[8:8 + TM, :] = cur_ref[0]
    buf[8 + TM:16 + TM, :] = jnp.where(lastt, 0.0, nxt_ref[0])
    cur = buf[8:8 + TM, :]
    prev = buf[7:7 + TM, :]
    nxt = buf[9:9 + TM, :]
    sft = cur + mu_ref[0:1, :] * (prev - cur) + mu_ref[1:2, :] * (nxt - cur)
    r = sft[:, 0:GW]
    k = sft[:, GW:2 * GW]
    v = sft[:, 2 * GW:3 * GW]
    ywf = sft[:, 768:896]
    ywb = sft[:, 896:1024]
    ya = sft[:, 1024:1152]
    yg = sft[:, 1152:1280]
    c = float(np.exp(-0.5))
    lwf = -c * _sigmoid(w0_ref[0:1, :] + _mm(jnp.tanh(ywf), w2_ref[0]))
    lwb = -c * _sigmoid(w0_ref[1:2, :] + _mm(jnp.tanh(ywb), w2_ref[1]))
    a = _sigmoid(a0_ref[...] + _mm(ya, a2_ref[...]))
    kk0 = k * kk_ref[...]
    ss = _mm(kk0 * kk0, _head_block_mask(GW).astype(F32), mode="x3")
    kk = kk0 / jnp.maximum(jnp.sqrt(ss), 1e-12)
    kp = k * (1.0 + (a - 1.0) * ka_ref[...])
    g = _mm(_sigmoid(yg), g2_ref[...])
    o_ref[0, :, 0:GW] = r
    o_ref[0, :, GW:2 * GW] = lwf
    o_ref[0, :, 2 * GW:3 * GW] = lwb
    o_ref[0, :, 3 * GW:4 * GW] = kp
    o_ref[0, :, 4 * GW:5 * GW] = v
    o_ref[0, :, 5 * GW:6 * GW] = kk
    o_ref[0, :, 6 * GW:7 * GW] = kk * a
    o_ref[0, :, 7 * GW:8 * GW] = g


def _rwprep(p_rw, w, lc):
    B, S, n = p_rw.shape
    nt = S // TM
    t8 = TM // 8
    full = lambda a: pl.BlockSpec(a.shape, lambda b, i: (0,) * a.ndim)
    small = [w["mu"], w["w0"], w["w2"], w["a0"], w["a2"], w["g2"], w["k_k"], w["k_a"]]
    return pl.pallas_call(
        functools.partial(_rwprep_body, lc_tiles=lc // TM),
        grid=(B, nt),
        in_specs=[pl.BlockSpec((1, TM, n), lambda b, i: (b, i, 0)),
                  pl.BlockSpec((1, 8, n), lambda b, i: (b, jnp.maximum(i * t8 - 1, 0), 0)),
                  pl.BlockSpec((1, 8, n), lambda b, i: (b, jnp.minimum((i + 1) * t8, S // 8 - 1), 0))]
                 + [full(a) for a in small],
        out_specs=pl.BlockSpec((1, TM, 8 * GW), lambda b, i: (b, i, 0)),
        out_shape=jax.ShapeDtypeStruct((B, S, 8 * GW), F32),
        scratch_shapes=[pltpu.VMEM((TM + 16, n), F32)],
        compiler_params=_cparams(("parallel", "parallel")),
        name="rwkv_prep",
    )(p_rw, p_rw, p_rw, *small)


def _rw_direction(x_ref, o_ref, h_ref, fwd):
    x = x_ref[0]
    r = x[:, 0:GW]
    lw = x[:, GW:2 * GW] if fwd else x[:, 2 * GW:3 * GW]
    k = x[:, 3 * GW:4 * GW]
    v = x[:, 4 * GW:5 * GW]
    kk = x[:, 5 * GW:6 * GW]
    bb = x[:, 6 * GW:7 * GW]

    ti = _iota((CH, CH), 0)
    si = _iota((CH, CH), 1)
    tri = ((si <= ti) if fwd else (si >= ti)).astype(F32)
    incl = _mm(tri, lw, mode="f32")
    tot = incl[CH - 1:CH] if fwd else incl[0:1]
    e_in = jnp.exp(incl)
    e_ni = jnp.exp(-incl)
    e_tt = jnp.exp(tot - incl)
    km = kk * jnp.exp(incl - lw)
    rt = r * e_in
    bh = bb * e_ni
    kh = k * e_ni
    bt = bb * e_tt
    kt = k * e_tt

    n = NH * CH
    hm = (_iota((n, GW), 0) // CH) == (_iota((n, GW), 1) // HD)
    tile4 = lambda a: jnp.concatenate([a] * NH, axis=0)
    expand = lambda a: jnp.where(hm, tile4(a), 0.0)
    km_e = expand(km)
    rt_e = expand(rt)
    v_e = expand(v)
    bt_e = expand(bt)
    kt_e = expand(kt)
    bh_t = tile4(bh)
    kh_t = tile4(kh)

    rr = _iota((n, n), 0)
    cc = _iota((n, n), 1)
    same = (rr // CH) == (cc // CH)
    rt_i = rr % CH
    ct_i = cc % CH
    before = same & ((ct_i < rt_i) if fwd else (ct_i > rt_i))
    upto = same & ((ct_i <= rt_i) if fwd else (ct_i >= rt_i))
    eye = rr == cc

    pg = RW_GRAM_MODE
    a_m = jnp.where(before, _mm(km_e, bh_t, _NT, pg), 0.0)
    yield

    blk = lambda m: (rr // m) == (cc // m)
    pb = RW_BASE_MODE
    nm = jnp.where(blk(8), -a_m, 0.0)
    x2 = _mm(nm, nm, mode=pb)
    b_m = jnp.where(before, _mm(km_e, kh_t, _NT, pg), 0.0)
    yield
    tm = jnp.where(eye, 1.0, 0.0) + nm
    tm = tm + _mm(tm, x2, mode=pb)
    x4 = _mm(x2, x2, mode=pb)
    yield
    tm = tm + _mm(tm, x4, mode=pb)
    mrb = jnp.where(upto, _mm(rt_e, bh_t, _NT, pg), 0.0)
    yield
    pm = RW_MERGE_MODE
    side = [lambda: jnp.where(upto, _mm(rt_e, kh_t, _NT, pg), 0.0), lambda: _mm(b_m, v_e, mode=pg),
            lambda: _mm(kt_e, v_e, _TN, pg)]
    done = []
    for m in (8, 16, 32):
        low = jnp.where(blk(2 * m) & (rr // m != cc // m), a_m, 0.0)
        y = _mm(low, tm, mode=pm)
        done.append(side.pop(0)())
        yield
        tm = tm - _mm(tm, y, mode=pm)
        yield
    mrk, bmv, ktv = done

    p1 = _mm(tm, km_e, mode=pg)
    q1 = _mm(tm, bmv, mode=pg)
    yield
    p2e = rt_e - _mm(mrb, p1, mode=pg)
    q2e = _mm(mrk, v_e, mode=pg) - _mm(mrb, q1, mode=pg)
    g_m = jnp.where(eye, jnp.exp(tot), 0.0) - _mm(bt_e, p1, _TN, pg)
    f_m = ktv - _mm(bt_e, q1, _TN, pg)
    yield
    fold = lambda a: a[0:CH] + a[CH:2 * CH] + a[2 * CH:3 * CH] + a[3 * CH:4 * CH]
    p2 = fold(p2e)
    q2 = fold(q2e)
    h = h_ref[...]
    ps = RW_STATE_MODE
    o_ref[0] = _mm(p2, h, mode=ps) + q2
    h_ref[...] = _mm(g_m, h, mode=ps) + f_m


def _rw_body(xf_ref, xb_ref, of_ref, ob_ref, hf_ref, hb_ref):
    @pl.when(pl.program_id(1) == 0)
    def _():
        hf_ref[...] = jnp.zeros_like(hf_ref)
        hb_ref[...] = jnp.zeros_like(hb_ref)

    _alternate(_rw_direction(xf_ref, of_ref, hf_ref, True), _rw_direction(xb_ref, ob_ref, hb_ref, False))


def _rwscan(prep, lc):
    B, S, n = prep.shape
    nc = S // CH
    nc_ctx = lc // CH
    fmap = lambda b, s: (b, s, 0)
    bmap = lambda b, s: (b, _chunk_of_step(1, s, nc, nc_ctx), 0)
    return pl.pallas_call(
        _rw_body,
        grid=(B, nc),
        in_specs=[pl.BlockSpec((1, CH, n), fmap), pl.BlockSpec((1, CH, n), bmap)],
        out_specs=[pl.BlockSpec((1, CH, GW), fmap), pl.BlockSpec((1, CH, GW), bmap)],
        out_shape=[jax.ShapeDtypeStruct((B, S, GW), F32)] * 2,
        scratch_shapes=[pltpu.VMEM((GW, GW), F32)] * 2,
        compiler_params=_cparams(("parallel", "arbitrary")),
        name="rwkv_scan",
    )(prep, prep)


def _swa_body(cur_ref, prv_ref, nxt_ref, ctx_ref, sink_ref, o_ref, *, lc_blocks, nbl):
    j = pl.program_id(1)
    w = SWA_W
    is_lat = j >= lc_blocks
    nblk = j - lc_blocks
    lane_h = _iota((w, GW), 1) // HD
    q = cur_ref[0, :, 0:GW]
    qs = jnp.concatenate([jnp.where(lane_h == h, q, jnp.zeros_like(q)) for h in range(NH)], axis=0)
    kloc = jnp.concatenate([prv_ref[0, :, GW:2 * GW], cur_ref[0, :, GW:2 * GW], nxt_ref[0, :, GW:2 * GW]], axis=0)
    vloc = jnp.concatenate([prv_ref[0, :, 2 * GW:3 * GW], cur_ref[0, :, 2 * GW:3 * GW],
                            nxt_ref[0, :, 2 * GW:3 * GW]], axis=0)
    kc = ctx_ref[0, :, GW:2 * GW]
    vc = ctx_ref[0, :, 2 * GW:3 * GW]
    scale = HD ** -0.5
    s_loc = _mm(qs, kloc, _NT) * scale
    s_ctx = _mm(qs, kc, _NT) * scale
    qi = _iota((NH * w, 3 * w), 0) % w
    col = _iota((NH * w, 3 * w), 1)
    seg = col // w
    kj = col % w
    valid = ((seg == 0) & (nblk >= 1) & (kj >= qi)) | (seg == 1) | ((seg == 2) & (nblk < nbl - 1) & (kj <= qi))
    valid = valid & is_lat
    s_loc = jnp.where(valid, s_loc, -jnp.inf)
    hrow = _iota((NH * w, 1), 0) // w
    sink = jnp.zeros((NH * w, 1), F32)
    for h in range(NH):
        sink = jnp.where(hrow == h, sink_ref[h:h + 1, 0:1], sink)
    m = jnp.maximum(jnp.maximum(jnp.max(s_loc, axis=-1, keepdims=True), jnp.max(s_ctx, axis=-1, keepdims=True)), sink)
    p_loc = jnp.exp(s_loc - m)
    p_ctx = jnp.exp(s_ctx - m)
    den = jnp.sum(p_loc, axis=-1, keepdims=True) + jnp.sum(p_ctx, axis=-1, keepdims=True) + jnp.exp(sink - m)
    of = (_mm(p_loc, vloc) + _mm(p_ctx, vc)) / den
    out = jnp.zeros((w, GW), F32)
    for h in range(NH):
        out = out + jnp.where(lane_h == h, of[h * w:(h + 1) * w], 0.0)
    o_ref[0] = out


def _swa(p_swa, sink8, lc):
    B, S, n = p_swa.shape
    w = SWA_W
    nb = S // w
    lcb = lc // w
    return pl.pallas_call(
        functools.partial(_swa_body, lc_blocks=lcb, nbl=nb - lcb),
        grid=(B, nb),
        in_specs=[pl.BlockSpec((1, w, n), lambda b, j: (b, j, 0)),
                  pl.BlockSpec((1, w, n), lambda b, j: (b, jnp.maximum(j - 1, 0), 0)),
                  pl.BlockSpec((1, w, n), lambda b, j: (b, jnp.minimum(j + 1, nb - 1), 0)),
                  pl.BlockSpec((1, lc, n), lambda b, j: (b, 0, 0)),
                  pl.BlockSpec(sink8.shape, lambda b, j: (0, 0))],
        out_specs=pl.BlockSpec((1, w, GW), lambda b, j: (b, j, 0)),
        out_shape=jax.ShapeDtypeStruct((B, S, GW), F32),
        compiler_params=_cparams(("parallel", "parallel")),
        name="swa",
    )(p_swa, p_swa, p_swa, p_swa, sink8)


def _mlaprep_body(p_ref, qg_ref, kg_ref, wq, wqr, wk, wv, cs_ref, sn_ref, q_ref, k_ref, v_ref):
    p = p_ref[0]
    cq = p[:, 0:256]
    ckv = p[:, 256:384]
    krs = p[:, 384:512]
    qn = cq * lax.rsqrt(jnp.sum(cq * cq, axis=-1, keepdims=True) / MLA_Q_RANK + NORM_EPS) * qg_ref[...]
    kvn = ckv * lax.rsqrt(jnp.mean(ckv * ckv, axis=-1, keepdims=True) + NORM_EPS) * kg_ref[...]
    scale = (MLA_NOPE + MLA_ROPE) ** -0.5
    cs = cs_ref[...]
    sn = sn_ref[...]
    kvn_t = kvn.T.astype(BF16)
    for h in range(NH):
        qh = _mm(qn, wq[h]) * cs + _mm(qn, wqr[h]) * sn
        q_ref[0, h] = (qh * scale).astype(BF16)
        k_ref[0, h] = (_mm(kvn, wk[h]) + krs).astype(BF16)
        v_ref[0, h] = _mm(wv[h], kvn_t).astype(BF16)


def _mlaprep(p_mla, w, tabs):
    B, S, n = p_mla.shape
    nt = S // TM
    full = lambda a: pl.BlockSpec(a.shape, lambda b, i: (0,) * a.ndim)
    ws = [w["qg"], w["kg"], w["wq"], w["wqr"], w["wk"], w["wv"]]
    hspec = lambda c: pl.BlockSpec((1, NH, TM, c), lambda b, i: (b, 0, i, 0))
    return pl.pallas_call(
        _mlaprep_body,
        grid=(B, nt),
        in_specs=[pl.BlockSpec((1, TM, n), lambda b, i: (b, i, 0))] + [full(a) for a in ws]
                 + [pl.BlockSpec((TM, 128), lambda b, i: (i, 0))] * 2,
        out_specs=[hspec(128), hspec(128), pl.BlockSpec((1, NH, HD, TM), lambda b, i: (b, 0, 0, i))],
        out_shape=[jax.ShapeDtypeStruct((B, NH, S, 128), BF16), jax.ShapeDtypeStruct((B, NH, S, 128), BF16),
                   jax.ShapeDtypeStruct((B, NH, HD, S), BF16)],
        compiler_params=_cparams(("parallel", "parallel")),
        name="mla_prep",
    )(p_mla, *ws, tabs["cs_q"], tabs["sn_q"])


def _mla_body(q_ref, k_ref, v_ref, o_ref, *, lc):
    i = pl.program_id(2)
    ctx_q = i * TM < lc
    kc = MLA_KCHUNK
    nk = k_ref.shape[2] // kc

    def head(h):
        q = q_ref[0, h]
        ss = []
        m = None
        for c in range(nk):
            s = _mm(k_ref[0, h, c * kc:(c + 1) * kc, :], q, _NT)
            if c * kc >= lc:
                s = jnp.where(ctx_q, -jnp.inf, s)
            ss.append(s)
            cm = jnp.max(s, axis=0, keepdims=True)
            m = cm if m is None else jnp.maximum(m, cm)
            yield
        acc = None
        l = None
        for c in range(nk):
            p = jnp.exp(ss[c] - m)
            pv = _mm(v_ref[0, h, :, c * kc:(c + 1) * kc], p)
            ps = jnp.sum(p, axis=0, keepdims=True)
            acc = pv if acc is None else acc + pv
            l = ps if l is None else l + ps
            yield
        o_ref[0, h] = acc / l

    _alternate(*[head(h) for h in range(MLA_HPS)])


def _mla(q, k, v, lc):
    B, H, S, _ = q.shape
    nt = S // TM
    hp = MLA_HPS
    return pl.pallas_call(
        functools.partial(_mla_body, lc=lc),
        grid=(B, H // hp, nt),
        in_specs=[pl.BlockSpec((1, hp, TM, 128), lambda b, h, i: (b, h, i, 0)),
                  pl.BlockSpec((1, hp, S, 128), lambda b, h, i: (b, h, 0, 0)),
                  pl.BlockSpec((1, hp, HD, S), lambda b, h, i: (b, h, 0, 0))],
        out_specs=pl.BlockSpec((1, hp, HD, TM), lambda b, h, i: (b, h, 0, i)),
        out_shape=jax.ShapeDtypeStruct((B, H, HD, S), F32),
        compiler_params=_cparams(("parallel", "parallel", "parallel")),
        name="mla_attn",
    )(q, k, v)


def _outproj_body(*refs, lc_tiles, nb, moe):
    if moe:
        (x_ref, mod_ref, n2_ref, hgf_ref, hgb_ref, hgg_ref, hgn_ref, rwf_ref, rwb_ref, rwp_ref, rk_ref, lng_ref, lnb_ref,
         swo_ref, mlo_ref, wout_ref, rt_ref, xo_ref, h2_ref, lg_ref) = refs
    else:
        (x_ref, mod_ref, n2_ref, hgf_ref, hgb_ref, hgg_ref, hgn_ref, rwf_ref, rwb_ref, rwp_ref, rk_ref, lng_ref, lnb_ref,
         swo_ref, mlo_ref, wout_ref, xo_ref, h2_ref) = refs
    b = pl.program_id(0)
    i = pl.program_id(1)
    row = jnp.where(i < lc_tiles, nb, b)
    bd = _head_block_mask(GW).astype(BF16)

    def hsum(a):
        hi = a.astype(BF16)
        lo = (a - hi.astype(F32)).astype(BF16)
        return _mm(hi, bd) + _mm(lo, bd)

    o = hgf_ref[0] + hgb_ref[0]
    hg = o * lax.rsqrt(hsum(o * o) / HD + NORM_EPS) * hgn_ref[...] * _silu(hgg_ref[0])

    o = rwf_ref[0] + rwb_ref[0]
    c = o - hsum(o) / HD
    on = c * lax.rsqrt(hsum(c * c) / HD + RW_LN_EPS) * lng_ref[...] + lnb_ref[...]
    r = rwp_ref[0, :, 0:GW]
    kp = rwp_ref[0, :, 3 * GW:4 * GW]
    v = rwp_ref[0, :, 4 * GW:5 * GW]
    g = rwp_ref[0, :, 7 * GW:8 * GW]
    rw = (on + hsum(r * kp * rk_ref[...]) * v) * g

    acc = _mm(hg, wout_ref[0:GW, :]) + _mm(rw, wout_ref[GW:2 * GW, :]) + _mm(swo_ref[0], wout_ref[2 * GW:3 * GW, :])
    for h in range(NH):
        acc = acc + _mm(mlo_ref[0, h], wout_ref[3 * GW + h * HD:3 * GW + (h + 1) * HD, :], _TN)

    x = x_ref[0] + mod_ref[pl.ds(row, 1), 2 * D:3 * D] * acc
    xo_ref[0] = x
    sh = mod_ref[pl.ds(row, 1), 3 * D:4 * D]
    sc = mod_ref[pl.ds(row, 1), 4 * D:5 * D]
    ms = jnp.mean(x * x, axis=-1, keepdims=True)
    h2 = (x * lax.rsqrt(ms + NORM_EPS) * n2_ref[...]) * (1.0 + sc) + sh
    h2_ref[0] = h2.astype(BF16)
    if moe:
        lg_ref[0] = _mm(h2, rt_ref[...], mode="x3")


def _outproj(x, mod, n2g, hgo, p_hg, hgn, rwo, rwp, rk, lng, lnb, swo, mlo, wout, router, lc):
    B, S, _ = x.shape
    nt = S // TM
    moe = router is not None
    row = lambda n: pl.BlockSpec((1, TM, n), lambda b, i: (b, i, 0))
    full = lambda a: pl.BlockSpec(a.shape, lambda b, i: (0,) * a.ndim)
    ins = [x, mod, n2g, hgo[0], hgo[1], p_hg, hgn, rwo[0], rwo[1], rwp, rk, lng, lnb, swo, mlo, wout]
    in_specs = [row(D), full(mod), full(n2g), row(GW), row(GW),
                pl.BlockSpec((1, TM, GW), lambda b, i: (b, i, 4)), full(hgn),
                row(GW), row(GW), row(8 * GW), full(rk), full(lng), full(lnb), row(GW),
                pl.BlockSpec((1, NH, HD, TM), lambda b, i: (b, 0, 0, i)), full(wout)]
    out_shape = [jax.ShapeDtypeStruct((B, S, D), F32), jax.ShapeDtypeStruct((B, S, D), BF16)]
    out_specs = [row(D), row(D)]
    if moe:
        ins.append(router)
        in_specs.append(full(router))
        out_shape.append(jax.ShapeDtypeStruct((B, S, 128), F32))
        out_specs.append(row(128))
    return pl.pallas_call(
        functools.partial(_outproj_body, lc_tiles=lc // TM, nb=B, moe=moe),
        grid=(B, nt), in_specs=in_specs, out_specs=out_specs, out_shape=out_shape,
        compiler_params=_cparams(("parallel", "parallel")),
        name="outproj",
    )(*ins)


def _ffn_body(h_ref, w1_ref, w3_ref, w2_ref, o_ref, acc_ref):
    j = pl.program_id(1)

    @pl.when(j == 0)
    def _():
        acc_ref[...] = jnp.zeros_like(acc_ref)

    h = h_ref[...]
    a = _mm(h, w1_ref[...])
    z = _silu(a) * _mm(h, w3_ref[...])
    acc_ref[...] += _mm(z, w2_ref[...])

    @pl.when(j == pl.num_programs(1) - 1)
    def _():
        o_ref[...] = acc_ref[...]


def _ffn(h2, w1, w3, w2):
    M = h2.shape[0]
    tm = 1024
    tf = D_FF // 2
    return pl.pallas_call(
        _ffn_body,
        grid=(M // tm, D_FF // tf),
        in_specs=[pl.BlockSpec((tm, D), lambda i, j: (i, 0)),
                  pl.BlockSpec((D, tf), lambda i, j: (0, j)),
                  pl.BlockSpec((D, tf), lambda i, j: (0, j)),
                  pl.BlockSpec((tf, D), lambda i, j: (j, 0))],
        out_specs=pl.BlockSpec((tm, D), lambda i, j: (i, 0)),
        out_shape=jax.ShapeDtypeStruct((M, D), F32),
        scratch_shapes=[pltpu.VMEM((tm, D), F32)],
        compiler_params=_cparams(("parallel", "arbitrary")),
        name="ffn",
    )(h2, w1, w3, w2)


def _moe_body(h_ref, lg_ref, w1_ref, w3_ref, w2_ref, o_ref, acc_ref, comb_ref):
    e = pl.program_id(1)
    lane = _iota(comb_ref.shape, 1)

    @pl.when(e == 0)
    def _():
        acc_ref[...] = jnp.zeros_like(acc_ref)
        lg = jnp.where(lane < N_EXPERTS, lg_ref[...], -jnp.inf)
        v1 = jnp.max(lg, axis=-1, keepdims=True)
        i1 = jnp.min(jnp.where(lg == v1, lane, 128), axis=-1, keepdims=True)
        lg2 = jnp.where(lane == i1, -jnp.inf, lg)
        v2 = jnp.max(lg2, axis=-1, keepdims=True)
        i2 = jnp.min(jnp.where(lg2 == v2, lane, 128), axis=-1, keepdims=True)
        ex = jnp.exp(v2 - v1)
        comb_ref[...] = jnp.where(lane == i1, 1.0 / (1.0 + ex), 0.0) + jnp.where(lane == i2, ex / (1.0 + ex), 0.0)

    ce = jnp.sum(jnp.where(lane == e, comb_ref[...], 0.0), axis=-1, keepdims=True)
    h = h_ref[...]
    a = _mm(h, w1_ref[0])
    z = _silu(a) * _mm(h, w3_ref[0])
    acc_ref[...] += ce * _mm(z, w2_ref[0])

    @pl.when(e == pl.num_programs(1) - 1)
    def _():
        o_ref[...] = acc_ref[...]


def _moe(h2, logits, w1, w3, w2):
    M = h2.shape[0]
    tm = 1024
    f = w1.shape[2]
    return pl.pallas_call(
        _moe_body,
        grid=(M // tm, N_EXPERTS),
        in_specs=[pl.BlockSpec((tm, D), lambda i, e: (i, 0)),
                  pl.BlockSpec((tm, 128), lambda i, e: (i, 0)),
                  pl.BlockSpec((1, D, f), lambda i, e: (e, 0, 0)),
                  pl.BlockSpec((1, D, f), lambda i, e: (e, 0, 0)),
                  pl.BlockSpec((1, f, D), lambda i, e: (e, 0, 0))],
        out_specs=pl.BlockSpec((tm, D), lambda i, e: (i, 0)),
        out_shape=jax.ShapeDtypeStruct((M, D), F32),
        scratch_shapes=[pltpu.VMEM((tm, D), F32), pltpu.VMEM((tm, 128), F32)],
        compiler_params=_cparams(("parallel", "arbitrary")),
        name="moe",
    )(h2, logits, w1, w3, w2)


def _final_body(x_ref, y_ref, mod_ref, g_ref, o_ref):
    b = pl.program_id(0)
    x = x_ref[0] + mod_ref[pl.ds(b, 1), 5 * D:6 * D] * y_ref[0]
    ms = jnp.mean(x * x, axis=-1, keepdims=True)
    o_ref[0] = x * lax.rsqrt(ms + NORM_EPS) * g_ref[...]


def _final(x, y, mod, g, lc):
    B, S, _ = x.shape
    T = S - lc
    off = lc // TM
    return pl.pallas_call(
        _final_body,
        grid=(B, T // TM),
        in_specs=[pl.BlockSpec((1, TM, D), lambda b, i: (b, i + off, 0)),
                  pl.BlockSpec((1, TM, D), lambda b, i: (b, i + off, 0)),
                  pl.BlockSpec(mod.shape, lambda b, i: (0, 0)),
                  pl.BlockSpec(g.shape, lambda b, i: (0, 0))],
        out_specs=pl.BlockSpec((1, TM, D), lambda b, i: (b, i, 0)),
        out_shape=jax.ShapeDtypeStruct((B, T, D), F32),
        compiler_params=_cparams(("parallel", "parallel")),
        name="final_norm",
    )(x, y, mod, g)


def _rot_perm(width, group):
    j = np.arange(width)
    half = group // 2
    return np.where(j % group < half, j + half, j - half)


def _layer_weights(l, w_in, rw_mu, rw_w0, rw_w2, rw_a0, rw_a2, rw_g2, rw_k_k, rw_k_a, rw_r_k, rw_ln_g, rw_ln_b,
                   mla_qnorm_g, mla_wuq, mla_kvnorm_g, mla_wukv):
    wi = w_in[l]
    zc = lambda n: jnp.zeros((D, n), F32)
    o = 0
    w_hg = wi[:, o:o + 5 * GW]
    o += 5 * GW
    rwc = wi[:, o:o + 1088]
    o += 1088
    swc = wi[:, o:o + 512]
    o += 512
    mlc = wi[:, o:o + 352]

    def pad_rw(a, axis):
        z = lambda n: jnp.zeros(a.shape[:axis] + (n,) + a.shape[axis + 1:], a.dtype)
        sl = lambda s, e: lax.slice_in_dim(a, s, e, axis=axis)
        return jnp.concatenate([sl(0, 768), sl(768, 832), z(64), sl(832, 896), z(64), sl(896, 960), z(64),
                                sl(960, 1088)], axis=axis)

    w_rw = pad_rw(rwc, 1)
    mu = pad_rw(rw_mu[l], 1)
    pad_rows = lambda a: jnp.concatenate([a, jnp.zeros((64, a.shape[1]), a.dtype)], axis=0)
    w2 = jnp.stack([pad_rows(rw_w2[l, 0]), pad_rows(rw_w2[l, 1])]).astype(BF16)
    a2 = pad_rows(rw_a2[l]).astype(BF16)

    q = swc[:, 0:GW]
    kx = jnp.concatenate([swc[:, 256:320], swc[:, 256:320], swc[:, 320:384], swc[:, 320:384]], axis=1)
    vx = jnp.concatenate([swc[:, 384:448], swc[:, 384:448], swc[:, 448:512], swc[:, 448:512]], axis=1)
    perm = _rot_perm(GW, 32)
    w_swa = jnp.concatenate([q, kx, vx, q[:, perm], kx[:, perm]], axis=1)

    kr = mlc[:, 320:352]
    krp = kr[:, _rot_perm(MLA_ROPE, 16)]
    w_mla = jnp.concatenate([mlc[:, 0:192], zc(64), mlc[:, 192:320], zc(64), kr, zc(32), zc(64), krp, zc(32)], axis=1)

    wuq = mla_wuq[l]
    wukv = mla_wukv[l]
    rp = _rot_perm(MLA_ROPE, 16)
    wq, wqr, wk, wv = [], [], [], []
    zq = lambda r, n: jnp.zeros((r, n), F32)
    for h in range(NH):
        qh = wuq[:, 96 * h:96 * (h + 1)]
        a = jnp.concatenate([qh, zq(192, 32)], axis=1)
        ar = jnp.concatenate([zq(192, 64), qh[:, 64:96][:, rp], zq(192, 32)], axis=1)
        wq.append(jnp.concatenate([a, zq(64, 128)], axis=0))
        wqr.append(jnp.concatenate([ar, zq(64, 128)], axis=0))
        wk.append(jnp.concatenate([wukv[:, 128 * h:128 * h + 64], zq(128, 64)], axis=1))
        wv.append(wukv[:, 128 * h + 64:128 * (h + 1)].T)
    row = lambda a: a.reshape(1, -1)
    return {
        "proj": jnp.concatenate([w_hg, w_rw, w_swa, w_mla], axis=1).astype(BF16),
        "rw": {"mu": mu, "w0": rw_w0[l], "w2": w2, "a0": row(rw_a0[l]), "a2": a2, "g2": rw_g2[l].astype(BF16),
               "k_k": row(rw_k_k[l]), "k_a": row(rw_k_a[l])},
        "rk": row(rw_r_k[l]), "lng": row(rw_ln_g[l]), "lnb": row(rw_ln_b[l]),
        "mla": {"qg": jnp.concatenate([row(mla_qnorm_g[l]), jnp.zeros((1, 64), F32)], axis=1),
                "kg": row(mla_kvnorm_g[l]),
                "wq": jnp.stack(wq).astype(BF16), "wqr": jnp.stack(wqr).astype(BF16),
                "wk": jnp.stack(wk).astype(BF16), "wv": jnp.stack(wv).astype(BF16)},
    }


def _rope_tables(T, lc):
    rows = T // GRID_W
    rowp = jnp.repeat(jnp.arange(rows, dtype=F32), GRID_W)
    colp = jnp.tile(jnp.arange(GRID_W, dtype=F32), rows)

    def table(rot_dim, width, lane0):
        nf = rot_dim // 4
        inv = ROPE_BASE ** (-jnp.arange(nf, dtype=F32) / nf)
        j = np.arange(rot_dim)
        ax = j // (2 * nf)
        fr = j % nf
        sign = np.where(j % (2 * nf) < nf, -1.0, 1.0).astype(np.float32)
        pos = jnp.stack([rowp, colp], axis=1)
        ang = pos[:, ax] * inv[fr][None, :]
        cos = jnp.concatenate([jnp.ones((lc, rot_dim), F32), jnp.cos(ang)], axis=0)
        sin = jnp.concatenate([jnp.zeros((lc, rot_dim), F32), jnp.sin(ang) * sign[None, :]], axis=0)
        reps = (width - lane0) // rot_dim if lane0 == 0 else 1
        cos = jnp.tile(cos, (1, reps))
        sin = jnp.tile(sin, (1, reps))
        return cos, sin

    cs_s, sn_s = table(HD, 2 * GW, 0)
    c32, s32 = table(MLA_ROPE, MLA_ROPE, 0)
    S = lc + T
    z = lambda n: jnp.zeros((S, n), F32)
    o = lambda n: jnp.ones((S, n), F32)
    cs_m = jnp.concatenate([z(64), c32, z(32)], axis=1)
    sn_m = jnp.concatenate([z(64), s32, z(32)], axis=1)
    cs_q = jnp.concatenate([o(64), c32, z(32)], axis=1)
    sn_q = jnp.concatenate([z(64), s32, z(32)], axis=1)
    return {"cs_s": cs_s, "sn_s": sn_s, "cs_m": cs_m, "sn_m": sn_m, "cs_q": cs_q, "sn_q": sn_q}


def kernel(x, c, ctx, c_ctx, w_ada, b_ada, norm1_g, norm2_g, w_in, hg_lb, hg_norm_g, rw_mu, rw_w0, rw_w2, rw_a0,
           rw_a2, rw_g2, rw_k_k, rw_k_a, rw_r_k, rw_ln_g, rw_ln_b, swa_sink, mla_qnorm_g, mla_wuq, mla_kvnorm_g,
           mla_wukv, w_out, ffn_w1, ffn_w3, ffn_w2, moe_router, moe_w1, moe_w3, moe_w2, final_norm_g):
    B, T, _ = x.shape
    lc = ctx.shape[1]
    S = lc + T
    L = w_in.shape[0]
    assert lc % TM == 0 and T % TM == 0 and lc % SWA_W == 0 and T % GRID_W == 0 and (B * S) % 1024 == 0

    r8 = -(-(B + 1) // 8) * 8
    c8 = jnp.concatenate([c, c_ctx[None, :], jnp.zeros((r8 - B - 1, D), F32)], axis=0)
    mod_all = _ada(c8, w_ada, b_ada)
    tabs = _rope_tables(T, lc)
    xs = jnp.concatenate([ctx, x], axis=1)
    row = lambda a: a.reshape(1, -1)

    pend = None
    for l in range(L):
        w = _layer_weights(l, w_in, rw_mu, rw_w0, rw_w2, rw_a0, rw_a2, rw_g2, rw_k_k, rw_k_a, rw_r_k, rw_ln_g,
                           rw_ln_b, mla_qnorm_g, mla_wuq, mla_kvnorm_g, mla_wukv)
        mod = mod_all[l]
        xs, p_hg, p_rw, p_swa, p_mla = _proj(xs, pend, mod, row(norm1_g[l]), w["proj"], tabs, lc)

        hgo = _hgrn2(p_hg, hg_lb, l, lc)
        rwp = _rwprep(p_rw, w["rw"], lc)
        rwo = _rwscan(rwp, lc)
        sink8 = jnp.zeros((8, 128), F32).at[0:NH, :].set(jnp.broadcast_to(swa_sink[l][:, None], (NH, 128)))
        swo = _swa(p_swa, sink8, lc)
        mq, mk, mv = _mlaprep(p_mla, w["mla"], tabs)
        mlo = _mla(mq, mk, mv, lc)

        moe = l % 2 == 1
        router = None
        if moe:
            router = jnp.concatenate([moe_router[l // 2], jnp.zeros((D, 128 - N_EXPERTS), F32)], axis=1)
        outs = _outproj(xs, mod, row(norm2_g[l]), hgo, p_hg, row(hg_norm_g[l]), rwo, rwp, w["rk"], w["lng"],
                        w["lnb"], swo, mlo, w_out[l].astype(BF16), router, lc)
        xs, h2 = outs[0], outs[1]
        h2f = h2.reshape(B * S, D)
        if moe:
            e = l // 2
            y = _moe(h2f, outs[2].reshape(B * S, 128), moe_w1[e].astype(BF16), moe_w3[e].astype(BF16),
                     moe_w2[e].astype(BF16))
        else:
            e = l // 2
            y = _ffn(h2f, ffn_w1[e].astype(BF16), ffn_w3[e].astype(BF16), ffn_w2[e].astype(BF16))
        pend = (y.reshape(B, S, D), mod)

    return _final(xs, pend[0], pend[1], row(final_norm_g), lc)
```

```python
import functools

import jax
import jax.numpy as jnp
import numpy as np
from jax import lax
from jax.experimental import pallas as pl
from jax.experimental.pallas import tpu as pltpu

F32 = jnp.float32
BF16 = jnp.bfloat16

D = 1024
GW = 256
HD = 64
NH = 4
GRID_W = 64
NORM_EPS = 1e-6
ROPE_BASE = 10000.0
HG_F_FLOOR = 1e-30
RW_LN_EPS = 64e-5
SWA_W = 128
MLA_Q_RANK = 192
MLA_KV_RANK = 128
MLA_NOPE = 64
MLA_ROPE = 32
D_FF = 2816
N_EXPERTS = 8
D_FF_EXPERT = 1408

TM = 256
CH = 64
HB = 16
RW_BPS = 2
MLA_HPS = 4
MLA_VROWS = 80
LOG2E = 1.4426950408889634
MLA_KCHUNK = 256
VMEM_LIMIT = 56 * 1024 * 1024
RW_GRAM_MODE = "bf16"
RW_BASE_MODE = "bf16"
RW_MERGE_MODE = "bf16"
RW_STATE_MODE = "x3"

_NN = (((1,), (0,)), ((), ()))
_NT = (((1,), (1,)), ((), ()))
_TN = (((0,), (0,)), ((), ()))


def _mm(a, b, dims=_NN, mode="bf16"):
    if mode == "bf16":
        return lax.dot_general(a.astype(BF16), b.astype(BF16), dims, preferred_element_type=F32)
    if mode == "x3":
        ah = a.astype(BF16)
        al = (a - ah.astype(F32)).astype(BF16)
        bh = b.astype(BF16)
        bl = (b - bh.astype(F32)).astype(BF16)
        d = lambda x, y: lax.dot_general(x, y, dims, preferred_element_type=F32)
        return d(ah, bh) + (d(ah, bl) + d(al, bh))
    return lax.dot_general(a, b, dims, precision=lax.Precision.HIGHEST, preferred_element_type=F32)


def _sigmoid(x):
    return 1.0 / (1.0 + jnp.exp(-x))


def _silu(x):
    return x / (1.0 + jnp.exp(-x))


def _iota(shape, dim):
    return lax.broadcasted_iota(jnp.int32, shape, dim)


def _head_block_mask(n):
    return (_iota((n, n), 0) // HD) == (_iota((n, n), 1) // HD)


def _running_sum(x, fwd, seg):
    n = x.shape[0]
    pos = _iota(x.shape, 0) % seg
    sh = 1
    while sh < seg:
        if fwd:
            x = x + jnp.where(pos >= sh, pltpu.roll(x, sh, 0), 0.0)
        else:
            x = x + jnp.where(pos < seg - sh, pltpu.roll(x, n - sh, 0), 0.0)
        sh *= 2
    return x


def _alternate(*stages, skew=0):
    live = list(enumerate(stages))
    rnd = 0
    while live:
        for item in list(live):
            i, g = item
            if rnd < i * skew:
                continue
            try:
                next(g)
            except StopIteration:
                live.remove(item)
        rnd += 1


def _cparams(sem):
    return pltpu.CompilerParams(dimension_semantics=sem, vmem_limit_bytes=VMEM_LIMIT)


def _ada_body(c_ref, w_ref, b_ref, o_ref):
    s = _silu(c_ref[...])
    o_ref[0] = _mm(s, w_ref[0]) + b_ref[0]


def _ada(c8, w_ada, b_ada):
    L = w_ada.shape[0]
    r8 = c8.shape[0]
    tn = 1536
    return pl.pallas_call(
        _ada_body,
        grid=(L, 6 * D // tn),
        in_specs=[
            pl.BlockSpec((r8, D), lambda l, j: (0, 0)),
            pl.BlockSpec((1, D, tn), lambda l, j: (l, 0, j)),
            pl.BlockSpec((1, 1, tn), lambda l, j: (l, 0, j)),
        ],
        out_specs=pl.BlockSpec((1, r8, tn), lambda l, j: (l, 0, j)),
        out_shape=jax.ShapeDtypeStruct((L, r8, 6 * D), F32),
        compiler_params=_cparams(("parallel", "parallel")),
        name="adaln",
    )(c8, w_ada, b_ada.reshape(L, 1, 6 * D))


def _proj_body(*refs, pending, lc_tiles, nb):
    if pending:
        (x_ref, y_ref, modp_ref, mod_ref, g_ref, wall, cs_s, sn_s, cs_m, sn_m,
         xo_ref, phg, prw, pswa, pmla) = refs
    else:
        (x_ref, mod_ref, g_ref, wall, cs_s, sn_s, cs_m, sn_m,
         phg, prw, pswa, pmla) = refs
    b = pl.program_id(0)
    i = pl.program_id(1)
    row = jnp.where(i < lc_tiles, nb, b)
    x = x_ref[0]
    if pending:
        gate = modp_ref[pl.ds(row, 1), 5 * D:6 * D]
        x = x + gate * y_ref[0]
        xo_ref[0] = x
    sh = mod_ref[pl.ds(row, 1), 0:D]
    sc = mod_ref[pl.ds(row, 1), D:2 * D]
    ms = jnp.mean(x * x, axis=-1, keepdims=True)
    h = (x * lax.rsqrt(ms + NORM_EPS) * g_ref[...]) * (1.0 + sc) + sh
    hb = h.astype(BF16)
    n1 = 5 * GW
    phg[0] = _mm(hb, wall[:, 0:n1])
    prw[0] = _mm(hb, wall[:, n1:2 * n1])
    s = _mm(hb, wall[:, 2 * n1:3 * n1])
    qk = s[:, 0:2 * GW] * cs_s[...] + s[:, 3 * GW:5 * GW] * sn_s[...]
    pswa[0, :, 0:2 * GW] = qk.astype(BF16)
    pswa[0, :, 2 * GW:3 * GW] = s[:, 2 * GW:3 * GW].astype(BF16)
    m = _mm(hb, wall[:, 3 * n1:3 * n1 + 640])
    pmla[0, :, 0:384] = m[:, 0:384]
    pmla[0, :, 384:512] = m[:, 384:512] * cs_m[...] + m[:, 512:640] * sn_m[...]


def _proj(x, pend, mod, g, w, tabs, lc):
    B, S, _ = x.shape
    nt = S // TM
    pending = pend is not None
    row_spec = lambda n: pl.BlockSpec((1, TM, n), lambda b, i: (b, i, 0))
    full = lambda a: pl.BlockSpec(a.shape, lambda b, i: (0,) * a.ndim)
    tab_spec = lambda a: pl.BlockSpec((TM, a.shape[1]), lambda b, i: (i, 0))
    ins = [x]
    in_specs = [row_spec(D)]
    if pending:
        ins += [pend[0], pend[1]]
        in_specs += [row_spec(D), full(pend[1])]
    ins += [mod, g, w, tabs["cs_s"], tabs["sn_s"], tabs["cs_m"], tabs["sn_m"]]
    in_specs += [full(mod), full(g), full(w),
                 tab_spec(tabs["cs_s"]), tab_spec(tabs["sn_s"]), tab_spec(tabs["cs_m"]), tab_spec(tabs["sn_m"])]
    out_shape = [jax.ShapeDtypeStruct((B, S, 5 * GW), F32), jax.ShapeDtypeStruct((B, S, 5 * GW), F32),
                 jax.ShapeDtypeStruct((B, S, 3 * GW), BF16), jax.ShapeDtypeStruct((B, S, 512), F32)]
    out_specs = [row_spec(5 * GW), row_spec(5 * GW), row_spec(3 * GW), row_spec(512)]
    if pending:
        out_shape = [jax.ShapeDtypeStruct((B, S, D), F32)] + out_shape
        out_specs = [row_spec(D)] + out_specs
    outs = pl.pallas_call(
        functools.partial(_proj_body, pending=pending, lc_tiles=lc // TM, nb=B),
        grid=(B, nt), in_specs=in_specs, out_specs=out_specs, out_shape=out_shape,
        compiler_params=_cparams(("parallel", "parallel")),
        name="proj",
    )(*ins)
    if pending:
        return tuple(outs)
    return (x,) + tuple(outs)


def _chunk_of_step(d, s, nc, nc_ctx):
    bwd = jnp.where(s < nc_ctx, nc_ctx - 1 - s, nc - 1 - (s - nc_ctx))
    return jnp.where(d == 0, s, bwd)


def _hg_direction(p_ref, lb, o_ref, st_ref, fwd):
    p = p_ref[0]
    z = p[:, GW:2 * GW] if fwd else p[:, 2 * GW:3 * GW]
    f = lb + (1.0 - lb) * _sigmoid(z)
    logf = jnp.log(jnp.maximum(f, HG_F_FLOOR))
    key = (1.0 - lb) * _sigmoid(-z)
    qs = _silu(p[:, 0:GW])
    iv = p[:, 3 * GW:4 * GW]

    bcum = _running_sum(logf, fwd, HB)

    bd = _head_block_mask(GW)
    bdf = bd.astype(BF16)
    half = HB // 2
    nblk = CH // HB
    for j in range(nblk):
        r0 = (j if fwd else nblk - 1 - j) * HB
        qb, kb, ib, bb = (a[r0:r0 + HB] for a in (qs, key, iv, bcum))
        tot = bb[HB - 1:HB] if fwd else bb[0:1]
        st = st_ref[...]
        o_blk = _mm(qb * jnp.exp(bb), st, _NT)
        es = []
        spans = []
        for sidx in range(HB):
            if fwd:
                lo, hi = (0, HB) if sidx < half else (half, HB)
            else:
                lo, hi = (0, half) if sidx < half else (0, HB)
            rows = _iota((hi - lo, 1), 0) + lo
            m = (rows >= sidx) if fwd else (rows <= sidx)
            dec = jnp.exp(jnp.where(m, bb[lo:hi] - bb[sidx:sidx + 1], -jnp.inf))
            es.append(qb[lo:hi] * kb[sidx:sidx + 1] * dec)
            spans.append((lo, hi))
        yield
        att = _mm(jnp.concatenate(es, axis=0), bdf)
        top = o_blk[0:half]
        bot = o_blk[half:HB]
        off = 0
        for sidx, (lo, hi) in enumerate(spans):
            a = att[off:off + hi - lo] * ib[sidx:sidx + 1]
            off += hi - lo
            if hi - lo == HB:
                top = top + a[0:half]
                bot = bot + a[half:HB]
            elif lo == 0:
                top = top + a
            else:
                bot = bot + a
        o_ref[0, r0:r0 + half, :] = top
        o_ref[0, r0 + half:r0 + HB, :] = bot
        outer = _mm(ib, kb * jnp.exp(tot - bb), _TN)
        st_ref[...] = jnp.where(bd, st * jnp.exp(tot) + outer, 0.0)
        yield


def _hg_body(pf_ref, pb_ref, lbraw_ref, of_ref, ob_ref, stf_ref, stb_ref, *, layer):
    @pl.when(pl.program_id(1) == 0)
    def _():
        stf_ref[...] = jnp.zeros_like(stf_ref)
        stb_ref[...] = jnp.zeros_like(stb_ref)

    raw = lbraw_ref[...]
    nl = raw.shape[0]
    mx = raw[0]
    for j in range(1, nl):
        mx = jnp.maximum(mx, raw[j])
    ex = [jnp.exp(raw[j] - mx) for j in range(nl)]
    den = ex[0]
    for j in range(1, nl):
        den = den + ex[j]
    lb2 = jnp.zeros_like(mx)
    for j in range(1, layer + 1):
        lb2 = lb2 + ex[j] / den
    _alternate(_hg_direction(pf_ref, lb2[0:1], of_ref, stf_ref, True),
               _hg_direction(pb_ref, lb2[1:2], ob_ref, stb_ref, False))


def _hgrn2(p_hg, hg_lb, layer, lc):
    B, S, _ = p_hg.shape
    nc = S // CH
    nc_ctx = lc // CH
    fmap = lambda b, s: (b, s, 0)
    bmap = lambda b, s: (b, _chunk_of_step(1, s, nc, nc_ctx), 0)
    return pl.pallas_call(
        functools.partial(_hg_body, layer=layer),
        grid=(B, nc),
        in_specs=[pl.BlockSpec((1, CH, 5 * GW), fmap), pl.BlockSpec((1, CH, 5 * GW), bmap),
                  pl.BlockSpec(hg_lb.shape, lambda b, s: (0, 0, 0))],
        out_specs=[pl.BlockSpec((1, CH, GW), fmap), pl.BlockSpec((1, CH, GW), bmap)],
        out_shape=[jax.ShapeDtypeStruct((B, S, GW), F32)] * 2,
        scratch_shapes=[pltpu.VMEM((GW, GW), F32)] * 2,
        compiler_params=_cparams(("parallel", "arbitrary")),
        name="hgrn2",
    )(p_hg, p_hg, hg_lb)


def _rwprep_body(cur_ref, prv_ref, nxt_ref, mu_ref, w0_ref, w2_ref, a0_ref, a2_ref, g2_ref, kk_ref, ka_ref,
                 o_ref, buf, *, lc_tiles):
    i = pl.program_id(1)
    nt = pl.num_programs(1)
    first = (i == 0) | (i == lc_tiles)
    lastt = (i == lc_tiles - 1) | (i == nt - 1)
    buf[0:8, :] = jnp.where(first, 0.0, prv_ref[0])
    buf[8:8 + TM, :] = cur_ref[0]
    buf[8 + TM:16 + TM, :] = jnp.where(lastt, 0.0, nxt_ref[0])
    cur = buf[8:8 + TM, :]
    prev = buf[7:7 + TM, :]
    nxt = buf[9:9 + TM, :]
    sft = cur + mu_ref[0:1, :] * (prev - cur) + mu_ref[1:2, :] * (nxt - cur)
    r = sft[:, 0:GW]
    k = sft[:, GW:2 * GW]
    v = sft[:, 2 * GW:3 * GW]
    ywf = sft[:, 768:896]
    ywb = sft[:, 896:1024]
    ya = sft[:, 1024:1152]
    yg = sft[:, 1152:1280]
    c = float(np.exp(-0.5))
    lwf = -c * _sigmoid(w0_ref[0:1, :] + _mm(jnp.tanh(ywf), w2_ref[0]))
    lwb = -c * _sigmoid(w0_ref[1:2, :] + _mm(jnp.tanh(ywb), w2_ref[1]))
    a = _sigmoid(a0_ref[...] + _mm(ya, a2_ref[...]))
    kk0 = k * kk_ref[...]
    ss = _mm(kk0 * kk0, _head_block_mask(GW).astype(F32), mode="x3")
    kk = kk0 / jnp.maximum(jnp.sqrt(ss), 1e-12)
    kp = k * (1.0 + (a - 1.0) * ka_ref[...])
    g = _mm(_sigmoid(yg), g2_ref[...])
    o_ref[0, :, 0:GW] = r
    o_ref[0, :, GW:2 * GW] = lwf
    o_ref[0, :, 2 * GW:3 * GW] = lwb
    o_ref[0, :, 3 * GW:4 * GW] = kp
    o_ref[0, :, 4 * GW:5 * GW] = v
    o_ref[0, :, 5 * GW:6 * GW] = kk
    o_ref[0, :, 6 * GW:7 * GW] = kk * a
    o_ref[0, :, 7 * GW:8 * GW] = g


def _rwprep(p_rw, w, lc):
    B, S, n = p_rw.shape
    nt = S // TM
    t8 = TM // 8
    full = lambda a: pl.BlockSpec(a.shape, lambda b, i: (0,) * a.ndim)
    small = [w["mu"], w["w0"], w["w2"], w["a0"], w["a2"], w["g2"], w["k_k"], w["k_a"]]
    return pl.pallas_call(
        functools.partial(_rwprep_body, lc_tiles=lc // TM),
        grid=(B, nt),
        in_specs=[pl.BlockSpec((1, TM, n), lambda b, i: (b, i, 0)),
                  pl.BlockSpec((1, 8, n), lambda b, i: (b, jnp.maximum(i * t8 - 1, 0), 0)),
                  pl.BlockSpec((1, 8, n), lambda b, i: (b, jnp.minimum((i + 1) * t8, S // 8 - 1), 0))]
                 + [full(a) for a in small],
        out_specs=pl.BlockSpec((1, TM, 8 * GW), lambda b, i: (b, i, 0)),
        out_shape=jax.ShapeDtypeStruct((B, S, 8 * GW), F32),
        scratch_shapes=[pltpu.VMEM((TM + 16, n), F32)],
        compiler_params=_cparams(("parallel", "parallel")),
        name="rwkv_prep",
    )(p_rw, p_rw, p_rw, *small)


def _rw_direction(x_ref, o_ref, h_ref, fwd, bi):
    x = x_ref[bi]
    r = x[:, 0:GW]
    lw = x[:, GW:2 * GW] if fwd else x[:, 2 * GW:3 * GW]
    k = x[:, 3 * GW:4 * GW]
    v = x[:, 4 * GW:5 * GW]
    kk = x[:, 5 * GW:6 * GW]
    bb = x[:, 6 * GW:7 * GW]

    incl = _running_sum(lw, fwd, CH)
    tot = incl[CH - 1:CH] if fwd else incl[0:1]
    e_in = jnp.exp(incl)
    e_ni = jnp.exp(-incl)
    e_tt = jnp.exp(tot - incl)
    km = kk * jnp.exp(incl - lw)
    rt = r * e_in
    bh = bb * e_ni
    kh = k * e_ni
    bt = bb * e_tt
    kt = k * e_tt

    n = NH * CH
    hm = (_iota((n, GW), 0) // CH) == (_iota((n, GW), 1) // HD)
    tile4 = lambda a: jnp.concatenate([a] * NH, axis=0)
    expand = lambda a: jnp.where(hm, tile4(a), 0.0)
    km_e = expand(km)
    rt_e = expand(rt)
    v_e = expand(v)
    bt_e = expand(bt)
    kt_e = expand(kt)
    bh_t = tile4(bh)
    kh_t = tile4(kh)

    rr = _iota((n, n), 0)
    cc = _iota((n, n), 1)
    same = (rr // CH) == (cc // CH)
    rt_i = rr % CH
    ct_i = cc % CH
    before = same & ((ct_i < rt_i) if fwd else (ct_i > rt_i))
    upto = same & ((ct_i <= rt_i) if fwd else (ct_i >= rt_i))
    eye = rr == cc

    pg = RW_GRAM_MODE
    a_m = jnp.where(before, _mm(km_e, bh_t, _NT, pg), 0.0)
    yield

    blk = lambda m: (rr // m) == (cc // m)
    pb = RW_BASE_MODE
    nm = jnp.where(blk(8), -a_m, 0.0)
    x2 = _mm(nm, nm, mode=pb)
    b_m = jnp.where(before, _mm(km_e, kh_t, _NT, pg), 0.0)
    yield
    tm = jnp.where(eye, 1.0, 0.0) + nm
    tm = tm + _mm(tm, x2, mode=pb)
    x4 = _mm(x2, x2, mode=pb)
    yield
    tm = tm + _mm(tm, x4, mode=pb)
    mrb = jnp.where(upto, _mm(rt_e, bh_t, _NT, pg), 0.0)
    yield
    pm = RW_MERGE_MODE
    side = [lambda: jnp.where(upto, _mm(rt_e, kh_t, _NT, pg), 0.0), lambda: _mm(b_m, v_e, mode=pg),
            lambda: _mm(kt_e, v_e, _TN, pg)]
    done = []
    for m in (8, 16, 32):
        low = jnp.where(blk(2 * m) & (rr // m != cc // m), a_m, 0.0)
        y = _mm(low, tm, mode=pm)
        done.append(side.pop(0)())
        yield
        tm = tm - _mm(tm, y, mode=pm)
        yield
    mrk, bmv, ktv = done

    p1 = _mm(tm, km_e, mode=pg)
    q1 = _mm(tm, bmv, mode=pg)
    yield
    p2e = rt_e - _mm(mrb, p1, mode=pg)
    q2e = _mm(mrk, v_e, mode=pg) - _mm(mrb, q1, mode=pg)
    g_m = jnp.where(eye, jnp.exp(tot), 0.0) - _mm(bt_e, p1, _TN, pg)
    f_m = ktv - _mm(bt_e, q1, _TN, pg)
    yield
    fold = lambda a: a[0:CH] + a[CH:2 * CH] + a[2 * CH:3 * CH] + a[3 * CH:4 * CH]
    p2 = fold(p2e)
    q2 = fold(q2e)
    h = h_ref[bi]
    ps = RW_STATE_MODE
    o_ref[bi] = _mm(p2, h, mode=ps) + q2
    h_ref[bi] = _mm(g_m, h, mode=ps) + f_m


def _rw_body(xf_ref, xb_ref, of_ref, ob_ref, hf_ref, hb_ref):
    @pl.when(pl.program_id(1) == 0)
    def _():
        hf_ref[...] = jnp.zeros_like(hf_ref)
        hb_ref[...] = jnp.zeros_like(hb_ref)

    chains = []
    for bi in range(xf_ref.shape[0]):
        chains.append(_rw_direction(xf_ref, of_ref, hf_ref, True, bi))
        chains.append(_rw_direction(xb_ref, ob_ref, hb_ref, False, bi))
    _alternate(*chains)


def _rwscan(prep, lc):
    B, S, n = prep.shape
    nc = S // CH
    nc_ctx = lc // CH
    bps = RW_BPS if B % RW_BPS == 0 else 1
    fmap = lambda b, s: (b, s, 0)
    bmap = lambda b, s: (b, _chunk_of_step(1, s, nc, nc_ctx), 0)
    return pl.pallas_call(
        _rw_body,
        grid=(B // bps, nc),
        in_specs=[pl.BlockSpec((bps, CH, n), fmap), pl.BlockSpec((bps, CH, n), bmap)],
        out_specs=[pl.BlockSpec((bps, CH, GW), fmap), pl.BlockSpec((bps, CH, GW), bmap)],
        out_shape=[jax.ShapeDtypeStruct((B, S, GW), F32)] * 2,
        scratch_shapes=[pltpu.VMEM((bps, GW, GW), F32)] * 2,
        compiler_params=_cparams(("parallel", "arbitrary")),
        name="rwkv_scan",
    )(prep, prep)


def _swa_block(q, kloc, vloc, kc, vc, sink_ref, o_ref, r0, j, lc_blocks, nbl):
    w = SWA_W
    is_lat = j >= lc_blocks
    nblk = j - lc_blocks
    lane_h = _iota((w, GW), 1) // HD
    qs = jnp.concatenate([jnp.where(lane_h == h, q, jnp.zeros_like(q)) for h in range(NH)], axis=0)
    scale = HD ** -0.5
    s_loc = _mm(qs, kloc, _NT) * scale
    s_ctx = _mm(qs, kc, _NT) * scale
    yield
    qi = _iota((NH * w, 3 * w), 0) % w
    col = _iota((NH * w, 3 * w), 1)
    seg = col // w
    kj = col % w
    valid = ((seg == 0) & (nblk >= 1) & (kj >= qi)) | (seg == 1) | ((seg == 2) & (nblk < nbl - 1) & (kj <= qi))
    valid = valid & is_lat
    s_loc = jnp.where(valid, s_loc, -jnp.inf)
    hrow = _iota((NH * w, 1), 0) // w
    sink = jnp.zeros((NH * w, 1), F32)
    for h in range(NH):
        sink = jnp.where(hrow == h, sink_ref[h:h + 1, 0:1], sink)
    m = jnp.maximum(jnp.maximum(jnp.max(s_loc, axis=-1, keepdims=True), jnp.max(s_ctx, axis=-1, keepdims=True)), sink)
    p_loc = jnp.exp(s_loc - m)
    p_ctx = jnp.exp(s_ctx - m)
    den = jnp.sum(p_loc, axis=-1, keepdims=True) + jnp.sum(p_ctx, axis=-1, keepdims=True) + jnp.exp(sink - m)
    yield
    of = (_mm(p_loc, vloc) + _mm(p_ctx, vc)) / den
    out = jnp.zeros((w, GW), F32)
    for h in range(NH):
        out = out + jnp.where(lane_h == h, of[h * w:(h + 1) * w], 0.0)
    o_ref[0, r0:r0 + w, :] = out
    yield


def _swa_body(pair_ref, prv_ref, nxt_ref, ctx_ref, sink_ref, o_ref, *, lc_blocks, nbl):
    i = pl.program_id(1)
    w = SWA_W
    kcol = slice(GW, 2 * GW)
    vcol = slice(2 * GW, 3 * GW)
    ka, kb = pair_ref[0, 0:w, kcol], pair_ref[0, w:2 * w, kcol]
    va, vb = pair_ref[0, 0:w, vcol], pair_ref[0, w:2 * w, vcol]
    kc = ctx_ref[0, :, kcol]
    vc = ctx_ref[0, :, vcol]
    cat = lambda *a: jnp.concatenate(a, axis=0)
    first = _swa_block(pair_ref[0, 0:w, 0:GW], cat(prv_ref[0, :, kcol], ka, kb), cat(prv_ref[0, :, vcol], va, vb),
                       kc, vc, sink_ref, o_ref, 0, 2 * i, lc_blocks, nbl)
    second = _swa_block(pair_ref[0, w:2 * w, 0:GW], cat(ka, kb, nxt_ref[0, :, kcol]), cat(va, vb, nxt_ref[0, :, vcol]),
                        kc, vc, sink_ref, o_ref, w, 2 * i + 1, lc_blocks, nbl)
    _alternate(first, second, skew=1)


def _swa(p_swa, sink8, lc):
    B, S, n = p_swa.shape
    w = SWA_W
    nb = S // w
    lcb = lc // w
    assert nb % 2 == 0 and lcb % 2 == 0
    return pl.pallas_call(
        functools.partial(_swa_body, lc_blocks=lcb, nbl=nb - lcb),
        grid=(B, nb // 2),
        in_specs=[pl.BlockSpec((1, 2 * w, n), lambda b, i: (b, i, 0)),
                  pl.BlockSpec((1, w, n), lambda b, i: (b, jnp.maximum(2 * i - 1, 0), 0)),
                  pl.BlockSpec((1, w, n), lambda b, i: (b, jnp.minimum(2 * i + 2, nb - 1), 0)),
                  pl.BlockSpec((1, lc, n), lambda b, i: (b, 0, 0)),
                  pl.BlockSpec(sink8.shape, lambda b, i: (0, 0))],
        out_specs=pl.BlockSpec((1, 2 * w, GW), lambda b, i: (b, i, 0)),
        out_shape=jax.ShapeDtypeStruct((B, S, GW), F32),
        compiler_params=_cparams(("parallel", "parallel")),
        name="swa",
    )(p_swa, p_swa, p_swa, p_swa, sink8)


def _mlaprep_body(p_ref, qg_ref, kg_ref, wq, wqr, wk, wv, cs_ref, sn_ref, q_ref, k_ref, v_ref):
    p = p_ref[0]
    cq = p[:, 0:256]
    ckv = p[:, 256:384]
    krs = p[:, 384:512]
    qn = cq * lax.rsqrt(jnp.sum(cq * cq, axis=-1, keepdims=True) / MLA_Q_RANK + NORM_EPS) * qg_ref[...]
    kvn = ckv * lax.rsqrt(jnp.mean(ckv * ckv, axis=-1, keepdims=True) + NORM_EPS) * kg_ref[...]
    scale = (MLA_NOPE + MLA_ROPE) ** -0.5 * LOG2E
    cs = cs_ref[...]
    sn = sn_ref[...]
    kvn_t = kvn.T.astype(BF16)
    ones_rows = (_iota((MLA_VROWS - HD, TM), 0) == 0).astype(BF16)
    for h in range(NH):
        qh = _mm(qn, wq[h]) * cs + _mm(qn, wqr[h]) * sn
        q_ref[0, h] = (qh * scale).astype(BF16)
        k_ref[0, h] = (_mm(kvn, wk[h]) + krs).astype(BF16)
        v_ref[0, h, 0:HD, :] = _mm(wv[h], kvn_t).astype(BF16)
        v_ref[0, h, HD:MLA_VROWS, :] = ones_rows


def _mlaprep(p_mla, w, tabs):
    B, S, n = p_mla.shape
    nt = S // TM
    full = lambda a: pl.BlockSpec(a.shape, lambda b, i: (0,) * a.ndim)
    ws = [w["qg"], w["kg"], w["wq"], w["wqr"], w["wk"], w["wv"]]
    hspec = lambda c: pl.BlockSpec((1, NH, TM, c), lambda b, i: (b, 0, i, 0))
    return pl.pallas_call(
        _mlaprep_body,
        grid=(B, nt),
        in_specs=[pl.BlockSpec((1, TM, n), lambda b, i: (b, i, 0))] + [full(a) for a in ws]
                 + [pl.BlockSpec((TM, 128), lambda b, i: (i, 0))] * 2,
        out_specs=[hspec(128), hspec(128), pl.BlockSpec((1, NH, MLA_VROWS, TM), lambda b, i: (b, 0, 0, i))],
        out_shape=[jax.ShapeDtypeStruct((B, NH, S, 128), BF16), jax.ShapeDtypeStruct((B, NH, S, 128), BF16),
                   jax.ShapeDtypeStruct((B, NH, MLA_VROWS, S), BF16)],
        compiler_params=_cparams(("parallel", "parallel")),
        name="mla_prep",
    )(p_mla, *ws, tabs["cs_q"], tabs["sn_q"])


def _mla_body(q_ref, k_ref, v_ref, o_ref, *, lc):
    i = pl.program_id(2)
    ctx_q = i * TM < lc
    kc = MLA_KCHUNK
    nk = k_ref.shape[2] // kc

    def head(h):
        q = q_ref[0, h]
        ss = []
        m = None
        for c in range(nk):
            s = _mm(k_ref[0, h, c * kc:(c + 1) * kc, :], q, _NT)
            if c * kc >= lc:
                s = jnp.where(ctx_q, -jnp.inf, s)
            ss.append(s)
            cm = jnp.max(s, axis=0, keepdims=True)
            m = cm if m is None else jnp.maximum(m, cm)
            yield
        acc = None
        for c in range(nk):
            p = jnp.exp2(ss[c] - m)
            pv = _mm(v_ref[0, h, :, c * kc:(c + 1) * kc], p)
            acc = pv if acc is None else acc + pv
            yield
        o_ref[0, h] = acc[0:HD] / acc[HD:HD + 1]

    _alternate(*[head(h) for h in range(MLA_HPS)])


def _mla(q, k, v, lc):
    B, H, S, _ = q.shape
    nt = S // TM
    hp = MLA_HPS
    return pl.pallas_call(
        functools.partial(_mla_body, lc=lc),
        grid=(B, H // hp, nt),
        in_specs=[pl.BlockSpec((1, hp, TM, 128), lambda b, h, i: (b, h, i, 0)),
                  pl.BlockSpec((1, hp, S, 128), lambda b, h, i: (b, h, 0, 0)),
                  pl.BlockSpec((1, hp, MLA_VROWS, S), lambda b, h, i: (b, h, 0, 0))],
        out_specs=pl.BlockSpec((1, hp, HD, TM), lambda b, h, i: (b, h, 0, i)),
        out_shape=jax.ShapeDtypeStruct((B, H, HD, S), F32),
        compiler_params=_cparams(("parallel", "parallel", "parallel")),
        name="mla_attn",
    )(q, k, v)


def _outproj_body(*refs, lc_tiles, nb, moe):
    if moe:
        (x_ref, mod_ref, n2_ref, hgf_ref, hgb_ref, hgg_ref, hgn_ref, rwf_ref, rwb_ref, rwp_ref, rk_ref, lng_ref, lnb_ref,
         swo_ref, mlo_ref, wout_ref, rt_ref, xo_ref, h2_ref, lg_ref) = refs
    else:
        (x_ref, mod_ref, n2_ref, hgf_ref, hgb_ref, hgg_ref, hgn_ref, rwf_ref, rwb_ref, rwp_ref, rk_ref, lng_ref, lnb_ref,
         swo_ref, mlo_ref, wout_ref, xo_ref, h2_ref) = refs
    b = pl.program_id(0)
    i = pl.program_id(1)
    row = jnp.where(i < lc_tiles, nb, b)
    bd = _head_block_mask(GW).astype(BF16)

    def hsum(a):
        hi = a.astype(BF16)
        lo = (a - hi.astype(F32)).astype(BF16)
        return _mm(hi, bd) + _mm(lo, bd)

    o = hgf_ref[0] + hgb_ref[0]
    hg = o * lax.rsqrt(hsum(o * o) / HD + NORM_EPS) * hgn_ref[...] * _silu(hgg_ref[0])

    o = rwf_ref[0] + rwb_ref[0]
    c = o - hsum(o) / HD
    on = c * lax.rsqrt(hsum(c * c) / HD + RW_LN_EPS) * lng_ref[...] + lnb_ref[...]
    r = rwp_ref[0, :, 0:GW]
    kp = rwp_ref[0, :, 3 * GW:4 * GW]
    v = rwp_ref[0, :, 4 * GW:5 * GW]
    g = rwp_ref[0, :, 7 * GW:8 * GW]
    rw = (on + hsum(r * kp * rk_ref[...]) * v) * g

    acc = _mm(hg, wout_ref[0:GW, :]) + _mm(rw, wout_ref[GW:2 * GW, :]) + _mm(swo_ref[0], wout_ref[2 * GW:3 * GW, :])
    for h in range(NH):
        acc = acc + _mm(mlo_ref[0, h], wout_ref[3 * GW + h * HD:3 * GW + (h + 1) * HD, :], _TN)

    x = x_ref[0] + mod_ref[pl.ds(row, 1), 2 * D:3 * D] * acc
    xo_ref[0] = x
    sh = mod_ref[pl.ds(row, 1), 3 * D:4 * D]
    sc = mod_ref[pl.ds(row, 1), 4 * D:5 * D]
    ms = jnp.mean(x * x, axis=-1, keepdims=True)
    h2 = (x * lax.rsqrt(ms + NORM_EPS) * n2_ref[...]) * (1.0 + sc) + sh
    h2_ref[0] = h2.astype(BF16)
    if moe:
        lg_ref[0] = _mm(h2, rt_ref[...], mode="x3")


def _outproj(x, mod, n2g, hgo, p_hg, hgn, rwo, rwp, rk, lng, lnb, swo, mlo, wout, router, lc):
    B, S, _ = x.shape
    nt = S // TM
    moe = router is not None
    row = lambda n: pl.BlockSpec((1, TM, n), lambda b, i: (b, i, 0))
    full = lambda a: pl.BlockSpec(a.shape, lambda b, i: (0,) * a.ndim)
    ins = [x, mod, n2g, hgo[0], hgo[1], p_hg, hgn, rwo[0], rwo[1], rwp, rk, lng, lnb, swo, mlo, wout]
    in_specs = [row(D), full(mod), full(n2g), row(GW), row(GW),
                pl.BlockSpec((1, TM, GW), lambda b, i: (b, i, 4)), full(hgn),
                row(GW), row(GW), row(8 * GW), full(rk), full(lng), full(lnb), row(GW),
                pl.BlockSpec((1, NH, HD, TM), lambda b, i: (b, 0, 0, i)), full(wout)]
    out_shape = [jax.ShapeDtypeStruct((B, S, D), F32), jax.ShapeDtypeStruct((B, S, D), BF16)]
    out_specs = [row(D), row(D)]
    if moe:
        ins.append(router)
        in_specs.append(full(router))
        out_shape.append(jax.ShapeDtypeStruct((B, S, 128), F32))
        out_specs.append(row(128))
    return pl.pallas_call(
        functools.partial(_outproj_body, lc_tiles=lc // TM, nb=B, moe=moe),
        grid=(B, nt), in_specs=in_specs, out_specs=out_specs, out_shape=out_shape,
        compiler_params=_cparams(("parallel", "parallel")),
        name="outproj",
    )(*ins)


def _ffn_body(h_ref, w1_ref, w3_ref, w2_ref, o_ref, acc_ref):
    j = pl.program_id(1)

    @pl.when(j == 0)
    def _():
        acc_ref[...] = jnp.zeros_like(acc_ref)

    h = h_ref[...]
    a = _mm(h, w1_ref[...])
    z = _silu(a) * _mm(h, w3_ref[...])
    acc_ref[...] += _mm(z, w2_ref[...])

    @pl.when(j == pl.num_programs(1) - 1)
    def _():
        o_ref[...] = acc_ref[...]


def _ffn(h2, w1, w3, w2):
    M = h2.shape[0]
    tm = 1024
    tf = D_FF // 2
    return pl.pallas_call(
        _ffn_body,
        grid=(M // tm, D_FF // tf),
        in_specs=[pl.BlockSpec((tm, D), lambda i, j: (i, 0)),
                  pl.BlockSpec((D, tf), lambda i, j: (0, j)),
                  pl.BlockSpec((D, tf), lambda i, j: (0, j)),
                  pl.BlockSpec((tf, D), lambda i, j: (j, 0))],
        out_specs=pl.BlockSpec((tm, D), lambda i, j: (i, 0)),
        out_shape=jax.ShapeDtypeStruct((M, D), F32),
        scratch_shapes=[pltpu.VMEM((tm, D), F32)],
        compiler_params=_cparams(("parallel", "arbitrary")),
        name="ffn",
    )(h2, w1, w3, w2)


def _moe_body(h_ref, lg_ref, w1_ref, w3_ref, w2_ref, o_ref, acc_ref, comb_ref):
    e = pl.program_id(1)
    lane = _iota(comb_ref.shape, 1)

    @pl.when(e == 0)
    def _():
        acc_ref[...] = jnp.zeros_like(acc_ref)
        lg = jnp.where(lane < N_EXPERTS, lg_ref[...], -jnp.inf)
        v1 = jnp.max(lg, axis=-1, keepdims=True)
        i1 = jnp.min(jnp.where(lg == v1, lane, 128), axis=-1, keepdims=True)
        lg2 = jnp.where(lane == i1, -jnp.inf, lg)
        v2 = jnp.max(lg2, axis=-1, keepdims=True)
        i2 = jnp.min(jnp.where(lg2 == v2, lane, 128), axis=-1, keepdims=True)
        ex = jnp.exp(v2 - v1)
        comb_ref[...] = jnp.where(lane == i1, 1.0 / (1.0 + ex), 0.0) + jnp.where(lane == i2, ex / (1.0 + ex), 0.0)

    ce = jnp.sum(jnp.where(lane == e, comb_ref[...], 0.0), axis=-1, keepdims=True)
    h = h_ref[...]
    a = _mm(h, w1_ref[0])
    z = _silu(a) * _mm(h, w3_ref[0])
    acc_ref[...] += ce * _mm(z, w2_ref[0])

    @pl.when(e == pl.num_programs(1) - 1)
    def _():
        o_ref[...] = acc_ref[...]


def _moe(h2, logits, w1, w3, w2):
    M = h2.shape[0]
    tm = 1024
    f = w1.shape[2]
    return pl.pallas_call(
        _moe_body,
        grid=(M // tm, N_EXPERTS),
        in_specs=[pl.BlockSpec((tm, D), lambda i, e: (i, 0)),
                  pl.BlockSpec((tm, 128), lambda i, e: (i, 0)),
                  pl.BlockSpec((1, D, f), lambda i, e: (e, 0, 0)),
                  pl.BlockSpec((1, D, f), lambda i, e: (e, 0, 0)),
                  pl.BlockSpec((1, f, D), lambda i, e: (e, 0, 0))],
        out_specs=pl.BlockSpec((tm, D), lambda i, e: (i, 0)),
        out_shape=jax.ShapeDtypeStruct((M, D), F32),
        scratch_shapes=[pltpu.VMEM((tm, D), F32), pltpu.VMEM((tm, 128), F32)],
        compiler_params=_cparams(("parallel", "arbitrary")),
        name="moe",
    )(h2, logits, w1, w3, w2)


def _final_body(x_ref, y_ref, mod_ref, g_ref, o_ref):
    b = pl.program_id(0)
    x = x_ref[0] + mod_ref[pl.ds(b, 1), 5 * D:6 * D] * y_ref[0]
    ms = jnp.mean(x * x, axis=-1, keepdims=True)
    o_ref[0] = x * lax.rsqrt(ms + NORM_EPS) * g_ref[...]


def _final(x, y, mod, g, lc):
    B, S, _ = x.shape
    T = S - lc
    off = lc // TM
    return pl.pallas_call(
        _final_body,
        grid=(B, T // TM),
        in_specs=[pl.BlockSpec((1, TM, D), lambda b, i: (b, i + off, 0)),
                  pl.BlockSpec((1, TM, D), lambda b, i: (b, i + off, 0)),
                  pl.BlockSpec(mod.shape, lambda b, i: (0, 0)),
                  pl.BlockSpec(g.shape, lambda b, i: (0, 0))],
        out_specs=pl.BlockSpec((1, TM, D), lambda b, i: (b, i, 0)),
        out_shape=jax.ShapeDtypeStruct((B, T, D), F32),
        compiler_params=_cparams(("parallel", "parallel")),
        name="final_norm",
    )(x, y, mod, g)


def _rot_perm(width, group):
    j = np.arange(width)
    half = group // 2
    return np.where(j % group < half, j + half, j - half)


def _layer_weights(l, w_in, rw_mu, rw_w0, rw_w2, rw_a0, rw_a2, rw_g2, rw_k_k, rw_k_a, rw_r_k, rw_ln_g, rw_ln_b,
                   mla_qnorm_g, mla_wuq, mla_kvnorm_g, mla_wukv):
    wi = w_in[l]
    zc = lambda n: jnp.zeros((D, n), F32)
    o = 0
    w_hg = wi[:, o:o + 5 * GW]
    o += 5 * GW
    rwc = wi[:, o:o + 1088]
    o += 1088
    swc = wi[:, o:o + 512]
    o += 512
    mlc = wi[:, o:o + 352]

    def pad_rw(a, axis):
        z = lambda n: jnp.zeros(a.shape[:axis] + (n,) + a.shape[axis + 1:], a.dtype)
        sl = lambda s, e: lax.slice_in_dim(a, s, e, axis=axis)
        return jnp.concatenate([sl(0, 768), sl(768, 832), z(64), sl(832, 896), z(64), sl(896, 960), z(64),
                                sl(960, 1088)], axis=axis)

    w_rw = pad_rw(rwc, 1)
    mu = pad_rw(rw_mu[l], 1)
    pad_rows = lambda a: jnp.concatenate([a, jnp.zeros((64, a.shape[1]), a.dtype)], axis=0)
    w2 = jnp.stack([pad_rows(rw_w2[l, 0]), pad_rows(rw_w2[l, 1])]).astype(BF16)
    a2 = pad_rows(rw_a2[l]).astype(BF16)

    q = swc[:, 0:GW]
    kx = jnp.concatenate([swc[:, 256:320], swc[:, 256:320], swc[:, 320:384], swc[:, 320:384]], axis=1)
    vx = jnp.concatenate([swc[:, 384:448], swc[:, 384:448], swc[:, 448:512], swc[:, 448:512]], axis=1)
    perm = _rot_perm(GW, 32)
    w_swa = jnp.concatenate([q, kx, vx, q[:, perm], kx[:, perm]], axis=1)

    kr = mlc[:, 320:352]
    krp = kr[:, _rot_perm(MLA_ROPE, 16)]
    w_mla = jnp.concatenate([mlc[:, 0:192], zc(64), mlc[:, 192:320], zc(64), kr, zc(32), zc(64), krp, zc(32)], axis=1)

    wuq = mla_wuq[l]
    wukv = mla_wukv[l]
    rp = _rot_perm(MLA_ROPE, 16)
    wq, wqr, wk, wv = [], [], [], []
    zq = lambda r, n: jnp.zeros((r, n), F32)
    for h in range(NH):
        qh = wuq[:, 96 * h:96 * (h + 1)]
        a = jnp.concatenate([qh, zq(192, 32)], axis=1)
        ar = jnp.concatenate([zq(192, 64), qh[:, 64:96][:, rp], zq(192, 32)], axis=1)
        wq.append(jnp.concatenate([a, zq(64, 128)], axis=0))
        wqr.append(jnp.concatenate([ar, zq(64, 128)], axis=0))
        wk.append(jnp.concatenate([wukv[:, 128 * h:128 * h + 64], zq(128, 64)], axis=1))
        wv.append(wukv[:, 128 * h + 64:128 * (h + 1)].T)
    row = lambda a: a.reshape(1, -1)
    return {
        "proj": jnp.concatenate([w_hg, w_rw, w_swa, w_mla], axis=1).astype(BF16),
        "rw": {"mu": mu, "w0": rw_w0[l], "w2": w2, "a0": row(rw_a0[l]), "a2": a2, "g2": rw_g2[l].astype(BF16),
               "k_k": row(rw_k_k[l]), "k_a": row(rw_k_a[l])},
        "rk": row(rw_r_k[l]), "lng": row(rw_ln_g[l]), "lnb": row(rw_ln_b[l]),
        "mla": {"qg": jnp.concatenate([row(mla_qnorm_g[l]), jnp.zeros((1, 64), F32)], axis=1),
                "kg": row(mla_kvnorm_g[l]),
                "wq": jnp.stack(wq).astype(BF16), "wqr": jnp.stack(wqr).astype(BF16),
                "wk": jnp.stack(wk).astype(BF16), "wv": jnp.stack(wv).astype(BF16)},
    }


def _rope_tables(T, lc):
    rows = T // GRID_W
    rowp = jnp.repeat(jnp.arange(rows, dtype=F32), GRID_W)
    colp = jnp.tile(jnp.arange(GRID_W, dtype=F32), rows)

    def table(rot_dim, width, lane0):
        nf = rot_dim // 4
        inv = ROPE_BASE ** (-jnp.arange(nf, dtype=F32) / nf)
        j = np.arange(rot_dim)
        ax = j // (2 * nf)
        fr = j % nf
        sign = np.where(j % (2 * nf) < nf, -1.0, 1.0).astype(np.float32)
        pos = jnp.stack([rowp, colp], axis=1)
        ang = pos[:, ax] * inv[fr][None, :]
        cos = jnp.concatenate([jnp.ones((lc, rot_dim), F32), jnp.cos(ang)], axis=0)
        sin = jnp.concatenate([jnp.zeros((lc, rot_dim), F32), jnp.sin(ang) * sign[None, :]], axis=0)
        reps = (width - lane0) // rot_dim if lane0 == 0 else 1
        cos = jnp.tile(cos, (1, reps))
        sin = jnp.tile(sin, (1, reps))
        return cos, sin

    cs_s, sn_s = table(HD, 2 * GW, 0)
    c32, s32 = table(MLA_ROPE, MLA_ROPE, 0)
    S = lc + T
    z = lambda n: jnp.zeros((S, n), F32)
    o = lambda n: jnp.ones((S, n), F32)
    cs_m = jnp.concatenate([z(64), c32, z(32)], axis=1)
    sn_m = jnp.concatenate([z(64), s32, z(32)], axis=1)
    cs_q = jnp.concatenate([o(64), c32, z(32)], axis=1)
    sn_q = jnp.concatenate([z(64), s32, z(32)], axis=1)
    return {"cs_s": cs_s, "sn_s": sn_s, "cs_m": cs_m, "sn_m": sn_m, "cs_q": cs_q, "sn_q": sn_q}


def kernel(x, c, ctx, c_ctx, w_ada, b_ada, norm1_g, norm2_g, w_in, hg_lb, hg_norm_g, rw_mu, rw_w0, rw_w2, rw_a0,
           rw_a2, rw_g2, rw_k_k, rw_k_a, rw_r_k, rw_ln_g, rw_ln_b, swa_sink, mla_qnorm_g, mla_wuq, mla_kvnorm_g,
           mla_wukv, w_out, ffn_w1, ffn_w3, ffn_w2, moe_router, moe_w1, moe_w3, moe_w2, final_norm_g):
    B, T, _ = x.shape
    lc = ctx.shape[1]
    S = lc + T
    L = w_in.shape[0]
    assert lc % TM == 0 and T % TM == 0 and lc % SWA_W == 0 and T % GRID_W == 0 and (B * S) % 1024 == 0

    r8 = -(-(B + 1) // 8) * 8
    c8 = jnp.concatenate([c, c_ctx[None, :], jnp.zeros((r8 - B - 1, D), F32)], axis=0)
    mod_all = _ada(c8, w_ada, b_ada)
    tabs = _rope_tables(T, lc)
    xs = jnp.concatenate([ctx, x], axis=1)
    row = lambda a: a.reshape(1, -1)

    pend = None
    for l in range(L):
        w = _layer_weights(l, w_in, rw_mu, rw_w0, rw_w2, rw_a0, rw_a2, rw_g2, rw_k_k, rw_k_a, rw_r_k, rw_ln_g,
                           rw_ln_b, mla_qnorm_g, mla_wuq, mla_kvnorm_g, mla_wukv)
        mod = mod_all[l]
        xs, p_hg, p_rw, p_swa, p_mla = _proj(xs, pend, mod, row(norm1_g[l]), w["proj"], tabs, lc)

        hgo = _hgrn2(p_hg, hg_lb, l, lc)
        rwp = _rwprep(p_rw, w["rw"], lc)
        rwo = _rwscan(rwp, lc)
        sink8 = jnp.zeros((8, 128), F32).at[0:NH, :].set(jnp.broadcast_to(swa_sink[l][:, None], (NH, 128)))
        swo = _swa(p_swa, sink8, lc)
        mq, mk, mv = _mlaprep(p_mla, w["mla"], tabs)
        mlo = _mla(mq, mk, mv, lc)

        moe = l % 2 == 1
        router = None
        if moe:
            router = jnp.concatenate([moe_router[l // 2], jnp.zeros((D, 128 - N_EXPERTS), F32)], axis=1)
        outs = _outproj(xs, mod, row(norm2_g[l]), hgo, p_hg, row(hg_norm_g[l]), rwo, rwp, w["rk"], w["lng"],
                        w["lnb"], swo, mlo, w_out[l].astype(BF16), router, lc)
        xs, h2 = outs[0], outs[1]
        h2f = h2.reshape(B * S, D)
        if moe:
            e = l // 2
            y = _moe(h2f, outs[2].reshape(B * S, 128), moe_w1[e].astype(BF16), moe_w3[e].astype(BF16),
                     moe_w2[e].astype(BF16))
        else:
            e = l // 2
            y = _ffn(h2f, ffn_w1[e].astype(BF16), ffn_w3[e].astype(BF16), ffn_w2[e].astype(BF16))
        pend = (y.reshape(B, S, D), mod)

    return _final(xs, pend[0], pend[1], row(final_norm_g), lc)
```

```python
import functools

import jax
import jax.numpy as jnp
import numpy as np
from jax import lax
from jax.experimental import pallas as pl
from jax.experimental.pallas import tpu as pltpu

F32 = jnp.float32
BF16 = jnp.bfloat16

D = 1024
GW = 256
HD = 64
NH = 4
GRID_W = 64
NORM_EPS = 1e-6
ROPE_BASE = 10000.0
HG_F_FLOOR = 1e-30
RW_LN_EPS = 64e-5
SWA_W = 128
MLA_Q_RANK = 192
MLA_KV_RANK = 128
MLA_NOPE = 64
MLA_ROPE = 32
D_FF = 2816
N_EXPERTS = 8
D_FF_EXPERT = 1408

TM = 256
CH = 64
SCAN_ROWS = 128
HB = 16
RW_BPS = 2
MLA_HPS = 4
MLA_VROWS = 80
LOG2E = 1.4426950408889634
MLA_KCHUNK = 256
MLA_SKEW = 8
VMEM_LIMIT = 56 * 1024 * 1024
RW_GRAM_MODE = "bf16"
RW_BASE_MODE = "bf16"
RW_MERGE_MODE = "bf16"
RW_STATE_MODE = "x3"

_NN = (((1,), (0,)), ((), ()))
_NT = (((1,), (1,)), ((), ()))
_TN = (((0,), (0,)), ((), ()))


def _mm(a, b, dims=_NN, mode="bf16"):
    if mode == "bf16":
        return lax.dot_general(a.astype(BF16), b.astype(BF16), dims, preferred_element_type=F32)
    if mode == "x3":
        ah = a.astype(BF16)
        al = (a - ah.astype(F32)).astype(BF16)
        bh = b.astype(BF16)
        bl = (b - bh.astype(F32)).astype(BF16)
        d = lambda x, y: lax.dot_general(x, y, dims, preferred_element_type=F32)
        return d(ah, bh) + (d(ah, bl) + d(al, bh))
    return lax.dot_general(a, b, dims, precision=lax.Precision.HIGHEST, preferred_element_type=F32)


def _sigmoid(x):
    return 1.0 / (1.0 + jnp.exp(-x))


def _silu(x):
    return x / (1.0 + jnp.exp(-x))


def _iota(shape, dim):
    return lax.broadcasted_iota(jnp.int32, shape, dim)


def _head_block_mask(n):
    return (_iota((n, n), 0) // HD) == (_iota((n, n), 1) // HD)


def _running_sum(x, fwd, seg):
    n = x.shape[0]
    pos = _iota(x.shape, 0) % seg
    sh = 1
    while sh < seg:
        if fwd:
            x = x + jnp.where(pos >= sh, pltpu.roll(x, sh, 0), 0.0)
        else:
            x = x + jnp.where(pos < seg - sh, pltpu.roll(x, n - sh, 0), 0.0)
        sh *= 2
    return x


def _alternate(*stages, skew=0):
    live = list(enumerate(stages))
    rnd = 0
    while live:
        for item in list(live):
            i, g = item
            if rnd < i * skew:
                continue
            try:
                next(g)
            except StopIteration:
                live.remove(item)
        rnd += 1


def _cparams(sem):
    return pltpu.CompilerParams(dimension_semantics=sem, vmem_limit_bytes=VMEM_LIMIT)


def _ada_body(c_ref, w_ref, b_ref, o_ref):
    s = _silu(c_ref[...])
    o_ref[0] = _mm(s, w_ref[0]) + b_ref[0]


def _ada(c8, w_ada, b_ada):
    L = w_ada.shape[0]
    r8 = c8.shape[0]
    tn = 1536
    return pl.pallas_call(
        _ada_body,
        grid=(L, 6 * D // tn),
        in_specs=[
            pl.BlockSpec((r8, D), lambda l, j: (0, 0)),
            pl.BlockSpec((1, D, tn), lambda l, j: (l, 0, j)),
            pl.BlockSpec((1, 1, tn), lambda l, j: (l, 0, j)),
        ],
        out_specs=pl.BlockSpec((1, r8, tn), lambda l, j: (l, 0, j)),
        out_shape=jax.ShapeDtypeStruct((L, r8, 6 * D), F32),
        compiler_params=_cparams(("parallel", "parallel")),
        name="adaln",
    )(c8, w_ada, b_ada.reshape(L, 1, 6 * D))


def _proj_body(*refs, pending, lc_tiles, nb):
    if pending:
        (x_ref, y_ref, modp_ref, mod_ref, g_ref, wall, cs_s, sn_s, cs_m, sn_m,
         xo_ref, phg, prw, pswa, pmla) = refs
    else:
        (x_ref, mod_ref, g_ref, wall, cs_s, sn_s, cs_m, sn_m,
         phg, prw, pswa, pmla) = refs
    b = pl.program_id(0)
    i = pl.program_id(1)
    row = jnp.where(i < lc_tiles, nb, b)
    x = x_ref[0]
    if pending:
        gate = modp_ref[pl.ds(row, 1), 5 * D:6 * D]
        x = x + gate * y_ref[0]
        xo_ref[0] = x
    sh = mod_ref[pl.ds(row, 1), 0:D]
    sc = mod_ref[pl.ds(row, 1), D:2 * D]
    ms = jnp.mean(x * x, axis=-1, keepdims=True)
    h = (x * lax.rsqrt(ms + NORM_EPS) * g_ref[...]) * (1.0 + sc) + sh
    hb = h.astype(BF16)
    n1 = 5 * GW
    phg[0] = _mm(hb, wall[:, 0:n1])
    prw[0] = _mm(hb, wall[:, n1:2 * n1])
    s = _mm(hb, wall[:, 2 * n1:3 * n1])
    qk = s[:, 0:2 * GW] * cs_s[...] + s[:, 3 * GW:5 * GW] * sn_s[...]
    pswa[0, :, 0:2 * GW] = qk.astype(BF16)
    pswa[0, :, 2 * GW:3 * GW] = s[:, 2 * GW:3 * GW].astype(BF16)
    m = _mm(hb, wall[:, 3 * n1:3 * n1 + 640])
    pmla[0, :, 0:384] = m[:, 0:384]
    pmla[0, :, 384:512] = m[:, 384:512] * cs_m[...] + m[:, 512:640] * sn_m[...]


def _proj(x, pend, mod, g, w, tabs, lc):
    B, S, _ = x.shape
    nt = S // TM
    pending = pend is not None
    row_spec = lambda n: pl.BlockSpec((1, TM, n), lambda b, i: (b, i, 0))
    full = lambda a: pl.BlockSpec(a.shape, lambda b, i: (0,) * a.ndim)
    tab_spec = lambda a: pl.BlockSpec((TM, a.shape[1]), lambda b, i: (i, 0))
    ins = [x]
    in_specs = [row_spec(D)]
    if pending:
        ins += [pend[0], pend[1]]
        in_specs += [row_spec(D), full(pend[1])]
    ins += [mod, g, w, tabs["cs_s"], tabs["sn_s"], tabs["cs_m"], tabs["sn_m"]]
    in_specs += [full(mod), full(g), full(w),
                 tab_spec(tabs["cs_s"]), tab_spec(tabs["sn_s"]), tab_spec(tabs["cs_m"]), tab_spec(tabs["sn_m"])]
    out_shape = [jax.ShapeDtypeStruct((B, S, 5 * GW), F32), jax.ShapeDtypeStruct((B, S, 5 * GW), F32),
                 jax.ShapeDtypeStruct((B, S, 3 * GW), BF16), jax.ShapeDtypeStruct((B, S, 512), F32)]
    out_specs = [row_spec(5 * GW), row_spec(5 * GW), row_spec(3 * GW), row_spec(512)]
    if pending:
        out_shape = [jax.ShapeDtypeStruct((B, S, D), F32)] + out_shape
        out_specs = [row_spec(D)] + out_specs
    outs = pl.pallas_call(
        functools.partial(_proj_body, pending=pending, lc_tiles=lc // TM, nb=B),
        grid=(B, nt), in_specs=in_specs, out_specs=out_specs, out_shape=out_shape,
        compiler_params=_cparams(("parallel", "parallel")),
        name="proj",
    )(*ins)
    if pending:
        return tuple(outs)
    return (x,) + tuple(outs)


def _chunk_of_step(d, s, nc, nc_ctx):
    bwd = jnp.where(s < nc_ctx, nc_ctx - 1 - s, nc - 1 - (s - nc_ctx))
    return jnp.where(d == 0, s, bwd)


def _hg_direction(p_ref, lb, o_ref, st_ref, fwd, c0):
    p = p_ref[0, c0:c0 + CH, :]
    z = p[:, GW:2 * GW] if fwd else p[:, 2 * GW:3 * GW]
    f = lb + (1.0 - lb) * _sigmoid(z)
    logf = jnp.log(jnp.maximum(f, HG_F_FLOOR))
    key = (1.0 - lb) * _sigmoid(-z)
    qs = _silu(p[:, 0:GW])
    iv = p[:, 3 * GW:4 * GW]

    bcum = _running_sum(logf, fwd, HB)

    bd = _head_block_mask(GW)
    bdf = bd.astype(BF16)
    half = HB // 2
    nblk = CH // HB
    for j in range(nblk):
        r0 = (j if fwd else nblk - 1 - j) * HB
        qb, kb, ib, bb = (a[r0:r0 + HB] for a in (qs, key, iv, bcum))
        tot = bb[HB - 1:HB] if fwd else bb[0:1]
        st = st_ref[...]
        o_blk = _mm(qb * jnp.exp(bb), st, _NT)
        es = []
        spans = []
        for sidx in range(HB):
            if fwd:
                lo, hi = (0, HB) if sidx < half else (half, HB)
            else:
                lo, hi = (0, half) if sidx < half else (0, HB)
            rows = _iota((hi - lo, 1), 0) + lo
            m = (rows >= sidx) if fwd else (rows <= sidx)
            dec = jnp.exp(jnp.where(m, bb[lo:hi] - bb[sidx:sidx + 1], -jnp.inf))
            es.append(qb[lo:hi] * kb[sidx:sidx + 1] * dec)
            spans.append((lo, hi))
        yield
        att = _mm(jnp.concatenate(es, axis=0), bdf)
        top = o_blk[0:half]
        bot = o_blk[half:HB]
        off = 0
        for sidx, (lo, hi) in enumerate(spans):
            a = att[off:off + hi - lo] * ib[sidx:sidx + 1]
            off += hi - lo
            if hi - lo == HB:
                top = top + a[0:half]
                bot = bot + a[half:HB]
            elif lo == 0:
                top = top + a
            else:
                bot = bot + a
        o_ref[0, c0 + r0:c0 + r0 + half, :] = top
        o_ref[0, c0 + r0 + half:c0 + r0 + HB, :] = bot
        outer = _mm(ib, kb * jnp.exp(tot - bb), _TN)
        st_ref[...] = jnp.where(bd, st * jnp.exp(tot) + outer, 0.0)
        yield


def _hg_body(pf_ref, pb_ref, lbraw_ref, of_ref, ob_ref, stf_ref, stb_ref, *, layer):
    @pl.when(pl.program_id(1) == 0)
    def _():
        stf_ref[...] = jnp.zeros_like(stf_ref)
        stb_ref[...] = jnp.zeros_like(stb_ref)

    raw = lbraw_ref[...]
    nl = raw.shape[0]
    mx = raw[0]
    for j in range(1, nl):
        mx = jnp.maximum(mx, raw[j])
    ex = [jnp.exp(raw[j] - mx) for j in range(nl)]
    den = ex[0]
    for j in range(1, nl):
        den = den + ex[j]
    lb2 = jnp.zeros_like(mx)
    for j in range(1, layer + 1):
        lb2 = lb2 + ex[j] / den
    def scan(p_ref, lb, o_ref, st_ref, fwd):
        chunks = range(SCAN_ROWS // CH)
        for c in (chunks if fwd else reversed(chunks)):
            yield from _hg_direction(p_ref, lb, o_ref, st_ref, fwd, c * CH)

    _alternate(scan(pf_ref, lb2[0:1], of_ref, stf_ref, True), scan(pb_ref, lb2[1:2], ob_ref, stb_ref, False))


def _hgrn2(p_hg, hg_lb, layer, lc):
    B, S, _ = p_hg.shape
    rs = SCAN_ROWS
    nc = S // rs
    nc_ctx = lc // rs
    fmap = lambda b, s: (b, s, 0)
    bmap = lambda b, s: (b, _chunk_of_step(1, s, nc, nc_ctx), 0)
    return pl.pallas_call(
        functools.partial(_hg_body, layer=layer),
        grid=(B, nc),
        in_specs=[pl.BlockSpec((1, rs, 5 * GW), fmap), pl.BlockSpec((1, rs, 5 * GW), bmap),
                  pl.BlockSpec(hg_lb.shape, lambda b, s: (0, 0, 0))],
        out_specs=[pl.BlockSpec((1, rs, GW), fmap), pl.BlockSpec((1, rs, GW), bmap)],
        out_shape=[jax.ShapeDtypeStruct((B, S, GW), F32)] * 2,
        scratch_shapes=[pltpu.VMEM((GW, GW), F32)] * 2,
        compiler_params=_cparams(("parallel", "arbitrary")),
        name="hgrn2",
    )(p_hg, p_hg, hg_lb)


def _rwprep_body(cur_ref, prv_ref, nxt_ref, mu_ref, w0_ref, w2_ref, a0_ref, a2_ref, g2_ref, kk_ref, ka_ref,
                 o_ref, buf, *, lc_tiles):
    i = pl.program_id(1)
    nt = pl.num_programs(1)
    first = (i == 0) | (i == lc_tiles)
    lastt = (i == lc_tiles - 1) | (i == nt - 1)
    buf[0:8, :] = jnp.where(first, 0.0, prv_ref[0])
    buf[8:8 + TM, :] = cur_ref[0]
    buf[8 + TM:16 + TM, :] = jnp.where(lastt, 0.0, nxt_ref[0])
    cur = buf[8:8 + TM, :]
    prev = buf[7:7 + TM, :]
    nxt = buf[9:9 + TM, :]
    sft = cur + mu_ref[0:1, :] * (prev - cur) + mu_ref[1:2, :] * (nxt - cur)
    r = sft[:, 0:GW]
    k = sft[:, GW:2 * GW]
    v = sft[:, 2 * GW:3 * GW]
    ywf = sft[:, 768:896]
    ywb = sft[:, 896:1024]
    ya = sft[:, 1024:1152]
    yg = sft[:, 1152:1280]
    c = float(np.exp(-0.5))
    lwf = -c * _sigmoid(w0_ref[0:1, :] + _mm(jnp.tanh(ywf), w2_ref[0]))
    lwb = -c * _sigmoid(w0_ref[1:2, :] + _mm(jnp.tanh(ywb), w2_ref[1]))
    a = _sigmoid(a0_ref[...] + _mm(ya, a2_ref[...]))
    kk0 = k * kk_ref[...]
    ss = _mm(kk0 * kk0, _head_block_mask(GW).astype(F32), mode="x3")
    kk = kk0 / jnp.maximum(jnp.sqrt(ss), 1e-12)
    kp = k * (1.0 + (a - 1.0) * ka_ref[...])
    g = _mm(_sigmoid(yg), g2_ref[...])
    o_ref[0, :, 0:GW] = r
    o_ref[0, :, GW:2 * GW] = lwf
    o_ref[0, :, 2 * GW:3 * GW] = lwb
    o_ref[0, :, 3 * GW:4 * GW] = kp
    o_ref[0, :, 4 * GW:5 * GW] = v
    o_ref[0, :, 5 * GW:6 * GW] = kk
    o_ref[0, :, 6 * GW:7 * GW] = kk * a
    o_ref[0, :, 7 * GW:8 * GW] = g


def _rwprep(p_rw, w, lc):
    B, S, n = p_rw.shape
    nt = S // TM
    t8 = TM // 8
    full = lambda a: pl.BlockSpec(a.shape, lambda b, i: (0,) * a.ndim)
    small = [w["mu"], w["w0"], w["w2"], w["a0"], w["a2"], w["g2"], w["k_k"], w["k_a"]]
    return pl.pallas_call(
        functools.partial(_rwprep_body, lc_tiles=lc // TM),
        grid=(B, nt),
        in_specs=[pl.BlockSpec((1, TM, n), lambda b, i: (b, i, 0)),
                  pl.BlockSpec((1, 8, n), lambda b, i: (b, jnp.maximum(i * t8 - 1, 0), 0)),
                  pl.BlockSpec((1, 8, n), lambda b, i: (b, jnp.minimum((i + 1) * t8, S // 8 - 1), 0))]
                 + [full(a) for a in small],
        out_specs=pl.BlockSpec((1, TM, 8 * GW), lambda b, i: (b, i, 0)),
        out_shape=jax.ShapeDtypeStruct((B, S, 8 * GW), F32),
        scratch_shapes=[pltpu.VMEM((TM + 16, n), F32)],
        compiler_params=_cparams(("parallel", "parallel")),
        name="rwkv_prep",
    )(p_rw, p_rw, p_rw, *small)


def _rw_direction(x_ref, o_ref, h_ref, fwd, bi, c0):
    x = x_ref[bi, c0:c0 + CH, :]
    r = x[:, 0:GW]
    lw = x[:, GW:2 * GW] if fwd else x[:, 2 * GW:3 * GW]
    k = x[:, 3 * GW:4 * GW]
    v = x[:, 4 * GW:5 * GW]
    kk = x[:, 5 * GW:6 * GW]
    bb = x[:, 6 * GW:7 * GW]

    incl = _running_sum(lw, fwd, CH)
    tot = incl[CH - 1:CH] if fwd else incl[0:1]
    e_in = jnp.exp(incl)
    e_ni = jnp.exp(-incl)
    e_tt = jnp.exp(tot - incl)
    km = kk * jnp.exp(incl - lw)
    rt = r * e_in
    bh = bb * e_ni
    kh = k * e_ni
    bt = bb * e_tt
    kt = k * e_tt

    n = NH * CH
    hm = (_iota((n, GW), 0) // CH) == (_iota((n, GW), 1) // HD)
    tile4 = lambda a: jnp.concatenate([a] * NH, axis=0)
    expand = lambda a: jnp.where(hm, tile4(a), 0.0)
    km_e = expand(km)
    rt_e = expand(rt)
    v_e = expand(v)
    bt_e = expand(bt)
    kt_e = expand(kt)
    bh_t = tile4(bh)
    kh_t = tile4(kh)

    rr = _iota((n, n), 0)
    cc = _iota((n, n), 1)
    same = (rr // CH) == (cc // CH)
    rt_i = rr % CH
    ct_i = cc % CH
    before = same & ((ct_i < rt_i) if fwd else (ct_i > rt_i))
    upto = same & ((ct_i <= rt_i) if fwd else (ct_i >= rt_i))
    eye = rr == cc

    pg = RW_GRAM_MODE
    a_m = jnp.where(before, _mm(km_e, bh_t, _NT, pg), 0.0)
    yield

    blk = lambda m: (rr // m) == (cc // m)
    pb = RW_BASE_MODE
    nm = jnp.where(blk(8), -a_m, 0.0)
    x2 = _mm(nm, nm, mode=pb)
    b_m = jnp.where(before, _mm(km_e, kh_t, _NT, pg), 0.0)
    yield
    tm = jnp.where(eye, 1.0, 0.0) + nm
    tm = tm + _mm(tm, x2, mode=pb)
    x4 = _mm(x2, x2, mode=pb)
    yield
    tm = tm + _mm(tm, x4, mode=pb)
    mrb = jnp.where(upto, _mm(rt_e, bh_t, _NT, pg), 0.0)
    yield
    pm = RW_MERGE_MODE
    side = [lambda: jnp.where(upto, _mm(rt_e, kh_t, _NT, pg), 0.0), lambda: _mm(b_m, v_e, mode=pg),
            lambda: _mm(kt_e, v_e, _TN, pg)]
    done = []
    for m in (8, 16, 32):
        low = jnp.where(blk(2 * m) & (rr // m != cc // m), a_m, 0.0)
        y = _mm(low, tm, mode=pm)
        done.append(side.pop(0)())
        yield
        tm = tm - _mm(tm, y, mode=pm)
        yield
    mrk, bmv, ktv = done

    p1 = _mm(tm, km_e, mode=pg)
    q1 = _mm(tm, bmv, mode=pg)
    yield
    p2e = rt_e - _mm(mrb, p1, mode=pg)
    q2e = _mm(mrk, v_e, mode=pg) - _mm(mrb, q1, mode=pg)
    g_m = jnp.where(eye, jnp.exp(tot), 0.0) - _mm(bt_e, p1, _TN, pg)
    f_m = ktv - _mm(bt_e, q1, _TN, pg)
    yield
    fold = lambda a: a[0:CH] + a[CH:2 * CH] + a[2 * CH:3 * CH] + a[3 * CH:4 * CH]
    p2 = fold(p2e)
    q2 = fold(q2e)
    h = h_ref[bi]
    ps = RW_STATE_MODE
    o_ref[bi, c0:c0 + CH, :] = _mm(p2, h, mode=ps) + q2
    h_ref[bi] = _mm(g_m, h, mode=ps) + f_m


def _rw_body(xf_ref, xb_ref, of_ref, ob_ref, hf_ref, hb_ref):
    @pl.when(pl.program_id(1) == 0)
    def _():
        hf_ref[...] = jnp.zeros_like(hf_ref)
        hb_ref[...] = jnp.zeros_like(hb_ref)

    def scan(x_ref, o_ref, h_ref, fwd, bi):
        chunks = range(SCAN_ROWS // CH)
        for c in (chunks if fwd else reversed(chunks)):
            yield from _rw_direction(x_ref, o_ref, h_ref, fwd, bi, c * CH)

    chains = []
    for bi in range(xf_ref.shape[0]):
        chains.append(scan(xf_ref, of_ref, hf_ref, True, bi))
        chains.append(scan(xb_ref, ob_ref, hb_ref, False, bi))
    _alternate(*chains)


def _rwscan(prep, lc):
    B, S, n = prep.shape
    rs = SCAN_ROWS
    nc = S // rs
    nc_ctx = lc // rs
    bps = RW_BPS if B % RW_BPS == 0 else 1
    fmap = lambda b, s: (b, s, 0)
    bmap = lambda b, s: (b, _chunk_of_step(1, s, nc, nc_ctx), 0)
    return pl.pallas_call(
        _rw_body,
        grid=(B // bps, nc),
        in_specs=[pl.BlockSpec((bps, rs, n), fmap), pl.BlockSpec((bps, rs, n), bmap)],
        out_specs=[pl.BlockSpec((bps, rs, GW), fmap), pl.BlockSpec((bps, rs, GW), bmap)],
        out_shape=[jax.ShapeDtypeStruct((B, S, GW), F32)] * 2,
        scratch_shapes=[pltpu.VMEM((bps, GW, GW), F32)] * 2,
        compiler_params=_cparams(("parallel", "arbitrary")),
        name="rwkv_scan",
    )(prep, prep)


def _swa_block(q, kloc, vloc, kc, vc, sink_ref, o_ref, r0, j, lc_blocks, nbl):
    w = SWA_W
    is_lat = j >= lc_blocks
    nblk = j - lc_blocks
    lane_h = _iota((w, GW), 1) // HD
    qs = jnp.concatenate([jnp.where(lane_h == h, q, jnp.zeros_like(q)) for h in range(NH)], axis=0)
    scale = HD ** -0.5
    s_loc = _mm(qs, kloc, _NT) * scale
    s_ctx = _mm(qs, kc, _NT) * scale
    yield
    qi = _iota((NH * w, 3 * w), 0) % w
    col = _iota((NH * w, 3 * w), 1)
    seg = col // w
    kj = col % w
    valid = ((seg == 0) & (nblk >= 1) & (kj >= qi)) | (seg == 1) | ((seg == 2) & (nblk < nbl - 1) & (kj <= qi))
    valid = valid & is_lat
    s_loc = jnp.where(valid, s_loc, -jnp.inf)
    hrow = _iota((NH * w, 1), 0) // w
    sink = jnp.zeros((NH * w, 1), F32)
    for h in range(NH):
        sink = jnp.where(hrow == h, sink_ref[h:h + 1, 0:1], sink)
    m = jnp.maximum(jnp.maximum(jnp.max(s_loc, axis=-1, keepdims=True), jnp.max(s_ctx, axis=-1, keepdims=True)), sink)
    p_loc = jnp.exp(s_loc - m)
    p_ctx = jnp.exp(s_ctx - m)
    den = jnp.sum(p_loc, axis=-1, keepdims=True) + jnp.sum(p_ctx, axis=-1, keepdims=True) + jnp.exp(sink - m)
    yield
    of = (_mm(p_loc, vloc) + _mm(p_ctx, vc)) / den
    out = jnp.zeros((w, GW), F32)
    for h in range(NH):
        out = out + jnp.where(lane_h == h, of[h * w:(h + 1) * w], 0.0)
    o_ref[0, r0:r0 + w, :] = out
    yield


def _swa_body(pair_ref, prv_ref, nxt_ref, ctx_ref, sink_ref, o_ref, *, lc_blocks, nbl, i0):
    i = pl.program_id(1) + i0
    w = SWA_W
    kcol = slice(GW, 2 * GW)
    vcol = slice(2 * GW, 3 * GW)
    ka, kb = pair_ref[0, 0:w, kcol], pair_ref[0, w:2 * w, kcol]
    va, vb = pair_ref[0, 0:w, vcol], pair_ref[0, w:2 * w, vcol]
    kc = ctx_ref[0, :, kcol]
    vc = ctx_ref[0, :, vcol]
    cat = lambda *a: jnp.concatenate(a, axis=0)
    first = _swa_block(pair_ref[0, 0:w, 0:GW], cat(prv_ref[0, :, kcol], ka, kb), cat(prv_ref[0, :, vcol], va, vb),
                       kc, vc, sink_ref, o_ref, 0, 2 * i, lc_blocks, nbl)
    second = _swa_block(pair_ref[0, w:2 * w, 0:GW], cat(ka, kb, nxt_ref[0, :, kcol]), cat(va, vb, nxt_ref[0, :, vcol]),
                        kc, vc, sink_ref, o_ref, w, 2 * i + 1, lc_blocks, nbl)
    _alternate(first, second, skew=1)


def _swa(p_swa, sink8, lc, skip_ctx):
    B, S, n = p_swa.shape
    w = SWA_W
    nb = S // w
    lcb = lc // w
    assert nb % 2 == 0 and lcb % 2 == 0
    i0 = lcb // 2 if skip_ctx else 0
    return pl.pallas_call(
        functools.partial(_swa_body, lc_blocks=lcb, nbl=nb - lcb, i0=i0),
        grid=(B, nb // 2 - i0),
        in_specs=[pl.BlockSpec((1, 2 * w, n), lambda b, i: (b, i + i0, 0)),
                  pl.BlockSpec((1, w, n), lambda b, i: (b, jnp.maximum(2 * (i + i0) - 1, 0), 0)),
                  pl.BlockSpec((1, w, n), lambda b, i: (b, jnp.minimum(2 * (i + i0) + 2, nb - 1), 0)),
                  pl.BlockSpec((1, lc, n), lambda b, i: (b, 0, 0)),
                  pl.BlockSpec(sink8.shape, lambda b, i: (0, 0))],
        out_specs=pl.BlockSpec((1, 2 * w, GW), lambda b, i: (b, i, 0)),
        out_shape=jax.ShapeDtypeStruct((B, S - 2 * w * i0, GW), F32),
        compiler_params=_cparams(("parallel", "parallel")),
        name="swa",
    )(p_swa, p_swa, p_swa, p_swa, sink8)


def _mlaprep_body(p_ref, qg_ref, kg_ref, wq, wqr, wk, wv, cs_ref, sn_ref, q_ref, k_ref, v_ref):
    p = p_ref[0]
    cq = p[:, 0:256]
    ckv = p[:, 256:384]
    krs = p[:, 384:512]
    qn = cq * lax.rsqrt(jnp.sum(cq * cq, axis=-1, keepdims=True) / MLA_Q_RANK + NORM_EPS) * qg_ref[...]
    kvn = ckv * lax.rsqrt(jnp.mean(ckv * ckv, axis=-1, keepdims=True) + NORM_EPS) * kg_ref[...]
    scale = (MLA_NOPE + MLA_ROPE) ** -0.5 * LOG2E
    cs = cs_ref[...]
    sn = sn_ref[...]
    kvn_t = kvn.T.astype(BF16)
    ones_rows = (_iota((MLA_VROWS - HD, TM), 0) == 0).astype(BF16)
    for h in range(NH):
        qh = _mm(qn, wq[h]) * cs + _mm(qn, wqr[h]) * sn
        q_ref[0, h] = (qh * scale).astype(BF16)
        k_ref[0, h] = (_mm(kvn, wk[h]) + krs).astype(BF16)
        v_ref[0, h, 0:HD, :] = _mm(wv[h], kvn_t).astype(BF16)
        v_ref[0, h, HD:MLA_VROWS, :] = ones_rows


def _mlaprep(p_mla, w, tabs):
    B, S, n = p_mla.shape
    nt = S // TM
    full = lambda a: pl.BlockSpec(a.shape, lambda b, i: (0,) * a.ndim)
    ws = [w["qg"], w["kg"], w["wq"], w["wqr"], w["wk"], w["wv"]]
    hspec = lambda c: pl.BlockSpec((1, NH, TM, c), lambda b, i: (b, 0, i, 0))
    return pl.pallas_call(
        _mlaprep_body,
        grid=(B, nt),
        in_specs=[pl.BlockSpec((1, TM, n), lambda b, i: (b, i, 0))] + [full(a) for a in ws]
                 + [pl.BlockSpec((TM, 128), lambda b, i: (i, 0))] * 2,
        out_specs=[hspec(128), hspec(128), pl.BlockSpec((1, NH, MLA_VROWS, TM), lambda b, i: (b, 0, 0, i))],
        out_shape=[jax.ShapeDtypeStruct((B, NH, S, 128), BF16), jax.ShapeDtypeStruct((B, NH, S, 128), BF16),
                   jax.ShapeDtypeStruct((B, NH, MLA_VROWS, S), BF16)],
        compiler_params=_cparams(("parallel", "parallel")),
        name="mla_prep",
    )(p_mla, *ws, tabs["cs_q"], tabs["sn_q"])


def _mla_body(q_ref, k_ref, v_ref, o_ref, *, lc, t0):
    i = pl.program_id(2) + t0
    ctx_q = i * TM < lc
    kc = MLA_KCHUNK
    nk = k_ref.shape[2] // kc

    def head(h):
        q = q_ref[0, h]
        ss = []
        m = None
        for c in range(nk):
            s = _mm(k_ref[0, h, c * kc:(c + 1) * kc, :], q, _NT)
            if c * kc >= lc and t0 * TM < lc:
                s = jnp.where(ctx_q, -jnp.inf, s)
            ss.append(s)
            cm = jnp.max(s, axis=0, keepdims=True)
            m = cm if m is None else jnp.maximum(m, cm)
            yield
        acc = None
        for c in range(nk):
            p = jnp.exp2(ss[c] - m)
            pv = _mm(v_ref[0, h, :, c * kc:(c + 1) * kc], p)
            acc = pv if acc is None else acc + pv
            yield
        o_ref[0, h] = acc[0:HD] / acc[HD:HD + 1]

    _alternate(*[head(h) for h in range(MLA_HPS)], skew=MLA_SKEW)


def _mla(q, k, v, lc, skip_ctx):
    B, H, S, _ = q.shape
    nt = S // TM
    hp = MLA_HPS
    t0 = lc // TM if skip_ctx else 0
    return pl.pallas_call(
        functools.partial(_mla_body, lc=lc, t0=t0),
        grid=(B, H // hp, nt - t0),
        in_specs=[pl.BlockSpec((1, hp, TM, 128), lambda b, h, i: (b, h, i + t0, 0)),
                  pl.BlockSpec((1, hp, S, 128), lambda b, h, i: (b, h, 0, 0)),
                  pl.BlockSpec((1, hp, MLA_VROWS, S), lambda b, h, i: (b, h, 0, 0))],
        out_specs=pl.BlockSpec((1, hp, HD, TM), lambda b, h, i: (b, h, 0, i)),
        out_shape=jax.ShapeDtypeStruct((B, H, HD, S - t0 * TM), F32),
        compiler_params=_cparams(("parallel", "parallel", "parallel")),
        name="mla_attn",
    )(q, k, v)


def _outproj_body(*refs, lc_tiles, nb, moe, t0):
    if moe:
        (x_ref, mod_ref, n2_ref, hgf_ref, hgb_ref, hgg_ref, hgn_ref, rwf_ref, rwb_ref, rwp_ref, rk_ref, lng_ref, lnb_ref,
         swo_ref, mlo_ref, wout_ref, rt_ref, xo_ref, h2_ref, lg_ref) = refs
    else:
        (x_ref, mod_ref, n2_ref, hgf_ref, hgb_ref, hgg_ref, hgn_ref, rwf_ref, rwb_ref, rwp_ref, rk_ref, lng_ref, lnb_ref,
         swo_ref, mlo_ref, wout_ref, xo_ref, h2_ref) = refs
    b = pl.program_id(0)
    i = pl.program_id(1) + t0
    row = jnp.where(i < lc_tiles, nb, b)
    bd = _head_block_mask(GW).astype(BF16)

    def hsum(a):
        hi = a.astype(BF16)
        lo = (a - hi.astype(F32)).astype(BF16)
        return _mm(hi, bd) + _mm(lo, bd)

    o = hgf_ref[0] + hgb_ref[0]
    hg = o * lax.rsqrt(hsum(o * o) / HD + NORM_EPS) * hgn_ref[...] * _silu(hgg_ref[0])

    o = rwf_ref[0] + rwb_ref[0]
    c = o - hsum(o) / HD
    on = c * lax.rsqrt(hsum(c * c) / HD + RW_LN_EPS) * lng_ref[...] + lnb_ref[...]
    r = rwp_ref[0, :, 0:GW]
    kp = rwp_ref[0, :, 3 * GW:4 * GW]
    v = rwp_ref[0, :, 4 * GW:5 * GW]
    g = rwp_ref[0, :, 7 * GW:8 * GW]
    rw = (on + hsum(r * kp * rk_ref[...]) * v) * g

    acc = _mm(hg, wout_ref[0:GW, :]) + _mm(rw, wout_ref[GW:2 * GW, :]) + _mm(swo_ref[0], wout_ref[2 * GW:3 * GW, :])
    for h in range(NH):
        acc = acc + _mm(mlo_ref[0, h], wout_ref[3 * GW + h * HD:3 * GW + (h + 1) * HD, :], _TN)

    x = x_ref[0] + mod_ref[pl.ds(row, 1), 2 * D:3 * D] * acc
    xo_ref[0] = x
    sh = mod_ref[pl.ds(row, 1), 3 * D:4 * D]
    sc = mod_ref[pl.ds(row, 1), 4 * D:5 * D]
    ms = jnp.mean(x * x, axis=-1, keepdims=True)
    h2 = (x * lax.rsqrt(ms + NORM_EPS) * n2_ref[...]) * (1.0 + sc) + sh
    h2_ref[0] = h2.astype(BF16)
    if moe:
        lg_ref[0] = _mm(h2, rt_ref[...], mode="x3")


def _outproj(x, mod, n2g, hgo, p_hg, hgn, rwo, rwp, rk, lng, lnb, swo, mlo, wout, router, lc, skip_ctx):
    B, S, _ = x.shape
    nt = S // TM
    moe = router is not None
    t0 = lc // TM if skip_ctx else 0
    so = S - t0 * TM
    row = lambda n: pl.BlockSpec((1, TM, n), lambda b, i: (b, i + t0, 0))
    lat = lambda n: pl.BlockSpec((1, TM, n), lambda b, i: (b, i, 0))
    full = lambda a: pl.BlockSpec(a.shape, lambda b, i: (0,) * a.ndim)
    ins = [x, mod, n2g, hgo[0], hgo[1], p_hg, hgn, rwo[0], rwo[1], rwp, rk, lng, lnb, swo, mlo, wout]
    in_specs = [row(D), full(mod), full(n2g), row(GW), row(GW),
                pl.BlockSpec((1, TM, GW), lambda b, i: (b, i + t0, 4)), full(hgn),
                row(GW), row(GW), row(8 * GW), full(rk), full(lng), full(lnb), lat(GW),
                pl.BlockSpec((1, NH, HD, TM), lambda b, i: (b, 0, 0, i)), full(wout)]
    out_shape = [jax.ShapeDtypeStruct((B, so, D), F32), jax.ShapeDtypeStruct((B, so, D), BF16)]
    out_specs = [lat(D), lat(D)]
    if moe:
        ins.append(router)
        in_specs.append(full(router))
        out_shape.append(jax.ShapeDtypeStruct((B, so, 128), F32))
        out_specs.append(lat(128))
    return pl.pallas_call(
        functools.partial(_outproj_body, lc_tiles=lc // TM, nb=B, moe=moe, t0=t0),
        grid=(B, nt - t0), in_specs=in_specs, out_specs=out_specs, out_shape=out_shape,
        compiler_params=_cparams(("parallel", "parallel")),
        name="outproj",
    )(*ins)


def _ffn_body(h_ref, w1_ref, w3_ref, w2_ref, o_ref, acc_ref):
    j = pl.program_id(1)

    @pl.when(j == 0)
    def _():
        acc_ref[...] = jnp.zeros_like(acc_ref)

    h = h_ref[...]
    a = _mm(h, w1_ref[...])
    z = _silu(a) * _mm(h, w3_ref[...])
    acc_ref[...] += _mm(z, w2_ref[...])

    @pl.when(j == pl.num_programs(1) - 1)
    def _():
        o_ref[...] = acc_ref[...]


def _ffn(h2, w1, w3, w2):
    M = h2.shape[0]
    tm = 1024
    tf = D_FF // 2
    return pl.pallas_call(
        _ffn_body,
        grid=(M // tm, D_FF // tf),
        in_specs=[pl.BlockSpec((tm, D), lambda i, j: (i, 0)),
                  pl.BlockSpec((D, tf), lambda i, j: (0, j)),
                  pl.BlockSpec((D, tf), lambda i, j: (0, j)),
                  pl.BlockSpec((tf, D), lambda i, j: (j, 0))],
        out_specs=pl.BlockSpec((tm, D), lambda i, j: (i, 0)),
        out_shape=jax.ShapeDtypeStruct((M, D), F32),
        scratch_shapes=[pltpu.VMEM((tm, D), F32)],
        compiler_params=_cparams(("parallel", "arbitrary")),
        name="ffn",
    )(h2, w1, w3, w2)


def _moe_body(h_ref, lg_ref, w1_ref, w3_ref, w2_ref, o_ref, acc_ref, comb_ref):
    e = pl.program_id(1)
    lane = _iota(comb_ref.shape, 1)

    @pl.when(e == 0)
    def _():
        acc_ref[...] = jnp.zeros_like(acc_ref)
        lg = jnp.where(lane < N_EXPERTS, lg_ref[...], -jnp.inf)
        v1 = jnp.max(lg, axis=-1, keepdims=True)
        i1 = jnp.min(jnp.where(lg == v1, lane, 128), axis=-1, keepdims=True)
        lg2 = jnp.where(lane == i1, -jnp.inf, lg)
        v2 = jnp.max(lg2, axis=-1, keepdims=True)
        i2 = jnp.min(jnp.where(lg2 == v2, lane, 128), axis=-1, keepdims=True)
        ex = jnp.exp(v2 - v1)
        comb_ref[...] = jnp.where(lane == i1, 1.0 / (1.0 + ex), 0.0) + jnp.where(lane == i2, ex / (1.0 + ex), 0.0)

    ce = jnp.sum(jnp.where(lane == e, comb_ref[...], 0.0), axis=-1, keepdims=True)
    h = h_ref[...]
    a = _mm(h, w1_ref[0])
    z = _silu(a) * _mm(h, w3_ref[0])
    acc_ref[...] += ce * _mm(z, w2_ref[0])

    @pl.when(e == pl.num_programs(1) - 1)
    def _():
        o_ref[...] = acc_ref[...]


def _moe(h2, logits, w1, w3, w2):
    M = h2.shape[0]
    tm = 1024
    f = w1.shape[2]
    return pl.pallas_call(
        _moe_body,
        grid=(M // tm, N_EXPERTS),
        in_specs=[pl.BlockSpec((tm, D), lambda i, e: (i, 0)),
                  pl.BlockSpec((tm, 128), lambda i, e: (i, 0)),
                  pl.BlockSpec((1, D, f), lambda i, e: (e, 0, 0)),
                  pl.BlockSpec((1, D, f), lambda i, e: (e, 0, 0)),
                  pl.BlockSpec((1, f, D), lambda i, e: (e, 0, 0))],
        out_specs=pl.BlockSpec((tm, D), lambda i, e: (i, 0)),
        out_shape=jax.ShapeDtypeStruct((M, D), F32),
        scratch_shapes=[pltpu.VMEM((tm, D), F32), pltpu.VMEM((tm, 128), F32)],
        compiler_params=_cparams(("parallel", "arbitrary")),
        name="moe",
    )(h2, logits, w1, w3, w2)


def _final_body(x_ref, y_ref, mod_ref, g_ref, o_ref):
    b = pl.program_id(0)
    x = x_ref[0] + mod_ref[pl.ds(b, 1), 5 * D:6 * D] * y_ref[0]
    ms = jnp.mean(x * x, axis=-1, keepdims=True)
    o_ref[0] = x * lax.rsqrt(ms + NORM_EPS) * g_ref[...]


def _final(x, y, mod, g, lc):
    B, S, _ = x.shape
    T = S - lc
    off = lc // TM
    return pl.pallas_call(
        _final_body,
        grid=(B, T // TM),
        in_specs=[pl.BlockSpec((1, TM, D), lambda b, i: (b, i + off, 0)),
                  pl.BlockSpec((1, TM, D), lambda b, i: (b, i + off, 0)),
                  pl.BlockSpec(mod.shape, lambda b, i: (0, 0)),
                  pl.BlockSpec(g.shape, lambda b, i: (0, 0))],
        out_specs=pl.BlockSpec((1, TM, D), lambda b, i: (b, i, 0)),
        out_shape=jax.ShapeDtypeStruct((B, T, D), F32),
        compiler_params=_cparams(("parallel", "parallel")),
        name="final_norm",
    )(x, y, mod, g)


def _rot_perm(width, group):
    j = np.arange(width)
    half = group // 2
    return np.where(j % group < half, j + half, j - half)


def _layer_weights(l, w_in, rw_mu, rw_w0, rw_w2, rw_a0, rw_a2, rw_g2, rw_k_k, rw_k_a, rw_r_k, rw_ln_g, rw_ln_b,
                   mla_qnorm_g, mla_wuq, mla_kvnorm_g, mla_wukv):
    wi = w_in[l]
    zc = lambda n: jnp.zeros((D, n), F32)
    o = 0
    w_hg = wi[:, o:o + 5 * GW]
    o += 5 * GW
    rwc = wi[:, o:o + 1088]
    o += 1088
    swc = wi[:, o:o + 512]
    o += 512
    mlc = wi[:, o:o + 352]

    def pad_rw(a, axis):
        z = lambda n: jnp.zeros(a.shape[:axis] + (n,) + a.shape[axis + 1:], a.dtype)
        sl = lambda s, e: lax.slice_in_dim(a, s, e, axis=axis)
        return jnp.concatenate([sl(0, 768), sl(768, 832), z(64), sl(832, 896), z(64), sl(896, 960), z(64),
                                sl(960, 1088)], axis=axis)

    w_rw = pad_rw(rwc, 1)
    mu = pad_rw(rw_mu[l], 1)
    pad_rows = lambda a: jnp.concatenate([a, jnp.zeros((64, a.shape[1]), a.dtype)], axis=0)
    w2 = jnp.stack([pad_rows(rw_w2[l, 0]), pad_rows(rw_w2[l, 1])]).astype(BF16)
    a2 = pad_rows(rw_a2[l]).astype(BF16)

    q = swc[:, 0:GW]
    kx = jnp.concatenate([swc[:, 256:320], swc[:, 256:320], swc[:, 320:384], swc[:, 320:384]], axis=1)
    vx = jnp.concatenate([swc[:, 384:448], swc[:, 384:448], swc[:, 448:512], swc[:, 448:512]], axis=1)
    perm = _rot_perm(GW, 32)
    w_swa = jnp.concatenate([q, kx, vx, q[:, perm], kx[:, perm]], axis=1)

    kr = mlc[:, 320:352]
    krp = kr[:, _rot_perm(MLA_ROPE, 16)]
    w_mla = jnp.concatenate([mlc[:, 0:192], zc(64), mlc[:, 192:320], zc(64), kr, zc(32), zc(64), krp, zc(32)], axis=1)

    wuq = mla_wuq[l]
    wukv = mla_wukv[l]
    rp = _rot_perm(MLA_ROPE, 16)
    wq, wqr, wk, wv = [], [], [], []
    zq = lambda r, n: jnp.zeros((r, n), F32)
    for h in range(NH):
        qh = wuq[:, 96 * h:96 * (h + 1)]
        a = jnp.concatenate([qh, zq(192, 32)], axis=1)
        ar = jnp.concatenate([zq(192, 64), qh[:, 64:96][:, rp], zq(192, 32)], axis=1)
        wq.append(jnp.concatenate([a, zq(64, 128)], axis=0))
        wqr.append(jnp.concatenate([ar, zq(64, 128)], axis=0))
        wk.append(jnp.concatenate([wukv[:, 128 * h:128 * h + 64], zq(128, 64)], axis=1))
        wv.append(wukv[:, 128 * h + 64:128 * (h + 1)].T)
    row = lambda a: a.reshape(1, -1)
    return {
        "proj": jnp.concatenate([w_hg, w_rw, w_swa, w_mla], axis=1).astype(BF16),
        "rw": {"mu": mu, "w0": rw_w0[l], "w2": w2, "a0": row(rw_a0[l]), "a2": a2, "g2": rw_g2[l].astype(BF16),
               "k_k": row(rw_k_k[l]), "k_a": row(rw_k_a[l])},
        "rk": row(rw_r_k[l]), "lng": row(rw_ln_g[l]), "lnb": row(rw_ln_b[l]),
        "mla": {"qg": jnp.concatenate([row(mla_qnorm_g[l]), jnp.zeros((1, 64), F32)], axis=1),
                "kg": row(mla_kvnorm_g[l]),
                "wq": jnp.stack(wq).astype(BF16), "wqr": jnp.stack(wqr).astype(BF16),
                "wk": jnp.stack(wk).astype(BF16), "wv": jnp.stack(wv).astype(BF16)},
    }


def _rope_tables(T, lc):
    rows = T // GRID_W
    rowp = jnp.repeat(jnp.arange(rows, dtype=F32), GRID_W)
    colp = jnp.tile(jnp.arange(GRID_W, dtype=F32), rows)

    def table(rot_dim, width, lane0):
        nf = rot_dim // 4
        inv = ROPE_BASE ** (-jnp.arange(nf, dtype=F32) / nf)
        j = np.arange(rot_dim)
        ax = j // (2 * nf)
        fr = j % nf
        sign = np.where(j % (2 * nf) < nf, -1.0, 1.0).astype(np.float32)
        pos = jnp.stack([rowp, colp], axis=1)
        ang = pos[:, ax] * inv[fr][None, :]
        cos = jnp.concatenate([jnp.ones((lc, rot_dim), F32), jnp.cos(ang)], axis=0)
        sin = jnp.concatenate([jnp.zeros((lc, rot_dim), F32), jnp.sin(ang) * sign[None, :]], axis=0)
        reps = (width - lane0) // rot_dim if lane0 == 0 else 1
        cos = jnp.tile(cos, (1, reps))
        sin = jnp.tile(sin, (1, reps))
        return cos, sin

    cs_s, sn_s = table(HD, 2 * GW, 0)
    c32, s32 = table(MLA_ROPE, MLA_ROPE, 0)
    S = lc + T
    z = lambda n: jnp.zeros((S, n), F32)
    o = lambda n: jnp.ones((S, n), F32)
    cs_m = jnp.concatenate([z(64), c32, z(32)], axis=1)
    sn_m = jnp.concatenate([z(64), s32, z(32)], axis=1)
    cs_q = jnp.concatenate([o(64), c32, z(32)], axis=1)
    sn_q = jnp.concatenate([z(64), s32, z(32)], axis=1)
    return {"cs_s": cs_s, "sn_s": sn_s, "cs_m": cs_m, "sn_m": sn_m, "cs_q": cs_q, "sn_q": sn_q}


def kernel(x, c, ctx, c_ctx, w_ada, b_ada, norm1_g, norm2_g, w_in, hg_lb, hg_norm_g, rw_mu, rw_w0, rw_w2, rw_a0,
           rw_a2, rw_g2, rw_k_k, rw_k_a, rw_r_k, rw_ln_g, rw_ln_b, swa_sink, mla_qnorm_g, mla_wuq, mla_kvnorm_g,
           mla_wukv, w_out, ffn_w1, ffn_w3, ffn_w2, moe_router, moe_w1, moe_w3, moe_w2, final_norm_g):
    B, T, _ = x.shape
    lc = ctx.shape[1]
    S = lc + T
    L = w_in.shape[0]
    assert lc % TM == 0 and T % TM == 0 and lc % SWA_W == 0 and T % GRID_W == 0 and (B * S) % 1024 == 0 and (B * T) % 1024 == 0

    r8 = -(-(B + 1) // 8) * 8
    c8 = jnp.concatenate([c, c_ctx[None, :], jnp.zeros((r8 - B - 1, D), F32)], axis=0)
    mod_all = _ada(c8, w_ada, b_ada)
    tabs = _rope_tables(T, lc)
    xs = jnp.concatenate([ctx, x], axis=1)
    row = lambda a: a.reshape(1, -1)

    pend = None
    for l in range(L):
        w = _layer_weights(l, w_in, rw_mu, rw_w0, rw_w2, rw_a0, rw_a2, rw_g2, rw_k_k, rw_k_a, rw_r_k, rw_ln_g,
                           rw_ln_b, mla_qnorm_g, mla_wuq, mla_kvnorm_g, mla_wukv)
        mod = mod_all[l]
        xs, p_hg, p_rw, p_swa, p_mla = _proj(xs, pend, mod, row(norm1_g[l]), w["proj"], tabs, lc)

        hgo = _hgrn2(p_hg, hg_lb, l, lc)
        rwp = _rwprep(p_rw, w["rw"], lc)
        rwo = _rwscan(rwp, lc)
        sink8 = jnp.zeros((8, 128), F32).at[0:NH, :].set(jnp.broadcast_to(swa_sink[l][:, None], (NH, 128)))
        last = l == L - 1
        swo = _swa(p_swa, sink8, lc, last)
        mq, mk, mv = _mlaprep(p_mla, w["mla"], tabs)
        mlo = _mla(mq, mk, mv, lc, last)

        moe = l % 2 == 1
        router = None
        if moe:
            router = jnp.concatenate([moe_router[l // 2], jnp.zeros((D, 128 - N_EXPERTS), F32)], axis=1)
        outs = _outproj(xs, mod, row(norm2_g[l]), hgo, p_hg, row(hg_norm_g[l]), rwo, rwp, w["rk"], w["lng"],
                        w["lnb"], swo, mlo, w_out[l].astype(BF16), router, lc, last)
        xs, h2 = outs[0], outs[1]
        so = xs.shape[1]
        h2f = h2.reshape(B * so, D)
        if moe:
            e = l // 2
            y = _moe(h2f, outs[2].reshape(B * so, 128), moe_w1[e].astype(BF16), moe_w3[e].astype(BF16),
                     moe_w2[e].astype(BF16))
        else:
            e = l // 2
            y = _ffn(h2f, ffn_w1[e].astype(BF16), ffn_w3[e].astype(BF16), ffn_w2[e].astype(BF16))
        pend = (y.reshape(B, so, D), mod)

    return _final(xs, pend[0], pend[1], row(final_norm_g), xs.shape[1] - T)
```

```python
import functools

import jax
import jax.numpy as jnp
import numpy as np
from jax import lax
from jax.experimental import pallas as pl
from jax.experimental.pallas import tpu as pltpu

F32 = jnp.float32
BF16 = jnp.bfloat16

D = 1024
GW = 256
HD = 64
NH = 4
GRID_W = 64
NORM_EPS = 1e-6
ROPE_BASE = 10000.0
HG_F_FLOOR = 1e-30
RW_LN_EPS = 64e-5
SWA_W = 128
MLA_Q_RANK = 192
MLA_KV_RANK = 128
MLA_NOPE = 64
MLA_ROPE = 32
D_FF = 2816
N_EXPERTS = 8
D_FF_EXPERT = 1408

TM = 256
CH = 64
SCAN_ROWS = 256
HB = 16
RW_BPS = 2
MLA_HPS = 4
MLA_VROWS = 80
LOG2E = 1.4426950408889634
MLA_KCHUNK = 256
MLA_SKEW = 8
VMEM_LIMIT = 56 * 1024 * 1024
RW_GRAM_MODE = "bf16"
RW_BASE_MODE = "bf16"
RW_MERGE_MODE = "bf16"
RW_STATE_MODE = "bf16"

_NN = (((1,), (0,)), ((), ()))
_NT = (((1,), (1,)), ((), ()))
_TN = (((0,), (0,)), ((), ()))


def _mm(a, b, dims=_NN, mode="bf16"):
    if mode == "bf16":
        return lax.dot_general(a.astype(BF16), b.astype(BF16), dims, preferred_element_type=F32)
    if mode == "x3":
        ah = a.astype(BF16)
        al = (a - ah.astype(F32)).astype(BF16)
        bh = b.astype(BF16)
        bl = (b - bh.astype(F32)).astype(BF16)
        d = lambda x, y: lax.dot_general(x, y, dims, preferred_element_type=F32)
        return d(ah, bh) + (d(ah, bl) + d(al, bh))
    return lax.dot_general(a, b, dims, precision=lax.Precision.HIGHEST, preferred_element_type=F32)


def _sigmoid(x):
    return 1.0 / (1.0 + jnp.exp(-x))


def _silu(x):
    return x / (1.0 + jnp.exp(-x))


def _iota(shape, dim):
    return lax.broadcasted_iota(jnp.int32, shape, dim)


def _head_block_mask(n):
    return (_iota((n, n), 0) // HD) == (_iota((n, n), 1) // HD)


def _running_sum(x, fwd, seg):
    n = x.shape[0]
    pos = _iota(x.shape, 0) % seg
    sh = 1
    while sh < seg:
        if fwd:
            x = x + jnp.where(pos >= sh, pltpu.roll(x, sh, 0), 0.0)
        else:
            x = x + jnp.where(pos < seg - sh, pltpu.roll(x, n - sh, 0), 0.0)
        sh *= 2
    return x


def _alternate(*stages, skew=0):
    live = list(enumerate(stages))
    rnd = 0
    while live:
        for item in list(live):
            i, g = item
            if rnd < i * skew:
                continue
            try:
                next(g)
            except StopIteration:
                live.remove(item)
        rnd += 1


def _cparams(sem):
    return pltpu.CompilerParams(dimension_semantics=sem, vmem_limit_bytes=VMEM_LIMIT)


def _ada_body(c_ref, w_ref, b_ref, o_ref):
    s = _silu(c_ref[...])
    o_ref[0] = _mm(s, w_ref[0]) + b_ref[0]


def _ada(c8, w_ada, b_ada):
    L = w_ada.shape[0]
    r8 = c8.shape[0]
    tn = 1536
    return pl.pallas_call(
        _ada_body,
        grid=(L, 6 * D // tn),
        in_specs=[
            pl.BlockSpec((r8, D), lambda l, j: (0, 0)),
            pl.BlockSpec((1, D, tn), lambda l, j: (l, 0, j)),
            pl.BlockSpec((1, 1, tn), lambda l, j: (l, 0, j)),
        ],
        out_specs=pl.BlockSpec((1, r8, tn), lambda l, j: (l, 0, j)),
        out_shape=jax.ShapeDtypeStruct((L, r8, 6 * D), F32),
        compiler_params=_cparams(("parallel", "parallel")),
        name="adaln",
    )(c8, w_ada, b_ada.reshape(L, 1, 6 * D))


def _proj_body(*refs, pending, lc_tiles, nb):
    if pending:
        (x_ref, y_ref, modp_ref, mod_ref, g_ref, wall, cs_s, sn_s, cs_m, sn_m,
         xo_ref, phg, prw, pswa, pmla) = refs
    else:
        (x_ref, mod_ref, g_ref, wall, cs_s, sn_s, cs_m, sn_m,
         phg, prw, pswa, pmla) = refs
    b = pl.program_id(0)
    i = pl.program_id(1)
    row = jnp.where(i < lc_tiles, nb, b)
    x = x_ref[0]
    if pending:
        gate = modp_ref[pl.ds(row, 1), 5 * D:6 * D]
        x = x + gate * y_ref[0]
        xo_ref[0] = x
    sh = mod_ref[pl.ds(row, 1), 0:D]
    sc = mod_ref[pl.ds(row, 1), D:2 * D]
    ms = jnp.mean(x * x, axis=-1, keepdims=True)
    h = (x * lax.rsqrt(ms + NORM_EPS) * g_ref[...]) * (1.0 + sc) + sh
    hb = h.astype(BF16)
    n1 = 5 * GW
    phg[0] = _mm(hb, wall[:, 0:n1])
    prw[0] = _mm(hb, wall[:, n1:2 * n1])
    s = _mm(hb, wall[:, 2 * n1:3 * n1])
    qk = s[:, 0:2 * GW] * cs_s[...] + s[:, 3 * GW:5 * GW] * sn_s[...]
    pswa[0, :, 0:2 * GW] = qk.astype(BF16)
    pswa[0, :, 2 * GW:3 * GW] = s[:, 2 * GW:3 * GW].astype(BF16)
    m = _mm(hb, wall[:, 3 * n1:3 * n1 + 640])
    pmla[0, :, 0:384] = m[:, 0:384]
    pmla[0, :, 384:512] = m[:, 384:512] * cs_m[...] + m[:, 512:640] * sn_m[...]


def _proj(x, pend, mod, g, w, tabs, lc):
    B, S, _ = x.shape
    nt = S // TM
    pending = pend is not None
    row_spec = lambda n: pl.BlockSpec((1, TM, n), lambda b, i: (b, i, 0))
    full = lambda a: pl.BlockSpec(a.shape, lambda b, i: (0,) * a.ndim)
    tab_spec = lambda a: pl.BlockSpec((TM, a.shape[1]), lambda b, i: (i, 0))
    ins = [x]
    in_specs = [row_spec(D)]
    if pending:
        ins += [pend[0], pend[1]]
        in_specs += [row_spec(D), full(pend[1])]
    ins += [mod, g, w, tabs["cs_s"], tabs["sn_s"], tabs["cs_m"], tabs["sn_m"]]
    in_specs += [full(mod), full(g), full(w),
                 tab_spec(tabs["cs_s"]), tab_spec(tabs["sn_s"]), tab_spec(tabs["cs_m"]), tab_spec(tabs["sn_m"])]
    out_shape = [jax.ShapeDtypeStruct((B, S, 5 * GW), F32), jax.ShapeDtypeStruct((B, S, 5 * GW), F32),
                 jax.ShapeDtypeStruct((B, S, 3 * GW), BF16), jax.ShapeDtypeStruct((B, S, 512), F32)]
    out_specs = [row_spec(5 * GW), row_spec(5 * GW), row_spec(3 * GW), row_spec(512)]
    if pending:
        out_shape = [jax.ShapeDtypeStruct((B, S, D), F32)] + out_shape
        out_specs = [row_spec(D)] + out_specs
    outs = pl.pallas_call(
        functools.partial(_proj_body, pending=pending, lc_tiles=lc // TM, nb=B),
        grid=(B, nt), in_specs=in_specs, out_specs=out_specs, out_shape=out_shape,
        compiler_params=_cparams(("parallel", "parallel")),
        name="proj",
    )(*ins)
    if pending:
        return tuple(outs)
    return (x,) + tuple(outs)


def _chunk_of_step(d, s, nc, nc_ctx):
    bwd = jnp.where(s < nc_ctx, nc_ctx - 1 - s, nc - 1 - (s - nc_ctx))
    return jnp.where(d == 0, s, bwd)


def _hg_direction(p_ref, lb, o_ref, st_ref, fwd, c0):
    p = p_ref[0, c0:c0 + CH, :]
    z = p[:, GW:2 * GW] if fwd else p[:, 2 * GW:3 * GW]
    f = lb + (1.0 - lb) * _sigmoid(z)
    logf = jnp.log(jnp.maximum(f, HG_F_FLOOR))
    key = (1.0 - lb) * _sigmoid(-z)
    qs = _silu(p[:, 0:GW])
    iv = p[:, 3 * GW:4 * GW]

    bcum = _running_sum(logf, fwd, HB)

    bd = _head_block_mask(GW)
    bdf = bd.astype(BF16)
    half = HB // 2
    nblk = CH // HB
    order = [(j if fwd else nblk - 1 - j) * HB for j in range(nblk)]
    blocks = [tuple(a[r0:r0 + HB] for a in (qs, key, iv, bcum)) for r0 in order]
    es = []
    spans = []
    for qb, kb, ib, bb in blocks:
        for sidx in range(HB):
            if fwd:
                lo, hi = (0, HB) if sidx < half else (half, HB)
            else:
                lo, hi = (0, half) if sidx < half else (0, HB)
            rows = _iota((hi - lo, 1), 0) + lo
            m = (rows >= sidx) if fwd else (rows <= sidx)
            dec = jnp.exp(jnp.where(m, bb[lo:hi] - bb[sidx:sidx + 1], -jnp.inf))
            es.append(qb[lo:hi] * kb[sidx:sidx + 1] * dec)
            spans.append((lo, hi))
        yield
    att = _mm(jnp.concatenate(es, axis=0), bdf)
    yield
    inblock = []
    outers = []
    off = 0
    for j, (qb, kb, ib, bb) in enumerate(blocks):
        top = jnp.zeros((half, GW), F32)
        bot = jnp.zeros((half, GW), F32)
        for sidx in range(HB):
            lo, hi = spans[j * HB + sidx]
            a = att[off:off + hi - lo] * ib[sidx:sidx + 1]
            off += hi - lo
            if hi - lo == HB:
                top = top + a[0:half]
                bot = bot + a[half:HB]
            elif lo == 0:
                top = top + a
            else:
                bot = bot + a
        inblock.append((top, bot))
        tot = bb[HB - 1:HB] if fwd else bb[0:1]
        outers.append((_mm(ib, kb * jnp.exp(tot - bb), _TN), jnp.exp(tot)))
        yield
    for r0, (qb, kb, ib, bb), (top, bot), (outer, dtot) in zip(order, blocks, inblock, outers):
        st = st_ref[...]
        o_blk = _mm(qb * jnp.exp(bb), st, _NT)
        o_ref[0, c0 + r0:c0 + r0 + half, :] = o_blk[0:half] + top
        o_ref[0, c0 + r0 + half:c0 + r0 + HB, :] = o_blk[half:HB] + bot
        st_ref[...] = jnp.where(bd, st * dtot + outer, 0.0)
        yield


def _hg_body(pf_ref, pb_ref, lbraw_ref, of_ref, ob_ref, stf_ref, stb_ref, *, layer):
    @pl.when(pl.program_id(1) == 0)
    def _():
        stf_ref[...] = jnp.zeros_like(stf_ref)
        stb_ref[...] = jnp.zeros_like(stb_ref)

    raw = lbraw_ref[...]
    nl = raw.shape[0]
    mx = raw[0]
    for j in range(1, nl):
        mx = jnp.maximum(mx, raw[j])
    ex = [jnp.exp(raw[j] - mx) for j in range(nl)]
    den = ex[0]
    for j in range(1, nl):
        den = den + ex[j]
    lb2 = jnp.zeros_like(mx)
    for j in range(1, layer + 1):
        lb2 = lb2 + ex[j] / den
    def scan(p_ref, lb, o_ref, st_ref, fwd):
        chunks = range(SCAN_ROWS // CH)
        for c in (chunks if fwd else reversed(chunks)):
            yield from _hg_direction(p_ref, lb, o_ref, st_ref, fwd, c * CH)

    _alternate(scan(pf_ref, lb2[0:1], of_ref, stf_ref, True), scan(pb_ref, lb2[1:2], ob_ref, stb_ref, False))


def _hgrn2(p_hg, hg_lb, layer, lc):
    B, S, _ = p_hg.shape
    rs = SCAN_ROWS
    nc = S // rs
    nc_ctx = lc // rs
    fmap = lambda b, s: (b, s, 0)
    bmap = lambda b, s: (b, _chunk_of_step(1, s, nc, nc_ctx), 0)
    return pl.pallas_call(
        functools.partial(_hg_body, layer=layer),
        grid=(B, nc),
        in_specs=[pl.BlockSpec((1, rs, 5 * GW), fmap), pl.BlockSpec((1, rs, 5 * GW), bmap),
                  pl.BlockSpec(hg_lb.shape, lambda b, s: (0, 0, 0))],
        out_specs=[pl.BlockSpec((1, rs, GW), fmap), pl.BlockSpec((1, rs, GW), bmap)],
        out_shape=[jax.ShapeDtypeStruct((B, S, GW), F32)] * 2,
        scratch_shapes=[pltpu.VMEM((GW, GW), F32)] * 2,
        compiler_params=_cparams(("parallel", "arbitrary")),
        name="hgrn2",
    )(p_hg, p_hg, hg_lb)


def _rwprep_body(cur_ref, prv_ref, nxt_ref, mu_ref, w0_ref, w2_ref, a0_ref, a2_ref, g2_ref, kk_ref, ka_ref,
                 o_ref, buf, *, lc_tiles):
    i = pl.program_id(1)
    nt = pl.num_programs(1)
    first = (i == 0) | (i == lc_tiles)
    lastt = (i == lc_tiles - 1) | (i == nt - 1)
    buf[0:8, :] = jnp.where(first, 0.0, prv_ref[0])
    buf[8:8 + TM, :] = cur_ref[0]
    buf[8 + TM:16 + TM, :] = jnp.where(lastt, 0.0, nxt_ref[0])
    cur = buf[8:8 + TM, :]
    prev = buf[7:7 + TM, :]
    nxt = buf[9:9 + TM, :]
    sft = cur + mu_ref[0:1, :] * (prev - cur) + mu_ref[1:2, :] * (nxt - cur)
    r = sft[:, 0:GW]
    k = sft[:, GW:2 * GW]
    v = sft[:, 2 * GW:3 * GW]
    ywf = sft[:, 768:896]
    ywb = sft[:, 896:1024]
    ya = sft[:, 1024:1152]
    yg = sft[:, 1152:1280]
    c = float(np.exp(-0.5))
    lwf = -c * _sigmoid(w0_ref[0:1, :] + _mm(jnp.tanh(ywf), w2_ref[0]))
    lwb = -c * _sigmoid(w0_ref[1:2, :] + _mm(jnp.tanh(ywb), w2_ref[1]))
    a = _sigmoid(a0_ref[...] + _mm(ya, a2_ref[...]))
    kk0 = k * kk_ref[...]
    ss = _mm(kk0 * kk0, _head_block_mask(GW).astype(F32), mode="x3")
    kk = kk0 / jnp.maximum(jnp.sqrt(ss), 1e-12)
    kp = k * (1.0 + (a - 1.0) * ka_ref[...])
    g = _mm(_sigmoid(yg), g2_ref[...])
    o_ref[0, :, 0:GW] = r
    o_ref[0, :, GW:2 * GW] = lwf
    o_ref[0, :, 2 * GW:3 * GW] = lwb
    o_ref[0, :, 3 * GW:4 * GW] = kp
    o_ref[0, :, 4 * GW:5 * GW] = v
    o_ref[0, :, 5 * GW:6 * GW] = kk
    o_ref[0, :, 6 * GW:7 * GW] = kk * a
    o_ref[0, :, 7 * GW:8 * GW] = g


def _rwprep(p_rw, w, lc):
    B, S, n = p_rw.shape
    nt = S // TM
    t8 = TM // 8
    full = lambda a: pl.BlockSpec(a.shape, lambda b, i: (0,) * a.ndim)
    small = [w["mu"], w["w0"], w["w2"], w["a0"], w["a2"], w["g2"], w["k_k"], w["k_a"]]
    return pl.pallas_call(
        functools.partial(_rwprep_body, lc_tiles=lc // TM),
        grid=(B, nt),
        in_specs=[pl.BlockSpec((1, TM, n), lambda b, i: (b, i, 0)),
                  pl.BlockSpec((1, 8, n), lambda b, i: (b, jnp.maximum(i * t8 - 1, 0), 0)),
                  pl.BlockSpec((1, 8, n), lambda b, i: (b, jnp.minimum((i + 1) * t8, S // 8 - 1), 0))]
                 + [full(a) for a in small],
        out_specs=pl.BlockSpec((1, TM, 8 * GW), lambda b, i: (b, i, 0)),
        out_shape=jax.ShapeDtypeStruct((B, S, 8 * GW), F32),
        scratch_shapes=[pltpu.VMEM((TM + 16, n), F32)],
        compiler_params=_cparams(("parallel", "parallel")),
        name="rwkv_prep",
    )(p_rw, p_rw, p_rw, *small)


def _rw_direction(x_ref, o_ref, h_ref, fwd, bi, c0):
    x = x_ref[bi, c0:c0 + CH, :]
    r = x[:, 0:GW]
    lw = x[:, GW:2 * GW] if fwd else x[:, 2 * GW:3 * GW]
    k = x[:, 3 * GW:4 * GW]
    v = x[:, 4 * GW:5 * GW]
    kk = x[:, 5 * GW:6 * GW]
    bb = x[:, 6 * GW:7 * GW]

    incl = _running_sum(lw, fwd, CH)
    tot = incl[CH - 1:CH] if fwd else incl[0:1]
    e_in = jnp.exp(incl)
    e_ni = jnp.exp(-incl)
    e_tt = jnp.exp(tot - incl)
    km = kk * jnp.exp(incl - lw)
    rt = r * e_in
    bh = bb * e_ni
    kh = k * e_ni
    bt = bb * e_tt
    kt = k * e_tt

    n = NH * CH
    hm = (_iota((n, GW), 0) // CH) == (_iota((n, GW), 1) // HD)
    tile4 = lambda a: jnp.concatenate([a] * NH, axis=0)
    expand = lambda a: jnp.where(hm, tile4(a), 0.0)
    km_e = expand(km)
    rt_e = expand(rt)
    v_e = expand(v)
    bt_e = expand(bt)
    kt_e = expand(kt)
    bh_t = tile4(bh)
    kh_t = tile4(kh)

    rr = _iota((n, n), 0)
    cc = _iota((n, n), 1)
    same = (rr // CH) == (cc // CH)
    rt_i = rr % CH
    ct_i = cc % CH
    before = same & ((ct_i < rt_i) if fwd else (ct_i > rt_i))
    upto = same & ((ct_i <= rt_i) if fwd else (ct_i >= rt_i))
    eye = rr == cc

    pg = RW_GRAM_MODE
    a_m = jnp.where(before, _mm(km_e, bh_t, _NT, pg), 0.0)
    yield

    blk = lambda m: (rr // m) == (cc // m)
    pb = RW_BASE_MODE
    nm = jnp.where(blk(8), -a_m, 0.0)
    x2 = _mm(nm, nm, mode=pb)
    b_m = jnp.where(before, _mm(km_e, kh_t, _NT, pg), 0.0)
    yield
    tm = jnp.where(eye, 1.0, 0.0) + nm
    tm = tm + _mm(tm, x2, mode=pb)
    x4 = _mm(x2, x2, mode=pb)
    yield
    tm = tm + _mm(tm, x4, mode=pb)
    mrb = jnp.where(upto, _mm(rt_e, bh_t, _NT, pg), 0.0)
    yield
    pm = RW_MERGE_MODE
    side = [lambda: jnp.where(upto, _mm(rt_e, kh_t, _NT, pg), 0.0), lambda: _mm(b_m, v_e, mode=pg),
            lambda: _mm(kt_e, v_e, _TN, pg)]
    done = []
    for m in (8, 16, 32):
        low = jnp.where(blk(2 * m) & (rr // m != cc // m), a_m, 0.0)
        y = _mm(low, tm, mode=pm)
        done.append(side.pop(0)())
        yield
        tm = tm - _mm(tm, y, mode=pm)
        yield
    mrk, bmv, ktv = done

    p1 = _mm(tm, km_e, mode=pg)
    q1 = _mm(tm, bmv, mode=pg)
    yield
    p2e = rt_e - _mm(mrb, p1, mode=pg)
    q2e = _mm(mrk, v_e, mode=pg) - _mm(mrb, q1, mode=pg)
    g_m = jnp.where(eye, jnp.exp(tot), 0.0) - _mm(bt_e, p1, _TN, pg)
    f_m = ktv - _mm(bt_e, q1, _TN, pg)
    yield
    fold = lambda a: a[0:CH] + a[CH:2 * CH] + a[2 * CH:3 * CH] + a[3 * CH:4 * CH]
    p2 = fold(p2e)
    q2 = fold(q2e)
    h = h_ref[bi]
    ps = RW_STATE_MODE
    o_ref[bi, c0:c0 + CH, :] = _mm(p2, h, mode=ps) + q2
    h_ref[bi] = _mm(g_m, h, mode=ps) + f_m


def _rw_body(xf_ref, xb_ref, of_ref, ob_ref, hf_ref, hb_ref):
    @pl.when(pl.program_id(1) == 0)
    def _():
        hf_ref[...] = jnp.zeros_like(hf_ref)
        hb_ref[...] = jnp.zeros_like(hb_ref)

    def scan(x_ref, o_ref, h_ref, fwd, bi):
        chunks = range(SCAN_ROWS // CH)
        for c in (chunks if fwd else reversed(chunks)):
            yield from _rw_direction(x_ref, o_ref, h_ref, fwd, bi, c * CH)

    chains = []
    for bi in range(xf_ref.shape[0]):
        chains.append(scan(xf_ref, of_ref, hf_ref, True, bi))
        chains.append(scan(xb_ref, ob_ref, hb_ref, False, bi))
    _alternate(*chains)


def _rwscan(prep, lc):
    B, S, n = prep.shape
    rs = SCAN_ROWS
    nc = S // rs
    nc_ctx = lc // rs
    bps = RW_BPS if B % RW_BPS == 0 else 1
    fmap = lambda b, s: (b, s, 0)
    bmap = lambda b, s: (b, _chunk_of_step(1, s, nc, nc_ctx), 0)
    return pl.pallas_call(
        _rw_body,
        grid=(B // bps, nc),
        in_specs=[pl.BlockSpec((bps, rs, n), fmap), pl.BlockSpec((bps, rs, n), bmap)],
        out_specs=[pl.BlockSpec((bps, rs, GW), fmap), pl.BlockSpec((bps, rs, GW), bmap)],
        out_shape=[jax.ShapeDtypeStruct((B, S, GW), F32)] * 2,
        scratch_shapes=[pltpu.VMEM((bps, GW, GW), F32)] * 2,
        compiler_params=_cparams(("parallel", "arbitrary")),
        name="rwkv_scan",
    )(prep, prep)


def _swa_block(q, kloc, vloc, kc, vc, sink_ref, o_ref, r0, j, lc_blocks, nbl):
    w = SWA_W
    is_lat = j >= lc_blocks
    nblk = j - lc_blocks
    lane_h = _iota((w, GW), 1) // HD
    qs = jnp.concatenate([jnp.where(lane_h == h, q, jnp.zeros_like(q)) for h in range(NH)], axis=0)
    scale = HD ** -0.5
    s_loc = _mm(qs, kloc, _NT) * scale
    s_ctx = _mm(qs, kc, _NT) * scale
    yield
    qi = _iota((NH * w, 3 * w), 0) % w
    col = _iota((NH * w, 3 * w), 1)
    seg = col // w
    kj = col % w
    valid = ((seg == 0) & (nblk >= 1) & (kj >= qi)) | (seg == 1) | ((seg == 2) & (nblk < nbl - 1) & (kj <= qi))
    valid = valid & is_lat
    s_loc = jnp.where(valid, s_loc, -jnp.inf)
    hrow = _iota((NH * w, 1), 0) // w
    sink = jnp.zeros((NH * w, 1), F32)
    for h in range(NH):
        sink = jnp.where(hrow == h, sink_ref[h:h + 1, 0:1], sink)
    m = jnp.maximum(jnp.maximum(jnp.max(s_loc, axis=-1, keepdims=True), jnp.max(s_ctx, axis=-1, keepdims=True)), sink)
    p_loc = jnp.exp(s_loc - m)
    p_ctx = jnp.exp(s_ctx - m)
    den = jnp.sum(p_loc, axis=-1, keepdims=True) + jnp.sum(p_ctx, axis=-1, keepdims=True) + jnp.exp(sink - m)
    yield
    of = (_mm(p_loc, vloc) + _mm(p_ctx, vc)) / den
    out = jnp.zeros((w, GW), F32)
    for h in range(NH):
        out = out + jnp.where(lane_h == h, of[h * w:(h + 1) * w], 0.0)
    o_ref[0, r0:r0 + w, :] = out
    yield


def _swa_body(pair_ref, prv_ref, nxt_ref, ctx_ref, sink_ref, o_ref, *, lc_blocks, nbl, i0):
    i = pl.program_id(1) + i0
    w = SWA_W
    kcol = slice(GW, 2 * GW)
    vcol = slice(2 * GW, 3 * GW)
    ka, kb = pair_ref[0, 0:w, kcol], pair_ref[0, w:2 * w, kcol]
    va, vb = pair_ref[0, 0:w, vcol], pair_ref[0, w:2 * w, vcol]
    kc = ctx_ref[0, :, kcol]
    vc = ctx_ref[0, :, vcol]
    cat = lambda *a: jnp.concatenate(a, axis=0)
    first = _swa_block(pair_ref[0, 0:w, 0:GW], cat(prv_ref[0, :, kcol], ka, kb), cat(prv_ref[0, :, vcol], va, vb),
                       kc, vc, sink_ref, o_ref, 0, 2 * i, lc_blocks, nbl)
    second = _swa_block(pair_ref[0, w:2 * w, 0:GW], cat(ka, kb, nxt_ref[0, :, kcol]), cat(va, vb, nxt_ref[0, :, vcol]),
                        kc, vc, sink_ref, o_ref, w, 2 * i + 1, lc_blocks, nbl)
    _alternate(first, second, skew=1)


def _swa(p_swa, sink8, lc, skip_ctx):
    B, S, n = p_swa.shape
    w = SWA_W
    nb = S // w
    lcb = lc // w
    assert nb % 2 == 0 and lcb % 2 == 0
    i0 = lcb // 2 if skip_ctx else 0
    return pl.pallas_call(
        functools.partial(_swa_body, lc_blocks=lcb, nbl=nb - lcb, i0=i0),
        grid=(B, nb // 2 - i0),
        in_specs=[pl.BlockSpec((1, 2 * w, n), lambda b, i: (b, i + i0, 0)),
                  pl.BlockSpec((1, w, n), lambda b, i: (b, jnp.maximum(2 * (i + i0) - 1, 0), 0)),
                  pl.BlockSpec((1, w, n), lambda b, i: (b, jnp.minimum(2 * (i + i0) + 2, nb - 1), 0)),
                  pl.BlockSpec((1, lc, n), lambda b, i: (b, 0, 0)),
                  pl.BlockSpec(sink8.shape, lambda b, i: (0, 0))],
        out_specs=pl.BlockSpec((1, 2 * w, GW), lambda b, i: (b, i, 0)),
        out_shape=jax.ShapeDtypeStruct((B, S - 2 * w * i0, GW), F32),
        compiler_params=_cparams(("parallel", "parallel")),
        name="swa",
    )(p_swa, p_swa, p_swa, p_swa, sink8)


def _mlaprep_body(p_ref, qg_ref, kg_ref, wq, wqr, wk, wv, cs_ref, sn_ref, q_ref, k_ref, v_ref):
    p = p_ref[0]
    cq = p[:, 0:256]
    ckv = p[:, 256:384]
    krs = p[:, 384:512]
    qn = cq * lax.rsqrt(jnp.sum(cq * cq, axis=-1, keepdims=True) / MLA_Q_RANK + NORM_EPS) * qg_ref[...]
    kvn = ckv * lax.rsqrt(jnp.mean(ckv * ckv, axis=-1, keepdims=True) + NORM_EPS) * kg_ref[...]
    scale = (MLA_NOPE + MLA_ROPE) ** -0.5 * LOG2E
    cs = cs_ref[...]
    sn = sn_ref[...]
    kvn_t = kvn.T.astype(BF16)
    ones_rows = (_iota((MLA_VROWS - HD, TM), 0) == 0).astype(BF16)
    for h in range(NH):
        qh = _mm(qn, wq[h]) * cs + _mm(qn, wqr[h]) * sn
        q_ref[0, h] = (qh * scale).astype(BF16)
        k_ref[0, h] = (_mm(kvn, wk[h]) + krs).astype(BF16)
        v_ref[0, h, 0:HD, :] = _mm(wv[h], kvn_t).astype(BF16)
        v_ref[0, h, HD:MLA_VROWS, :] = ones_rows


def _mlaprep(p_mla, w, tabs):
    B, S, n = p_mla.shape
    nt = S // TM
    full = lambda a: pl.BlockSpec(a.shape, lambda b, i: (0,) * a.ndim)
    ws = [w["qg"], w["kg"], w["wq"], w["wqr"], w["wk"], w["wv"]]
    hspec = lambda c: pl.BlockSpec((1, NH, TM, c), lambda b, i: (b, 0, i, 0))
    return pl.pallas_call(
        _mlaprep_body,
        grid=(B, nt),
        in_specs=[pl.BlockSpec((1, TM, n), lambda b, i: (b, i, 0))] + [full(a) for a in ws]
                 + [pl.BlockSpec((TM, 128), lambda b, i: (i, 0))] * 2,
        out_specs=[hspec(128), hspec(128), pl.BlockSpec((1, NH, MLA_VROWS, TM), lambda b, i: (b, 0, 0, i))],
        out_shape=[jax.ShapeDtypeStruct((B, NH, S, 128), BF16), jax.ShapeDtypeStruct((B, NH, S, 128), BF16),
                   jax.ShapeDtypeStruct((B, NH, MLA_VROWS, S), BF16)],
        compiler_params=_cparams(("parallel", "parallel")),
        name="mla_prep",
    )(p_mla, *ws, tabs["cs_q"], tabs["sn_q"])


def _mla_body(q_ref, k_ref, v_ref, o_ref, *, lc, t0):
    i = pl.program_id(2) + t0
    ctx_q = i * TM < lc
    kc = MLA_KCHUNK
    nk = k_ref.shape[2] // kc

    def head(h):
        q = q_ref[0, h]
        ss = []
        m = None
        for c in range(nk):
            s = _mm(k_ref[0, h, c * kc:(c + 1) * kc, :], q, _NT)
            if c * kc >= lc and t0 * TM < lc:
                s = jnp.where(ctx_q, -jnp.inf, s)
            ss.append(s)
            cm = jnp.max(s, axis=0, keepdims=True)
            m = cm if m is None else jnp.maximum(m, cm)
            yield
        acc = None
        for c in range(nk):
            p = jnp.exp2(ss[c] - m)
            pv = _mm(v_ref[0, h, :, c * kc:(c + 1) * kc], p)
            acc = pv if acc is None else acc + pv
            yield
        o_ref[0, h] = acc[0:HD] / acc[HD:HD + 1]

    _alternate(*[head(h) for h in range(MLA_HPS)], skew=MLA_SKEW)


def _mla(q, k, v, lc, skip_ctx):
    B, H, S, _ = q.shape
    nt = S // TM
    hp = MLA_HPS
    t0 = lc // TM if skip_ctx else 0
    return pl.pallas_call(
        functools.partial(_mla_body, lc=lc, t0=t0),
        grid=(B, H // hp, nt - t0),
        in_specs=[pl.BlockSpec((1, hp, TM, 128), lambda b, h, i: (b, h, i + t0, 0)),
                  pl.BlockSpec((1, hp, S, 128), lambda b, h, i: (b, h, 0, 0)),
                  pl.BlockSpec((1, hp, MLA_VROWS, S), lambda b, h, i: (b, h, 0, 0))],
        out_specs=pl.BlockSpec((1, hp, HD, TM), lambda b, h, i: (b, h, 0, i)),
        out_shape=jax.ShapeDtypeStruct((B, H, HD, S - t0 * TM), F32),
        compiler_params=_cparams(("parallel", "parallel", "parallel")),
        name="mla_attn",
    )(q, k, v)


def _outproj_body(*refs, lc_tiles, nb, moe, t0):
    if moe:
        (x_ref, mod_ref, n2_ref, hgf_ref, hgb_ref, hgg_ref, hgn_ref, rwf_ref, rwb_ref, rwp_ref, rk_ref, lng_ref, lnb_ref,
         swo_ref, mlo_ref, wout_ref, rt_ref, xo_ref, h2_ref, lg_ref) = refs
    else:
        (x_ref, mod_ref, n2_ref, hgf_ref, hgb_ref, hgg_ref, hgn_ref, rwf_ref, rwb_ref, rwp_ref, rk_ref, lng_ref, lnb_ref,
         swo_ref, mlo_ref, wout_ref, xo_ref, h2_ref) = refs
    b = pl.program_id(0)
    i = pl.program_id(1) + t0
    row = jnp.where(i < lc_tiles, nb, b)
    bd = _head_block_mask(GW).astype(BF16)

    def hsum(a):
        hi = a.astype(BF16)
        lo = (a - hi.astype(F32)).astype(BF16)
        return _mm(hi, bd) + _mm(lo, bd)

    o = hgf_ref[0] + hgb_ref[0]
    hg = o * lax.rsqrt(hsum(o * o) / HD + NORM_EPS) * hgn_ref[...] * _silu(hgg_ref[0])

    o = rwf_ref[0] + rwb_ref[0]
    c = o - hsum(o) / HD
    on = c * lax.rsqrt(hsum(c * c) / HD + RW_LN_EPS) * lng_ref[...] + lnb_ref[...]
    r = rwp_ref[0, :, 0:GW]
    kp = rwp_ref[0, :, 3 * GW:4 * GW]
    v = rwp_ref[0, :, 4 * GW:5 * GW]
    g = rwp_ref[0, :, 7 * GW:8 * GW]
    rw = (on + hsum(r * kp * rk_ref[...]) * v) * g

    acc = _mm(hg, wout_ref[0:GW, :]) + _mm(rw, wout_ref[GW:2 * GW, :]) + _mm(swo_ref[0], wout_ref[2 * GW:3 * GW, :])
    for h in range(NH):
        acc = acc + _mm(mlo_ref[0, h], wout_ref[3 * GW + h * HD:3 * GW + (h + 1) * HD, :], _TN)

    x = x_ref[0] + mod_ref[pl.ds(row, 1), 2 * D:3 * D] * acc
    xo_ref[0] = x
    sh = mod_ref[pl.ds(row, 1), 3 * D:4 * D]
    sc = mod_ref[pl.ds(row, 1), 4 * D:5 * D]
    ms = jnp.mean(x * x, axis=-1, keepdims=True)
    h2 = (x * lax.rsqrt(ms + NORM_EPS) * n2_ref[...]) * (1.0 + sc) + sh
    h2_ref[0] = h2.astype(BF16)
    if moe:
        lg_ref[0] = _mm(h2, rt_ref[...], mode="x3")


def _outproj(x, mod, n2g, hgo, p_hg, hgn, rwo, rwp, rk, lng, lnb, swo, mlo, wout, router, lc, skip_ctx):
    B, S, _ = x.shape
    nt = S // TM
    moe = router is not None
    t0 = lc // TM if skip_ctx else 0
    so = S - t0 * TM
    row = lambda n: pl.BlockSpec((1, TM, n), lambda b, i: (b, i + t0, 0))
    lat = lambda n: pl.BlockSpec((1, TM, n), lambda b, i: (b, i, 0))
    full = lambda a: pl.BlockSpec(a.shape, lambda b, i: (0,) * a.ndim)
    ins = [x, mod, n2g, hgo[0], hgo[1], p_hg, hgn, rwo[0], rwo[1], rwp, rk, lng, lnb, swo, mlo, wout]
    in_specs = [row(D), full(mod), full(n2g), row(GW), row(GW),
                pl.BlockSpec((1, TM, GW), lambda b, i: (b, i + t0, 4)), full(hgn),
                row(GW), row(GW), row(8 * GW), full(rk), full(lng), full(lnb), lat(GW),
                pl.BlockSpec((1, NH, HD, TM), lambda b, i: (b, 0, 0, i)), full(wout)]
    out_shape = [jax.ShapeDtypeStruct((B, so, D), F32), jax.ShapeDtypeStruct((B, so, D), BF16)]
    out_specs = [lat(D), lat(D)]
    if moe:
        ins.append(router)
        in_specs.append(full(router))
        out_shape.append(jax.ShapeDtypeStruct((B, so, 128), F32))
        out_specs.append(lat(128))
    return pl.pallas_call(
        functools.partial(_outproj_body, lc_tiles=lc // TM, nb=B, moe=moe, t0=t0),
        grid=(B, nt - t0), in_specs=in_specs, out_specs=out_specs, out_shape=out_shape,
        compiler_params=_cparams(("parallel", "parallel")),
        name="outproj",
    )(*ins)


def _ffn_body(h_ref, w1_ref, w3_ref, w2_ref, o_ref, acc_ref):
    j = pl.program_id(1)

    @pl.when(j == 0)
    def _():
        acc_ref[...] = jnp.zeros_like(acc_ref)

    h = h_ref[...]
    a = _mm(h, w1_ref[...])
    z = _silu(a) * _mm(h, w3_ref[...])
    acc_ref[...] += _mm(z, w2_ref[...])

    @pl.when(j == pl.num_programs(1) - 1)
    def _():
        o_ref[...] = acc_ref[...]


def _ffn(h2, w1, w3, w2):
    M = h2.shape[0]
    tm = 1024
    tf = D_FF // 2
    return pl.pallas_call(
        _ffn_body,
        grid=(M // tm, D_FF // tf),
        in_specs=[pl.BlockSpec((tm, D), lambda i, j: (i, 0)),
                  pl.BlockSpec((D, tf), lambda i, j: (0, j)),
                  pl.BlockSpec((D, tf), lambda i, j: (0, j)),
                  pl.BlockSpec((tf, D), lambda i, j: (j, 0))],
        out_specs=pl.BlockSpec((tm, D), lambda i, j: (i, 0)),
        out_shape=jax.ShapeDtypeStruct((M, D), F32),
        scratch_shapes=[pltpu.VMEM((tm, D), F32)],
        compiler_params=_cparams(("parallel", "arbitrary")),
        name="ffn",
    )(h2, w1, w3, w2)


def _moe_body(h_ref, lg_ref, w1_ref, w3_ref, w2_ref, o_ref, acc_ref, comb_ref):
    e = pl.program_id(1)
    lane = _iota(comb_ref.shape, 1)

    @pl.when(e == 0)
    def _():
        acc_ref[...] = jnp.zeros_like(acc_ref)
        lg = jnp.where(lane < N_EXPERTS, lg_ref[...], -jnp.inf)
        v1 = jnp.max(lg, axis=-1, keepdims=True)
        i1 = jnp.min(jnp.where(lg == v1, lane, 128), axis=-1, keepdims=True)
        lg2 = jnp.where(lane == i1, -jnp.inf, lg)
        v2 = jnp.max(lg2, axis=-1, keepdims=True)
        i2 = jnp.min(jnp.where(lg2 == v2, lane, 128), axis=-1, keepdims=True)
        ex = jnp.exp(v2 - v1)
        comb_ref[...] = jnp.where(lane == i1, 1.0 / (1.0 + ex), 0.0) + jnp.where(lane == i2, ex / (1.0 + ex), 0.0)

    ce = jnp.sum(jnp.where(lane == e, comb_ref[...], 0.0), axis=-1, keepdims=True)
    h = h_ref[...]
    a = _mm(h, w1_ref[0])
    z = _silu(a) * _mm(h, w3_ref[0])
    acc_ref[...] += ce * _mm(z, w2_ref[0])

    @pl.when(e == pl.num_programs(1) - 1)
    def _():
        o_ref[...] = acc_ref[...]


def _moe(h2, logits, w1, w3, w2):
    M = h2.shape[0]
    tm = 1024
    f = w1.shape[2]
    return pl.pallas_call(
        _moe_body,
        grid=(M // tm, N_EXPERTS),
        in_specs=[pl.BlockSpec((tm, D), lambda i, e: (i, 0)),
                  pl.BlockSpec((tm, 128), lambda i, e: (i, 0)),
                  pl.BlockSpec((1, D, f), lambda i, e: (e, 0, 0)),
                  pl.BlockSpec((1, D, f), lambda i, e: (e, 0, 0)),
                  pl.BlockSpec((1, f, D), lambda i, e: (e, 0, 0))],
        out_specs=pl.BlockSpec((tm, D), lambda i, e: (i, 0)),
        out_shape=jax.ShapeDtypeStruct((M, D), F32),
        scratch_shapes=[pltpu.VMEM((tm, D), F32), pltpu.VMEM((tm, 128), F32)],
        compiler_params=_cparams(("parallel", "arbitrary")),
        name="moe",
    )(h2, logits, w1, w3, w2)


def _final_body(x_ref, y_ref, mod_ref, g_ref, o_ref):
    b = pl.program_id(0)
    x = x_ref[0] + mod_ref[pl.ds(b, 1), 5 * D:6 * D] * y_ref[0]
    ms = jnp.mean(x * x, axis=-1, keepdims=True)
    o_ref[0] = x * lax.rsqrt(ms + NORM_EPS) * g_ref[...]


def _final(x, y, mod, g, lc):
    B, S, _ = x.shape
    T = S - lc
    off = lc // TM
    return pl.pallas_call(
        _final_body,
        grid=(B, T // TM),
        in_specs=[pl.BlockSpec((1, TM, D), lambda b, i: (b, i + off, 0)),
                  pl.BlockSpec((1, TM, D), lambda b, i: (b, i + off, 0)),
                  pl.BlockSpec(mod.shape, lambda b, i: (0, 0)),
                  pl.BlockSpec(g.shape, lambda b, i: (0, 0))],
        out_specs=pl.BlockSpec((1, TM, D), lambda b, i: (b, i, 0)),
        out_shape=jax.ShapeDtypeStruct((B, T, D), F32),
        compiler_params=_cparams(("parallel", "parallel")),
        name="final_norm",
    )(x, y, mod, g)


def _rot_perm(width, group):
    j = np.arange(width)
    half = group // 2
    return np.where(j % group < half, j + half, j - half)


def _layer_weights(l, w_in, rw_mu, rw_w0, rw_w2, rw_a0, rw_a2, rw_g2, rw_k_k, rw_k_a, rw_r_k, rw_ln_g, rw_ln_b,
                   mla_qnorm_g, mla_wuq, mla_kvnorm_g, mla_wukv):
    wi = w_in[l]
    zc = lambda n: jnp.zeros((D, n), F32)
    o = 0
    w_hg = wi[:, o:o + 5 * GW]
    o += 5 * GW
    rwc = wi[:, o:o + 1088]
    o += 1088
    swc = wi[:, o:o + 512]
    o += 512
    mlc = wi[:, o:o + 352]

    def pad_rw(a, axis):
        z = lambda n: jnp.zeros(a.shape[:axis] + (n,) + a.shape[axis + 1:], a.dtype)
        sl = lambda s, e: lax.slice_in_dim(a, s, e, axis=axis)
        return jnp.concatenate([sl(0, 768), sl(768, 832), z(64), sl(832, 896), z(64), sl(896, 960), z(64),
                                sl(960, 1088)], axis=axis)

    w_rw = pad_rw(rwc, 1)
    mu = pad_rw(rw_mu[l], 1)
    pad_rows = lambda a: jnp.concatenate([a, jnp.zeros((64, a.shape[1]), a.dtype)], axis=0)
    w2 = jnp.stack([pad_rows(rw_w2[l, 0]), pad_rows(rw_w2[l, 1])]).astype(BF16)
    a2 = pad_rows(rw_a2[l]).astype(BF16)

    q = swc[:, 0:GW]
    kx = jnp.concatenate([swc[:, 256:320], swc[:, 256:320], swc[:, 320:384], swc[:, 320:384]], axis=1)
    vx = jnp.concatenate([swc[:, 384:448], swc[:, 384:448], swc[:, 448:512], swc[:, 448:512]], axis=1)
    perm = _rot_perm(GW, 32)
    w_swa = jnp.concatenate([q, kx, vx, q[:, perm], kx[:, perm]], axis=1)

    kr = mlc[:, 320:352]
    krp = kr[:, _rot_perm(MLA_ROPE, 16)]
    w_mla = jnp.concatenate([mlc[:, 0:192], zc(64), mlc[:, 192:320], zc(64), kr, zc(32), zc(64), krp, zc(32)], axis=1)

    wuq = mla_wuq[l]
    wukv = mla_wukv[l]
    rp = _rot_perm(MLA_ROPE, 16)
    wq, wqr, wk, wv = [], [], [], []
    zq = lambda r, n: jnp.zeros((r, n), F32)
    for h in range(NH):
        qh = wuq[:, 96 * h:96 * (h + 1)]
        a = jnp.concatenate([qh, zq(192, 32)], axis=1)
        ar = jnp.concatenate([zq(192, 64), qh[:, 64:96][:, rp], zq(192, 32)], axis=1)
        wq.append(jnp.concatenate([a, zq(64, 128)], axis=0))
        wqr.append(jnp.concatenate([ar, zq(64, 128)], axis=0))
        wk.append(jnp.concatenate([wukv[:, 128 * h:128 * h + 64], zq(128, 64)], axis=1))
        wv.append(wukv[:, 128 * h + 64:128 * (h + 1)].T)
    row = lambda a: a.reshape(1, -1)
    return {
        "proj": jnp.concatenate([w_hg, w_rw, w_swa, w_mla], axis=1).astype(BF16),
        "rw": {"mu": mu, "w0": rw_w0[l], "w2": w2, "a0": row(rw_a0[l]), "a2": a2, "g2": rw_g2[l].astype(BF16),
               "k_k": row(rw_k_k[l]), "k_a": row(rw_k_a[l])},
        "rk": row(rw_r_k[l]), "lng": row(rw_ln_g[l]), "lnb": row(rw_ln_b[l]),
        "mla": {"qg": jnp.concatenate([row(mla_qnorm_g[l]), jnp.zeros((1, 64), F32)], axis=1),
                "kg": row(mla_kvnorm_g[l]),
                "wq": jnp.stack(wq).astype(BF16), "wqr": jnp.stack(wqr).astype(BF16),
                "wk": jnp.stack(wk).astype(BF16), "wv": jnp.stack(wv).astype(BF16)},
    }


def _rope_tables(T, lc):
    rows = T // GRID_W
    rowp = jnp.repeat(jnp.arange(rows, dtype=F32), GRID_W)
    colp = jnp.tile(jnp.arange(GRID_W, dtype=F32), rows)

    def table(rot_dim, width, lane0):
        nf = rot_dim // 4
        inv = ROPE_BASE ** (-jnp.arange(nf, dtype=F32) / nf)
        j = np.arange(rot_dim)
        ax = j // (2 * nf)
        fr = j % nf
        sign = np.where(j % (2 * nf) < nf, -1.0, 1.0).astype(np.float32)
        pos = jnp.stack([rowp, colp], axis=1)
        ang = pos[:, ax] * inv[fr][None, :]
        cos = jnp.concatenate([jnp.ones((lc, rot_dim), F32), jnp.cos(ang)], axis=0)
        sin = jnp.concatenate([jnp.zeros((lc, rot_dim), F32), jnp.sin(ang) * sign[None, :]], axis=0)
        reps = (width - lane0) // rot_dim if lane0 == 0 else 1
        cos = jnp.tile(cos, (1, reps))
        sin = jnp.tile(sin, (1, reps))
        return cos, sin

    cs_s, sn_s = table(HD, 2 * GW, 0)
    c32, s32 = table(MLA_ROPE, MLA_ROPE, 0)
    S = lc + T
    z = lambda n: jnp.zeros((S, n), F32)
    o = lambda n: jnp.ones((S, n), F32)
    cs_m = jnp.concatenate([z(64), c32, z(32)], axis=1)
    sn_m = jnp.concatenate([z(64), s32, z(32)], axis=1)
    cs_q = jnp.concatenate([o(64), c32, z(32)], axis=1)
    sn_q = jnp.concatenate([z(64), s32, z(32)], axis=1)
    return {"cs_s": cs_s, "sn_s": sn_s, "cs_m": cs_m, "sn_m": sn_m, "cs_q": cs_q, "sn_q": sn_q}


def kernel(x, c, ctx, c_ctx, w_ada, b_ada, norm1_g, norm2_g, w_in, hg_lb, hg_norm_g, rw_mu, rw_w0, rw_w2, rw_a0,
           rw_a2, rw_g2, rw_k_k, rw_k_a, rw_r_k, rw_ln_g, rw_ln_b, swa_sink, mla_qnorm_g, mla_wuq, mla_kvnorm_g,
           mla_wukv, w_out, ffn_w1, ffn_w3, ffn_w2, moe_router, moe_w1, moe_w3, moe_w2, final_norm_g):
    B, T, _ = x.shape
    lc = ctx.shape[1]
    S = lc + T
    L = w_in.shape[0]
    assert lc % TM == 0 and T % TM == 0 and lc % SWA_W == 0 and T % GRID_W == 0 and (B * S) % 1024 == 0 and (B * T) % 1024 == 0

    r8 = -(-(B + 1) // 8) * 8
    c8 = jnp.concatenate([c, c_ctx[None, :], jnp.zeros((r8 - B - 1, D), F32)], axis=0)
    mod_all = _ada(c8, w_ada, b_ada)
    tabs = _rope_tables(T, lc)
    xs = jnp.concatenate([ctx, x], axis=1)
    row = lambda a: a.reshape(1, -1)

    pend = None
    for l in range(L):
        w = _layer_weights(l, w_in, rw_mu, rw_w0, rw_w2, rw_a0, rw_a2, rw_g2, rw_k_k, rw_k_a, rw_r_k, rw_ln_g,
                           rw_ln_b, mla_qnorm_g, mla_wuq, mla_kvnorm_g, mla_wukv)
        mod = mod_all[l]
        xs, p_hg, p_rw, p_swa, p_mla = _proj(xs, pend, mod, row(norm1_g[l]), w["proj"], tabs, lc)

        hgo = _hgrn2(p_hg, hg_lb, l, lc)
        rwp = _rwprep(p_rw, w["rw"], lc)
        rwo = _rwscan(rwp, lc)
        sink8 = jnp.zeros((8, 128), F32).at[0:NH, :].set(jnp.broadcast_to(swa_sink[l][:, None], (NH, 128)))
        last = l == L - 1
        swo = _swa(p_swa, sink8, lc, last)
        mq, mk, mv = _mlaprep(p_mla, w["mla"], tabs)
        mlo = _mla(mq, mk, mv, lc, last)

        moe = l % 2 == 1
        router = None
        if moe:
            router = jnp.concatenate([moe_router[l // 2], jnp.zeros((D, 128 - N_EXPERTS), F32)], axis=1)
        outs = _outproj(xs, mod, row(norm2_g[l]), hgo, p_hg, row(hg_norm_g[l]), rwo, rwp, w["rk"], w["lng"],
                        w["lnb"], swo, mlo, w_out[l].astype(BF16), router, lc, last)
        xs, h2 = outs[0], outs[1]
        so = xs.shape[1]
        h2f = h2.reshape(B * so, D)
        if moe:
            e = l // 2
            y = _moe(h2f, outs[2].reshape(B * so, 128), moe_w1[e].astype(BF16), moe_w3[e].astype(BF16),
                     moe_w2[e].astype(BF16))
        else:
            e = l // 2
            y = _ffn(h2f, ffn_w1[e].astype(BF16), ffn_w3[e].astype(BF16), ffn_w2[e].astype(BF16))
        pend = (y.reshape(B, so, D), mod)

    return _final(xs, pend[0], pend[1], row(final_norm_g), xs.shape[1] - T)
```

```python
import functools

import jax
import jax.numpy as jnp
import numpy as np
from jax import lax
from jax.experimental import pallas as pl
from jax.experimental.pallas import tpu as pltpu

F32 = jnp.float32
BF16 = jnp.bfloat16

D = 1024
GW = 256
HD = 64
NH = 4
GRID_W = 64
NORM_EPS = 1e-6
ROPE_BASE = 10000.0
HG_F_FLOOR = 1e-30
RW_LN_EPS = 64e-5
SWA_W = 128
MLA_Q_RANK = 192
MLA_KV_RANK = 128
MLA_NOPE = 64
MLA_ROPE = 32
D_FF = 2816
N_EXPERTS = 8
D_FF_EXPERT = 1408

TM = 256
CH = 64
SCAN_ROWS = 256
HB = 16
RW_BPS = 2
MLA_HPS = 4
MLA_VROWS = 80
LOG2E = 1.4426950408889634
MLA_KCHUNK = 256
MLA_SKEW = 8
VMEM_LIMIT = 56 * 1024 * 1024
RW_GRAM_MODE = "bf16"
RW_BASE_MODE = "bf16"
RW_MERGE_MODE = "bf16"
RW_STATE_MODE = "bf16"

_NN = (((1,), (0,)), ((), ()))
_NT = (((1,), (1,)), ((), ()))
_TN = (((0,), (0,)), ((), ()))


def _mm(a, b, dims=_NN, mode="bf16"):
    if mode == "bf16":
        return lax.dot_general(a.astype(BF16), b.astype(BF16), dims, preferred_element_type=F32)
    if mode == "x3":
        ah = a.astype(BF16)
        al = (a - ah.astype(F32)).astype(BF16)
        bh = b.astype(BF16)
        bl = (b - bh.astype(F32)).astype(BF16)
        d = lambda x, y: lax.dot_general(x, y, dims, preferred_element_type=F32)
        return d(ah, bh) + (d(ah, bl) + d(al, bh))
    return lax.dot_general(a, b, dims, precision=lax.Precision.HIGHEST, preferred_element_type=F32)


def _sigmoid(x):
    return 1.0 / (1.0 + jnp.exp(-x))


def _silu(x):
    return x / (1.0 + jnp.exp(-x))


def _iota(shape, dim):
    return lax.broadcasted_iota(jnp.int32, shape, dim)


def _head_block_mask(n):
    return (_iota((n, n), 0) // HD) == (_iota((n, n), 1) // HD)


def _running_sum(x, fwd, seg):
    n = x.shape[0]
    pos = _iota(x.shape, 0) % seg
    sh = 1
    while sh < seg:
        if fwd:
            x = x + jnp.where(pos >= sh, pltpu.roll(x, sh, 0), 0.0)
        else:
            x = x + jnp.where(pos < seg - sh, pltpu.roll(x, n - sh, 0), 0.0)
        sh *= 2
    return x


def _alternate(*stages, skew=0):
    live = list(enumerate(stages))
    rnd = 0
    while live:
        for item in list(live):
            i, g = item
            if rnd < i * skew:
                continue
            try:
                next(g)
            except StopIteration:
                live.remove(item)
        rnd += 1


def _cparams(sem):
    return pltpu.CompilerParams(dimension_semantics=sem, vmem_limit_bytes=VMEM_LIMIT)


def _ada_body(c_ref, w_ref, b_ref, o_ref):
    s = _silu(c_ref[...])
    o_ref[0] = _mm(s, w_ref[0]) + b_ref[0]


def _ada(c8, w_ada, b_ada):
    L = w_ada.shape[0]
    r8 = c8.shape[0]
    tn = 1536
    return pl.pallas_call(
        _ada_body,
        grid=(L, 6 * D // tn),
        in_specs=[
            pl.BlockSpec((r8, D), lambda l, j: (0, 0)),
            pl.BlockSpec((1, D, tn), lambda l, j: (l, 0, j)),
            pl.BlockSpec((1, 1, tn), lambda l, j: (l, 0, j)),
        ],
        out_specs=pl.BlockSpec((1, r8, tn), lambda l, j: (l, 0, j)),
        out_shape=jax.ShapeDtypeStruct((L, r8, 6 * D), F32),
        compiler_params=_cparams(("parallel", "parallel")),
        name="adaln",
    )(c8, w_ada, b_ada.reshape(L, 1, 6 * D))


def _proj_body(*refs, pending, lc_tiles, nb):
    if pending:
        (x_ref, y_ref, modp_ref, mod_ref, g_ref, wall, cs_s, sn_s, cs_m, sn_m,
         xo_ref, phg, prw, pswa, pmla) = refs
    else:
        (x_ref, mod_ref, g_ref, wall, cs_s, sn_s, cs_m, sn_m,
         phg, prw, pswa, pmla) = refs
    b = pl.program_id(0)
    i = pl.program_id(1)
    row = jnp.where(i < lc_tiles, nb, b)
    x = x_ref[0]
    if pending:
        gate = modp_ref[pl.ds(row, 1), 5 * D:6 * D]
        x = x + gate * y_ref[0]
        xo_ref[0] = x
    sh = mod_ref[pl.ds(row, 1), 0:D]
    sc = mod_ref[pl.ds(row, 1), D:2 * D]
    ms = jnp.mean(x * x, axis=-1, keepdims=True)
    h = (x * lax.rsqrt(ms + NORM_EPS) * g_ref[...]) * (1.0 + sc) + sh
    hb = h.astype(BF16)
    n1 = 5 * GW
    phg[0] = _mm(hb, wall[:, 0:n1])
    prw[0] = _mm(hb, wall[:, n1:2 * n1])
    s = _mm(hb, wall[:, 2 * n1:3 * n1])
    qk = s[:, 0:2 * GW] * cs_s[...] + s[:, 3 * GW:5 * GW] * sn_s[...]
    pswa[0, :, 0:2 * GW] = qk.astype(BF16)
    pswa[0, :, 2 * GW:3 * GW] = s[:, 2 * GW:3 * GW].astype(BF16)
    m = _mm(hb, wall[:, 3 * n1:3 * n1 + 640])
    pmla[0, :, 0:384] = m[:, 0:384]
    pmla[0, :, 384:512] = m[:, 384:512] * cs_m[...] + m[:, 512:640] * sn_m[...]


def _proj(x, pend, mod, g, w, tabs, lc):
    B, S, _ = x.shape
    nt = S // TM
    pending = pend is not None
    row_spec = lambda n: pl.BlockSpec((1, TM, n), lambda b, i: (b, i, 0))
    full = lambda a: pl.BlockSpec(a.shape, lambda b, i: (0,) * a.ndim)
    tab_spec = lambda a: pl.BlockSpec((TM, a.shape[1]), lambda b, i: (i, 0))
    ins = [x]
    in_specs = [row_spec(D)]
    if pending:
        ins += [pend[0], pend[1]]
        in_specs += [row_spec(D), full(pend[1])]
    ins += [mod, g, w, tabs["cs_s"], tabs["sn_s"], tabs["cs_m"], tabs["sn_m"]]
    in_specs += [full(mod), full(g), full(w),
                 tab_spec(tabs["cs_s"]), tab_spec(tabs["sn_s"]), tab_spec(tabs["cs_m"]), tab_spec(tabs["sn_m"])]
    out_shape = [jax.ShapeDtypeStruct((B, S, 5 * GW), F32), jax.ShapeDtypeStruct((B, S, 5 * GW), F32),
                 jax.ShapeDtypeStruct((B, S, 3 * GW), BF16), jax.ShapeDtypeStruct((B, S, 512), F32)]
    out_specs = [row_spec(5 * GW), row_spec(5 * GW), row_spec(3 * GW), row_spec(512)]
    if pending:
        out_shape = [jax.ShapeDtypeStruct((B, S, D), F32)] + out_shape
        out_specs = [row_spec(D)] + out_specs
    outs = pl.pallas_call(
        functools.partial(_proj_body, pending=pending, lc_tiles=lc // TM, nb=B),
        grid=(B, nt), in_specs=in_specs, out_specs=out_specs, out_shape=out_shape,
        compiler_params=_cparams(("parallel", "parallel")),
        name="proj",
    )(*ins)
    if pending:
        return tuple(outs)
    return (x,) + tuple(outs)


def _chunk_of_step(d, s, nc, nc_ctx):
    bwd = jnp.where(s < nc_ctx, nc_ctx - 1 - s, nc - 1 - (s - nc_ctx))
    return jnp.where(d == 0, s, bwd)


def _hg_direction(p_ref, lb, o_ref, st_ref, fwd, c0):
    p = p_ref[0, c0:c0 + CH, :]
    z = p[:, GW:2 * GW] if fwd else p[:, 2 * GW:3 * GW]
    sig = _sigmoid(z)
    f = lb + (1.0 - lb) * sig
    logf = jnp.log(jnp.maximum(f, HG_F_FLOOR))
    key = (1.0 - lb) * (1.0 - sig)
    qs = _silu(p[:, 0:GW])
    iv = p[:, 3 * GW:4 * GW]

    bcum = _running_sum(logf, fwd, HB)

    bdf = _head_block_mask(GW).astype(BF16)
    hm = (_iota((NH * HB, GW), 0) // HB) == (_iota((NH * HB, GW), 1) // HD)
    expand = lambda a: jnp.where(hm, jnp.concatenate([a] * NH, axis=0), 0.0)
    half = HB // 2
    nblk = CH // HB
    order = [(j if fwd else nblk - 1 - j) * HB for j in range(nblk)]
    blocks = [tuple(a[r0:r0 + HB] for a in (qs, key, iv, bcum)) for r0 in order]
    es = []
    spans = []
    for qb, kb, ib, bb in blocks:
        for sidx in range(HB):
            if fwd:
                lo, hi = (0, HB) if sidx < half else (half, HB)
            else:
                lo, hi = (0, half) if sidx < half else (0, HB)
            rows = _iota((hi - lo, 1), 0) + lo
            m = (rows >= sidx) if fwd else (rows <= sidx)
            dec = jnp.exp(jnp.where(m, bb[lo:hi] - bb[sidx:sidx + 1], -jnp.inf))
            es.append(qb[lo:hi] * kb[sidx:sidx + 1] * dec)
            spans.append((lo, hi))
        yield
    att = _mm(jnp.concatenate(es, axis=0), bdf)
    yield
    inblock = []
    outers = []
    off = 0
    for j, (qb, kb, ib, bb) in enumerate(blocks):
        top = jnp.zeros((half, GW), F32)
        bot = jnp.zeros((half, GW), F32)
        for sidx in range(HB):
            lo, hi = spans[j * HB + sidx]
            a = att[off:off + hi - lo] * ib[sidx:sidx + 1]
            off += hi - lo
            if hi - lo == HB:
                top = top + a[0:half]
                bot = bot + a[half:HB]
            elif lo == 0:
                top = top + a
            else:
                bot = bot + a
        inblock.append((top, bot))
        tot = bb[HB - 1:HB] if fwd else bb[0:1]
        outers.append((_mm(expand(ib), expand(kb * jnp.exp(tot - bb)), _TN), jnp.exp(tot)))
        yield
    for r0, (qb, kb, ib, bb), (top, bot), (outer, dtot) in zip(order, blocks, inblock, outers):
        st = st_ref[...]
        o_blk = _mm(qb * jnp.exp(bb), st, _NT)
        o_ref[0, c0 + r0:c0 + r0 + half, :] = o_blk[0:half] + top
        o_ref[0, c0 + r0 + half:c0 + r0 + HB, :] = o_blk[half:HB] + bot
        st_ref[...] = st * dtot + outer
        yield


def _hg_body(pf_ref, pb_ref, lbraw_ref, of_ref, ob_ref, stf_ref, stb_ref, *, layer):
    @pl.when(pl.program_id(1) == 0)
    def _():
        stf_ref[...] = jnp.zeros_like(stf_ref)
        stb_ref[...] = jnp.zeros_like(stb_ref)

    raw = lbraw_ref[...]
    nl = raw.shape[0]
    mx = raw[0]
    for j in range(1, nl):
        mx = jnp.maximum(mx, raw[j])
    ex = [jnp.exp(raw[j] - mx) for j in range(nl)]
    den = ex[0]
    for j in range(1, nl):
        den = den + ex[j]
    lb2 = jnp.zeros_like(mx)
    for j in range(1, layer + 1):
        lb2 = lb2 + ex[j] / den
    def scan(p_ref, lb, o_ref, st_ref, fwd):
        chunks = range(SCAN_ROWS // CH)
        for c in (chunks if fwd else reversed(chunks)):
            yield from _hg_direction(p_ref, lb, o_ref, st_ref, fwd, c * CH)

    _alternate(scan(pf_ref, lb2[0:1], of_ref, stf_ref, True), scan(pb_ref, lb2[1:2], ob_ref, stb_ref, False))


def _hgrn2(p_hg, hg_lb, layer, lc):
    B, S, _ = p_hg.shape
    rs = SCAN_ROWS
    nc = S // rs
    nc_ctx = lc // rs
    fmap = lambda b, s: (b, s, 0)
    bmap = lambda b, s: (b, _chunk_of_step(1, s, nc, nc_ctx), 0)
    return pl.pallas_call(
        functools.partial(_hg_body, layer=layer),
        grid=(B, nc),
        in_specs=[pl.BlockSpec((1, rs, 5 * GW), fmap), pl.BlockSpec((1, rs, 5 * GW), bmap),
                  pl.BlockSpec(hg_lb.shape, lambda b, s: (0, 0, 0))],
        out_specs=[pl.BlockSpec((1, rs, GW), fmap), pl.BlockSpec((1, rs, GW), bmap)],
        out_shape=[jax.ShapeDtypeStruct((B, S, GW), F32)] * 2,
        scratch_shapes=[pltpu.VMEM((GW, GW), F32)] * 2,
        compiler_params=_cparams(("parallel", "arbitrary")),
        name="hgrn2",
    )(p_hg, p_hg, hg_lb)


def _rwprep_body(cur_ref, prv_ref, nxt_ref, mu_ref, w0_ref, w2_ref, a0_ref, a2_ref, g2_ref, kk_ref, ka_ref,
                 o_ref, buf, *, lc_tiles):
    i = pl.program_id(1)
    nt = pl.num_programs(1)
    first = (i == 0) | (i == lc_tiles)
    lastt = (i == lc_tiles - 1) | (i == nt - 1)
    buf[0:8, :] = jnp.where(first, 0.0, prv_ref[0])
    buf[8:8 + TM, :] = cur_ref[0]
    buf[8 + TM:16 + TM, :] = jnp.where(lastt, 0.0, nxt_ref[0])
    cur = buf[8:8 + TM, :]
    prev = buf[7:7 + TM, :]
    nxt = buf[9:9 + TM, :]
    sft = cur + mu_ref[0:1, :] * (prev - cur) + mu_ref[1:2, :] * (nxt - cur)
    r = sft[:, 0:GW]
    k = sft[:, GW:2 * GW]
    v = sft[:, 2 * GW:3 * GW]
    ywf = sft[:, 768:896]
    ywb = sft[:, 896:1024]
    ya = sft[:, 1024:1152]
    yg = sft[:, 1152:1280]
    c = float(np.exp(-0.5))
    lwf = -c * _sigmoid(w0_ref[0:1, :] + _mm(jnp.tanh(ywf), w2_ref[0]))
    lwb = -c * _sigmoid(w0_ref[1:2, :] + _mm(jnp.tanh(ywb), w2_ref[1]))
    a = _sigmoid(a0_ref[...] + _mm(ya, a2_ref[...]))
    kk0 = k * kk_ref[...]
    ss = _mm(kk0 * kk0, _head_block_mask(GW).astype(F32), mode="x3")
    kk = kk0 / jnp.maximum(jnp.sqrt(ss), 1e-12)
    kp = k * (1.0 + (a - 1.0) * ka_ref[...])
    g = _mm(_sigmoid(yg), g2_ref[...])
    o_ref[0, :, 0:GW] = r
    o_ref[0, :, GW:2 * GW] = lwf
    o_ref[0, :, 2 * GW:3 * GW] = lwb
    o_ref[0, :, 3 * GW:4 * GW] = kp
    o_ref[0, :, 4 * GW:5 * GW] = v
    o_ref[0, :, 5 * GW:6 * GW] = kk
    o_ref[0, :, 6 * GW:7 * GW] = kk * a
    o_ref[0, :, 7 * GW:8 * GW] = g


def _rwprep(p_rw, w, lc):
    B, S, n = p_rw.shape
    nt = S // TM
    t8 = TM // 8
    full = lambda a: pl.BlockSpec(a.shape, lambda b, i: (0,) * a.ndim)
    small = [w["mu"], w["w0"], w["w2"], w["a0"], w["a2"], w["g2"], w["k_k"], w["k_a"]]
    return pl.pallas_call(
        functools.partial(_rwprep_body, lc_tiles=lc // TM),
        grid=(B, nt),
        in_specs=[pl.BlockSpec((1, TM, n), lambda b, i: (b, i, 0)),
                  pl.BlockSpec((1, 8, n), lambda b, i: (b, jnp.maximum(i * t8 - 1, 0), 0)),
                  pl.BlockSpec((1, 8, n), lambda b, i: (b, jnp.minimum((i + 1) * t8, S // 8 - 1), 0))]
                 + [full(a) for a in small],
        out_specs=pl.BlockSpec((1, TM, 8 * GW), lambda b, i: (b, i, 0)),
        out_shape=jax.ShapeDtypeStruct((B, S, 8 * GW), F32),
        scratch_shapes=[pltpu.VMEM((TM + 16, n), F32)],
        compiler_params=_cparams(("parallel", "parallel")),
        name="rwkv_prep",
    )(p_rw, p_rw, p_rw, *small)


def _rw_direction(x_ref, o_ref, h_ref, fwd, bi, c0):
    x = x_ref[bi, c0:c0 + CH, :]
    r = x[:, 0:GW]
    lw = x[:, GW:2 * GW] if fwd else x[:, 2 * GW:3 * GW]
    k = x[:, 3 * GW:4 * GW]
    v = x[:, 4 * GW:5 * GW]
    kk = x[:, 5 * GW:6 * GW]
    bb = x[:, 6 * GW:7 * GW]

    incl = _running_sum(lw, fwd, CH)
    tot = incl[CH - 1:CH] if fwd else incl[0:1]
    e_in = jnp.exp(incl)
    e_ni = jnp.exp(-incl)
    e_tt = jnp.exp(tot - incl)
    km = kk * jnp.exp(incl - lw)
    rt = r * e_in
    bh = bb * e_ni
    kh = k * e_ni
    bt = bb * e_tt
    kt = k * e_tt

    n = NH * CH
    hm = (_iota((n, GW), 0) // CH) == (_iota((n, GW), 1) // HD)
    tile4 = lambda a: jnp.concatenate([a] * NH, axis=0)
    expand = lambda a: jnp.where(hm, tile4(a), 0.0)
    km_e = expand(km)
    rt_e = expand(rt)
    v_e = expand(v)
    bt_e = expand(bt)
    kt_e = expand(kt)
    bh_t = tile4(bh)
    kh_t = tile4(kh)

    rr = _iota((n, n), 0)
    cc = _iota((n, n), 1)
    same = (rr // CH) == (cc // CH)
    rt_i = rr % CH
    ct_i = cc % CH
    before = same & ((ct_i < rt_i) if fwd else (ct_i > rt_i))
    upto = same & ((ct_i <= rt_i) if fwd else (ct_i >= rt_i))
    eye = rr == cc

    pg = RW_GRAM_MODE
    a_m = jnp.where(before, _mm(km_e, bh_t, _NT, pg), 0.0)
    yield

    blk = lambda m: (rr // m) == (cc // m)
    pb = RW_BASE_MODE
    nm = jnp.where(blk(8), -a_m, 0.0)
    x2 = _mm(nm, nm, mode=pb)
    b_m = jnp.where(before, _mm(km_e, kh_t, _NT, pg), 0.0)
    yield
    tm = jnp.where(eye, 1.0, 0.0) + nm
    tm = tm + _mm(tm, x2, mode=pb)
    x4 = _mm(x2, x2, mode=pb)
    yield
    tm = tm + _mm(tm, x4, mode=pb)
    mrb = jnp.where(upto, _mm(rt_e, bh_t, _NT, pg), 0.0)
    yield
    pm = RW_MERGE_MODE
    side = [lambda: jnp.where(upto, _mm(rt_e, kh_t, _NT, pg), 0.0), lambda: _mm(b_m, v_e, mode=pg),
            lambda: _mm(kt_e, v_e, _TN, pg)]
    done = []
    for m in (8, 16, 32):
        low = jnp.where(blk(2 * m) & (rr // m != cc // m), a_m, 0.0)
        y = _mm(low, tm, mode=pm)
        done.append(side.pop(0)())
        yield
        tm = tm - _mm(tm, y, mode=pm)
        yield
    mrk, bmv, ktv = done

    p1 = _mm(tm, km_e, mode=pg)
    q1 = _mm(tm, bmv, mode=pg)
    yield
    p2e = rt_e - _mm(mrb, p1, mode=pg)
    q2e = _mm(mrk, v_e, mode=pg) - _mm(mrb, q1, mode=pg)
    g_m = jnp.where(eye, jnp.exp(tot), 0.0) - _mm(bt_e, p1, _TN, pg)
    f_m = ktv - _mm(bt_e, q1, _TN, pg)
    yield
    fold = lambda a: a[0:CH] + a[CH:2 * CH] + a[2 * CH:3 * CH] + a[3 * CH:4 * CH]
    p2 = fold(p2e)
    q2 = fold(q2e)
    h = h_ref[bi]
    ps = RW_STATE_MODE
    o_ref[bi, c0:c0 + CH, :] = _mm(p2, h, mode=ps) + q2
    h_ref[bi] = _mm(g_m, h, mode=ps) + f_m


def _rw_body(xf_ref, xb_ref, of_ref, ob_ref, hf_ref, hb_ref):
    @pl.when(pl.program_id(1) == 0)
    def _():
        hf_ref[...] = jnp.zeros_like(hf_ref)
        hb_ref[...] = jnp.zeros_like(hb_ref)

    def scan(x_ref, o_ref, h_ref, fwd, bi):
        chunks = range(SCAN_ROWS // CH)
        for c in (chunks if fwd else reversed(chunks)):
            yield from _rw_direction(x_ref, o_ref, h_ref, fwd, bi, c * CH)

    chains = []
    for bi in range(xf_ref.shape[0]):
        chains.append(scan(xf_ref, of_ref, hf_ref, True, bi))
        chains.append(scan(xb_ref, ob_ref, hb_ref, False, bi))
    _alternate(*chains)


def _rwscan(prep, lc):
    B, S, n = prep.shape
    rs = SCAN_ROWS
    nc = S // rs
    nc_ctx = lc // rs
    bps = RW_BPS if B % RW_BPS == 0 else 1
    fmap = lambda b, s: (b, s, 0)
    bmap = lambda b, s: (b, _chunk_of_step(1, s, nc, nc_ctx), 0)
    return pl.pallas_call(
        _rw_body,
        grid=(B // bps, nc),
        in_specs=[pl.BlockSpec((bps, rs, n), fmap), pl.BlockSpec((bps, rs, n), bmap)],
        out_specs=[pl.BlockSpec((bps, rs, GW), fmap), pl.BlockSpec((bps, rs, GW), bmap)],
        out_shape=[jax.ShapeDtypeStruct((B, S, GW), F32)] * 2,
        scratch_shapes=[pltpu.VMEM((bps, GW, GW), F32)] * 2,
        compiler_params=_cparams(("parallel", "arbitrary")),
        name="rwkv_scan",
    )(prep, prep)


def _swa_block(q, kloc, vloc, kc, vc, sink_ref, o_ref, r0, j, lc_blocks, nbl):
    w = SWA_W
    is_lat = j >= lc_blocks
    nblk = j - lc_blocks
    lane_h = _iota((w, GW), 1) // HD
    qs = jnp.concatenate([jnp.where(lane_h == h, q, jnp.zeros_like(q)) for h in range(NH)], axis=0)
    scale = HD ** -0.5
    s_loc = _mm(qs, kloc, _NT) * scale
    s_ctx = _mm(qs, kc, _NT) * scale
    yield
    qi = _iota((NH * w, 3 * w), 0) % w
    col = _iota((NH * w, 3 * w), 1)
    seg = col // w
    kj = col % w
    valid = ((seg == 0) & (nblk >= 1) & (kj >= qi)) | (seg == 1) | ((seg == 2) & (nblk < nbl - 1) & (kj <= qi))
    valid = valid & is_lat
    s_loc = jnp.where(valid, s_loc, -jnp.inf)
    hrow = _iota((NH * w, 1), 0) // w
    sink = jnp.zeros((NH * w, 1), F32)
    for h in range(NH):
        sink = jnp.where(hrow == h, sink_ref[h:h + 1, 0:1], sink)
    m = jnp.maximum(jnp.maximum(jnp.max(s_loc, axis=-1, keepdims=True), jnp.max(s_ctx, axis=-1, keepdims=True)), sink)
    p_loc = jnp.exp(s_loc - m)
    p_ctx = jnp.exp(s_ctx - m)
    den = jnp.sum(p_loc, axis=-1, keepdims=True) + jnp.sum(p_ctx, axis=-1, keepdims=True) + jnp.exp(sink - m)
    yield
    of = (_mm(p_loc, vloc) + _mm(p_ctx, vc)) / den
    out = jnp.zeros((w, GW), F32)
    for h in range(NH):
        out = out + jnp.where(lane_h == h, of[h * w:(h + 1) * w], 0.0)
    o_ref[0, r0:r0 + w, :] = out
    yield


def _swa_body(pair_ref, prv_ref, nxt_ref, ctx_ref, sink_ref, o_ref, *, lc_blocks, nbl, i0):
    i = pl.program_id(1) + i0
    w = SWA_W
    kcol = slice(GW, 2 * GW)
    vcol = slice(2 * GW, 3 * GW)
    ka, kb = pair_ref[0, 0:w, kcol], pair_ref[0, w:2 * w, kcol]
    va, vb = pair_ref[0, 0:w, vcol], pair_ref[0, w:2 * w, vcol]
    kc = ctx_ref[0, :, kcol]
    vc = ctx_ref[0, :, vcol]
    cat = lambda *a: jnp.concatenate(a, axis=0)
    first = _swa_block(pair_ref[0, 0:w, 0:GW], cat(prv_ref[0, :, kcol], ka, kb), cat(prv_ref[0, :, vcol], va, vb),
                       kc, vc, sink_ref, o_ref, 0, 2 * i, lc_blocks, nbl)
    second = _swa_block(pair_ref[0, w:2 * w, 0:GW], cat(ka, kb, nxt_ref[0, :, kcol]), cat(va, vb, nxt_ref[0, :, vcol]),
                        kc, vc, sink_ref, o_ref, w, 2 * i + 1, lc_blocks, nbl)
    _alternate(first, second, skew=1)


def _swa(p_swa, sink8, lc, skip_ctx):
    B, S, n = p_swa.shape
    w = SWA_W
    nb = S // w
    lcb = lc // w
    assert nb % 2 == 0 and lcb % 2 == 0
    i0 = lcb // 2 if skip_ctx else 0
    return pl.pallas_call(
        functools.partial(_swa_body, lc_blocks=lcb, nbl=nb - lcb, i0=i0),
        grid=(B, nb // 2 - i0),
        in_specs=[pl.BlockSpec((1, 2 * w, n), lambda b, i: (b, i + i0, 0)),
                  pl.BlockSpec((1, w, n), lambda b, i: (b, jnp.maximum(2 * (i + i0) - 1, 0), 0)),
                  pl.BlockSpec((1, w, n), lambda b, i: (b, jnp.minimum(2 * (i + i0) + 2, nb - 1), 0)),
                  pl.BlockSpec((1, lc, n), lambda b, i: (b, 0, 0)),
                  pl.BlockSpec(sink8.shape, lambda b, i: (0, 0))],
        out_specs=pl.BlockSpec((1, 2 * w, GW), lambda b, i: (b, i, 0)),
        out_shape=jax.ShapeDtypeStruct((B, S - 2 * w * i0, GW), F32),
        compiler_params=_cparams(("parallel", "parallel")),
        name="swa",
    )(p_swa, p_swa, p_swa, p_swa, sink8)


def _mlaprep_body(p_ref, qg_ref, kg_ref, wq, wqr, wk, wv, cs_ref, sn_ref, q_ref, k_ref, v_ref):
    p = p_ref[0]
    cq = p[:, 0:256]
    ckv = p[:, 256:384]
    krs = p[:, 384:512]
    qn = cq * lax.rsqrt(jnp.sum(cq * cq, axis=-1, keepdims=True) / MLA_Q_RANK + NORM_EPS) * qg_ref[...]
    kvn = ckv * lax.rsqrt(jnp.mean(ckv * ckv, axis=-1, keepdims=True) + NORM_EPS) * kg_ref[...]
    scale = (MLA_NOPE + MLA_ROPE) ** -0.5 * LOG2E
    cs = cs_ref[...]
    sn = sn_ref[...]
    kvn_t = kvn.T.astype(BF16)
    ones_rows = (_iota((MLA_VROWS - HD, TM), 0) == 0).astype(BF16)
    for h in range(NH):
        qh = _mm(qn, wq[h]) * cs + _mm(qn, wqr[h]) * sn
        q_ref[0, h] = (qh * scale).T.astype(BF16)
        k_ref[0, h] = (_mm(kvn, wk[h]) + krs).astype(BF16)
        v_ref[0, h, 0:HD, :] = _mm(wv[h], kvn_t).astype(BF16)
        v_ref[0, h, HD:MLA_VROWS, :] = ones_rows


def _mlaprep(p_mla, w, tabs):
    B, S, n = p_mla.shape
    nt = S // TM
    full = lambda a: pl.BlockSpec(a.shape, lambda b, i: (0,) * a.ndim)
    ws = [w["qg"], w["kg"], w["wq"], w["wqr"], w["wk"], w["wv"]]
    hspec = lambda c: pl.BlockSpec((1, NH, TM, c), lambda b, i: (b, 0, i, 0))
    return pl.pallas_call(
        _mlaprep_body,
        grid=(B, nt),
        in_specs=[pl.BlockSpec((1, TM, n), lambda b, i: (b, i, 0))] + [full(a) for a in ws]
                 + [pl.BlockSpec((TM, 128), lambda b, i: (i, 0))] * 2,
        out_specs=[pl.BlockSpec((1, NH, 128, TM), lambda b, i: (b, 0, 0, i)), hspec(128),
                   pl.BlockSpec((1, NH, MLA_VROWS, TM), lambda b, i: (b, 0, 0, i))],
        out_shape=[jax.ShapeDtypeStruct((B, NH, 128, S), BF16), jax.ShapeDtypeStruct((B, NH, S, 128), BF16),
                   jax.ShapeDtypeStruct((B, NH, MLA_VROWS, S), BF16)],
        compiler_params=_cparams(("parallel", "parallel")),
        name="mla_prep",
    )(p_mla, *ws, tabs["cs_q"], tabs["sn_q"])


def _mla_body(q_ref, k_ref, v_ref, o_ref, *, lc, t0):
    i = pl.program_id(2) + t0
    ctx_q = i * TM < lc
    kc = MLA_KCHUNK
    nk = k_ref.shape[2] // kc

    def head(h):
        q = q_ref[0, h]
        ss = []
        m = None
        for c in range(nk):
            s = _mm(k_ref[0, h, c * kc:(c + 1) * kc, :], q)
            if c * kc >= lc and t0 * TM < lc:
                s = jnp.where(ctx_q, -jnp.inf, s)
            ss.append(s)
            cm = jnp.max(s, axis=0, keepdims=True)
            m = cm if m is None else jnp.maximum(m, cm)
            yield
        acc = None
        for c in range(nk):
            p = jnp.exp2(ss[c] - m)
            pv = _mm(v_ref[0, h, :, c * kc:(c + 1) * kc], p)
            acc = pv if acc is None else acc + pv
            yield
        o_ref[0, h] = acc[0:HD] / acc[HD:HD + 1]

    _alternate(*[head(h) for h in range(MLA_HPS)], skew=MLA_SKEW)


def _mla(q, k, v, lc, skip_ctx):
    B, H, S, _ = k.shape
    nt = S // TM
    hp = MLA_HPS
    t0 = lc // TM if skip_ctx else 0
    return pl.pallas_call(
        functools.partial(_mla_body, lc=lc, t0=t0),
        grid=(B, H // hp, nt - t0),
        in_specs=[pl.BlockSpec((1, hp, 128, TM), lambda b, h, i: (b, h, 0, i + t0)),
                  pl.BlockSpec((1, hp, S, 128), lambda b, h, i: (b, h, 0, 0)),
                  pl.BlockSpec((1, hp, MLA_VROWS, S), lambda b, h, i: (b, h, 0, 0))],
        out_specs=pl.BlockSpec((1, hp, HD, TM), lambda b, h, i: (b, h, 0, i)),
        out_shape=jax.ShapeDtypeStruct((B, H, HD, S - t0 * TM), F32),
        compiler_params=_cparams(("parallel", "parallel", "parallel")),
        name="mla_attn",
    )(q, k, v)


def _outproj_body(*refs, lc_tiles, nb, moe, t0):
    if moe:
        (x_ref, mod_ref, n2_ref, hgf_ref, hgb_ref, hgg_ref, hgn_ref, rwf_ref, rwb_ref, rwp_ref, rk_ref, lng_ref, lnb_ref,
         swo_ref, mlo_ref, wout_ref, rt_ref, xo_ref, h2_ref, lg_ref) = refs
    else:
        (x_ref, mod_ref, n2_ref, hgf_ref, hgb_ref, hgg_ref, hgn_ref, rwf_ref, rwb_ref, rwp_ref, rk_ref, lng_ref, lnb_ref,
         swo_ref, mlo_ref, wout_ref, xo_ref, h2_ref) = refs
    b = pl.program_id(0)
    i = pl.program_id(1) + t0
    row = jnp.where(i < lc_tiles, nb, b)
    bd = _head_block_mask(GW).astype(BF16)

    def hsum(a):
        hi = a.astype(BF16)
        lo = (a - hi.astype(F32)).astype(BF16)
        return _mm(hi, bd) + _mm(lo, bd)

    o = hgf_ref[0] + hgb_ref[0]
    hg = o * lax.rsqrt(hsum(o * o) / HD + NORM_EPS) * hgn_ref[...] * _silu(hgg_ref[0])

    o = rwf_ref[0] + rwb_ref[0]
    c = o - hsum(o) / HD
    on = c * lax.rsqrt(hsum(c * c) / HD + RW_LN_EPS) * lng_ref[...] + lnb_ref[...]
    r = rwp_ref[0, :, 0:GW]
    kp = rwp_ref[0, :, 3 * GW:4 * GW]
    v = rwp_ref[0, :, 4 * GW:5 * GW]
    g = rwp_ref[0, :, 7 * GW:8 * GW]
    rw = (on + hsum(r * kp * rk_ref[...]) * v) * g

    acc = _mm(hg, wout_ref[0:GW, :]) + _mm(rw, wout_ref[GW:2 * GW, :]) + _mm(swo_ref[0], wout_ref[2 * GW:3 * GW, :])
    for h in range(NH):
        acc = acc + _mm(mlo_ref[0, h], wout_ref[3 * GW + h * HD:3 * GW + (h + 1) * HD, :], _TN)

    x = x_ref[0] + mod_ref[pl.ds(row, 1), 2 * D:3 * D] * acc
    xo_ref[0] = x
    sh = mod_ref[pl.ds(row, 1), 3 * D:4 * D]
    sc = mod_ref[pl.ds(row, 1), 4 * D:5 * D]
    ms = jnp.mean(x * x, axis=-1, keepdims=True)
    h2 = (x * lax.rsqrt(ms + NORM_EPS) * n2_ref[...]) * (1.0 + sc) + sh
    h2_ref[0] = h2.astype(BF16)
    if moe:
        lg_ref[0] = _mm(h2, rt_ref[...], mode="x3")


def _outproj(x, mod, n2g, hgo, p_hg, hgn, rwo, rwp, rk, lng, lnb, swo, mlo, wout, router, lc, skip_ctx):
    B, S, _ = x.shape
    nt = S // TM
    moe = router is not None
    t0 = lc // TM if skip_ctx else 0
    so = S - t0 * TM
    row = lambda n: pl.BlockSpec((1, TM, n), lambda b, i: (b, i + t0, 0))
    lat = lambda n: pl.BlockSpec((1, TM, n), lambda b, i: (b, i, 0))
    full = lambda a: pl.BlockSpec(a.shape, lambda b, i: (0,) * a.ndim)
    ins = [x, mod, n2g, hgo[0], hgo[1], p_hg, hgn, rwo[0], rwo[1], rwp, rk, lng, lnb, swo, mlo, wout]
    in_specs = [row(D), full(mod), full(n2g), row(GW), row(GW),
                pl.BlockSpec((1, TM, GW), lambda b, i: (b, i + t0, 4)), full(hgn),
                row(GW), row(GW), row(8 * GW), full(rk), full(lng), full(lnb), lat(GW),
                pl.BlockSpec((1, NH, HD, TM), lambda b, i: (b, 0, 0, i)), full(wout)]
    out_shape = [jax.ShapeDtypeStruct((B, so, D), F32), jax.ShapeDtypeStruct((B, so, D), BF16)]
    out_specs = [lat(D), lat(D)]
    if moe:
        ins.append(router)
        in_specs.append(full(router))
        out_shape.append(jax.ShapeDtypeStruct((B, so, 128), F32))
        out_specs.append(lat(128))
    return pl.pallas_call(
        functools.partial(_outproj_body, lc_tiles=lc // TM, nb=B, moe=moe, t0=t0),
        grid=(B, nt - t0), in_specs=in_specs, out_specs=out_specs, out_shape=out_shape,
        compiler_params=_cparams(("parallel", "parallel")),
        name="outproj",
    )(*ins)


def _ffn_body(h_ref, w1_ref, w3_ref, w2_ref, o_ref, acc_ref):
    j = pl.program_id(1)

    @pl.when(j == 0)
    def _():
        acc_ref[...] = jnp.zeros_like(acc_ref)

    h = h_ref[...]
    a = _mm(h, w1_ref[...])
    z = _silu(a) * _mm(h, w3_ref[...])
    acc_ref[...] += _mm(z, w2_ref[...])

    @pl.when(j == pl.num_programs(1) - 1)
    def _():
        o_ref[...] = acc_ref[...]


def _ffn(h2, w1, w3, w2):
    M = h2.shape[0]
    tm = 1024
    tf = D_FF // 2
    return pl.pallas_call(
        _ffn_body,
        grid=(M // tm, D_FF // tf),
        in_specs=[pl.BlockSpec((tm, D), lambda i, j: (i, 0)),
                  pl.BlockSpec((D, tf), lambda i, j: (0, j)),
                  pl.BlockSpec((D, tf), lambda i, j: (0, j)),
                  pl.BlockSpec((tf, D), lambda i, j: (j, 0))],
        out_specs=pl.BlockSpec((tm, D), lambda i, j: (i, 0)),
        out_shape=jax.ShapeDtypeStruct((M, D), F32),
        scratch_shapes=[pltpu.VMEM((tm, D), F32)],
        compiler_params=_cparams(("parallel", "arbitrary")),
        name="ffn",
    )(h2, w1, w3, w2)


def _moe_body(*refs, final_rows):
    if final_rows:
        h_ref, lg_ref, w1_ref, w3_ref, w2_ref, x_ref, mod_ref, g_ref, o_ref, acc_ref, comb_ref = refs
    else:
        h_ref, lg_ref, w1_ref, w3_ref, w2_ref, o_ref, acc_ref, comb_ref = refs
    e = pl.program_id(1)
    tile = pl.program_id(0)
    lane = _iota(comb_ref.shape, 1)

    @pl.when(e == 0)
    def _():
        acc_ref[...] = jnp.zeros_like(acc_ref)
        lg = jnp.where(lane < N_EXPERTS, lg_ref[...], -jnp.inf)
        v1 = jnp.max(lg, axis=-1, keepdims=True)
        i1 = jnp.min(jnp.where(lg == v1, lane, 128), axis=-1, keepdims=True)
        lg2 = jnp.where(lane == i1, -jnp.inf, lg)
        v2 = jnp.max(lg2, axis=-1, keepdims=True)
        i2 = jnp.min(jnp.where(lg2 == v2, lane, 128), axis=-1, keepdims=True)
        ex = jnp.exp(v2 - v1)
        comb_ref[...] = jnp.where(lane == i1, 1.0 / (1.0 + ex), 0.0) + jnp.where(lane == i2, ex / (1.0 + ex), 0.0)

    ce = jnp.sum(jnp.where(lane == e, comb_ref[...], 0.0), axis=-1, keepdims=True)
    h = h_ref[...]
    a = _mm(h, w1_ref[0])
    z = _silu(a) * _mm(h, w3_ref[0])
    acc_ref[...] += ce * _mm(z, w2_ref[0])

    @pl.when(e == pl.num_programs(1) - 1)
    def _():
        if final_rows:
            b = tile // (final_rows // acc_ref.shape[0])
            x = x_ref[...] + mod_ref[pl.ds(b, 1), 5 * D:6 * D] * acc_ref[...]
            ms = jnp.mean(x * x, axis=-1, keepdims=True)
            o_ref[...] = x * lax.rsqrt(ms + NORM_EPS) * g_ref[...]
        else:
            o_ref[...] = acc_ref[...]


def _moe(h2, logits, w1, w3, w2, fin=None):
    M = h2.shape[0]
    tm = 1024
    f = w1.shape[2]
    ins = [h2, logits, w1, w3, w2]
    in_specs = [pl.BlockSpec((tm, D), lambda i, e: (i, 0)),
                pl.BlockSpec((tm, 128), lambda i, e: (i, 0)),
                pl.BlockSpec((1, D, f), lambda i, e: (e, 0, 0)),
                pl.BlockSpec((1, D, f), lambda i, e: (e, 0, 0)),
                pl.BlockSpec((1, f, D), lambda i, e: (e, 0, 0))]
    final_rows = 0
    if fin is not None:
        x, mod, g, final_rows = fin
        assert final_rows % tm == 0
        ins += [x, mod, g]
        in_specs += [pl.BlockSpec((tm, D), lambda i, e: (i, 0)), pl.BlockSpec(mod.shape, lambda i, e: (0, 0)),
                     pl.BlockSpec(g.shape, lambda i, e: (0, 0))]
    return pl.pallas_call(
        functools.partial(_moe_body, final_rows=final_rows),
        grid=(M // tm, N_EXPERTS),
        in_specs=in_specs,
        out_specs=pl.BlockSpec((tm, D), lambda i, e: (i, 0)),
        out_shape=jax.ShapeDtypeStruct((M, D), F32),
        scratch_shapes=[pltpu.VMEM((tm, D), F32), pltpu.VMEM((tm, 128), F32)],
        compiler_params=_cparams(("parallel", "arbitrary")),
        name="moe",
    )(*ins)


def _final_body(x_ref, y_ref, mod_ref, g_ref, o_ref):
    b = pl.program_id(0)
    x = x_ref[0] + mod_ref[pl.ds(b, 1), 5 * D:6 * D] * y_ref[0]
    ms = jnp.mean(x * x, axis=-1, keepdims=True)
    o_ref[0] = x * lax.rsqrt(ms + NORM_EPS) * g_ref[...]


def _final(x, y, mod, g, lc):
    B, S, _ = x.shape
    T = S - lc
    off = lc // TM
    return pl.pallas_call(
        _final_body,
        grid=(B, T // TM),
        in_specs=[pl.BlockSpec((1, TM, D), lambda b, i: (b, i + off, 0)),
                  pl.BlockSpec((1, TM, D), lambda b, i: (b, i + off, 0)),
                  pl.BlockSpec(mod.shape, lambda b, i: (0, 0)),
                  pl.BlockSpec(g.shape, lambda b, i: (0, 0))],
        out_specs=pl.BlockSpec((1, TM, D), lambda b, i: (b, i, 0)),
        out_shape=jax.ShapeDtypeStruct((B, T, D), F32),
        compiler_params=_cparams(("parallel", "parallel")),
        name="final_norm",
    )(x, y, mod, g)


def _swap_halves(a, group):
    n = a.shape[-1]
    return a.reshape(a.shape[:-1] + (n // group, 2, group // 2))[..., ::-1, :].reshape(a.shape)


def _layer_weights(l, w_in, rw_mu, rw_w0, rw_w2, rw_a0, rw_a2, rw_g2, rw_k_k, rw_k_a, rw_r_k, rw_ln_g, rw_ln_b,
                   mla_qnorm_g, mla_wuq, mla_kvnorm_g, mla_wukv):
    wi = w_in[l]
    zc = lambda n: jnp.zeros((D, n), F32)
    o = 0
    w_hg = wi[:, o:o + 5 * GW]
    o += 5 * GW
    rwc = wi[:, o:o + 1088]
    o += 1088
    swc = wi[:, o:o + 512]
    o += 512
    mlc = wi[:, o:o + 352]

    def pad_rw(a, axis):
        z = lambda n: jnp.zeros(a.shape[:axis] + (n,) + a.shape[axis + 1:], a.dtype)
        sl = lambda s, e: lax.slice_in_dim(a, s, e, axis=axis)
        return jnp.concatenate([sl(0, 768), sl(768, 832), z(64), sl(832, 896), z(64), sl(896, 960), z(64),
                                sl(960, 1088)], axis=axis)

    w_rw = pad_rw(rwc, 1)
    mu = pad_rw(rw_mu[l], 1)
    pad_rows = lambda a: jnp.concatenate([a, jnp.zeros((64, a.shape[1]), a.dtype)], axis=0)
    w2 = jnp.stack([pad_rows(rw_w2[l, 0]), pad_rows(rw_w2[l, 1])]).astype(BF16)
    a2 = pad_rows(rw_a2[l]).astype(BF16)

    q = swc[:, 0:GW]
    kx = jnp.concatenate([swc[:, 256:320], swc[:, 256:320], swc[:, 320:384], swc[:, 320:384]], axis=1)
    vx = jnp.concatenate([swc[:, 384:448], swc[:, 384:448], swc[:, 448:512], swc[:, 448:512]], axis=1)
    w_swa = jnp.concatenate([q, kx, vx, _swap_halves(q, 32), _swap_halves(kx, 32)], axis=1)

    kr = mlc[:, 320:352]
    krp = _swap_halves(kr, 16)
    w_mla = jnp.concatenate([mlc[:, 0:192], zc(64), mlc[:, 192:320], zc(64), kr, zc(32), zc(64), krp, zc(32)], axis=1)

    wuq = mla_wuq[l]
    wukv = mla_wukv[l]
    wq, wqr, wk, wv = [], [], [], []
    zq = lambda r, n: jnp.zeros((r, n), F32)
    for h in range(NH):
        qh = wuq[:, 96 * h:96 * (h + 1)]
        a = jnp.concatenate([qh, zq(192, 32)], axis=1)
        ar = jnp.concatenate([zq(192, 64), _swap_halves(qh[:, 64:96], 16), zq(192, 32)], axis=1)
        wq.append(jnp.concatenate([a, zq(64, 128)], axis=0))
        wqr.append(jnp.concatenate([ar, zq(64, 128)], axis=0))
        wk.append(jnp.concatenate([wukv[:, 128 * h:128 * h + 64], zq(128, 64)], axis=1))
        wv.append(wukv[:, 128 * h + 64:128 * (h + 1)].T)
    row = lambda a: a.reshape(1, -1)
    return {
        "proj": jnp.concatenate([w_hg, w_rw, w_swa, w_mla], axis=1).astype(BF16),
        "rw": {"mu": mu, "w0": rw_w0[l], "w2": w2, "a0": row(rw_a0[l]), "a2": a2, "g2": rw_g2[l].astype(BF16),
               "k_k": row(rw_k_k[l]), "k_a": row(rw_k_a[l])},
        "rk": row(rw_r_k[l]), "lng": row(rw_ln_g[l]), "lnb": row(rw_ln_b[l]),
        "mla": {"qg": jnp.concatenate([row(mla_qnorm_g[l]), jnp.zeros((1, 64), F32)], axis=1),
                "kg": row(mla_kvnorm_g[l]),
                "wq": jnp.stack(wq).astype(BF16), "wqr": jnp.stack(wqr).astype(BF16),
                "wk": jnp.stack(wk).astype(BF16), "wv": jnp.stack(wv).astype(BF16)},
    }


def _rope_tables(T, lc):
    rows = T // GRID_W
    rowp = jnp.repeat(jnp.arange(rows, dtype=F32), GRID_W)
    colp = jnp.tile(jnp.arange(GRID_W, dtype=F32), rows)

    def table(rot_dim, width, lane0):
        nf = rot_dim // 4
        inv = ROPE_BASE ** (-jnp.arange(nf, dtype=F32) / nf)
        j = np.arange(rot_dim)
        ax = j // (2 * nf)
        fr = j % nf
        sign = np.where(j % (2 * nf) < nf, -1.0, 1.0).astype(np.float32)
        pos = jnp.stack([rowp, colp], axis=1)
        ang = pos[:, ax] * inv[fr][None, :]
        cos = jnp.concatenate([jnp.ones((lc, rot_dim), F32), jnp.cos(ang)], axis=0)
        sin = jnp.concatenate([jnp.zeros((lc, rot_dim), F32), jnp.sin(ang) * sign[None, :]], axis=0)
        reps = (width - lane0) // rot_dim if lane0 == 0 else 1
        cos = jnp.tile(cos, (1, reps))
        sin = jnp.tile(sin, (1, reps))
        return cos, sin

    cs_s, sn_s = table(HD, 2 * GW, 0)
    c32, s32 = table(MLA_ROPE, MLA_ROPE, 0)
    S = lc + T
    z = lambda n: jnp.zeros((S, n), F32)
    o = lambda n: jnp.ones((S, n), F32)
    cs_m = jnp.concatenate([z(64), c32, z(32)], axis=1)
    sn_m = jnp.concatenate([z(64), s32, z(32)], axis=1)
    cs_q = jnp.concatenate([o(64), c32, z(32)], axis=1)
    sn_q = jnp.concatenate([z(64), s32, z(32)], axis=1)
    return {"cs_s": cs_s, "sn_s": sn_s, "cs_m": cs_m, "sn_m": sn_m, "cs_q": cs_q, "sn_q": sn_q}


def kernel(x, c, ctx, c_ctx, w_ada, b_ada, norm1_g, norm2_g, w_in, hg_lb, hg_norm_g, rw_mu, rw_w0, rw_w2, rw_a0,
           rw_a2, rw_g2, rw_k_k, rw_k_a, rw_r_k, rw_ln_g, rw_ln_b, swa_sink, mla_qnorm_g, mla_wuq, mla_kvnorm_g,
           mla_wukv, w_out, ffn_w1, ffn_w3, ffn_w2, moe_router, moe_w1, moe_w3, moe_w2, final_norm_g):
    B, T, _ = x.shape
    lc = ctx.shape[1]
    S = lc + T
    L = w_in.shape[0]
    assert lc % TM == 0 and T % TM == 0 and lc % SWA_W == 0 and T % GRID_W == 0 and (B * S) % 1024 == 0 and (B * T) % 1024 == 0

    r8 = -(-(B + 1) // 8) * 8
    c8 = jnp.concatenate([c, c_ctx[None, :], jnp.zeros((r8 - B - 1, D), F32)], axis=0)
    mod_all = _ada(c8, w_ada, b_ada)
    tabs = _rope_tables(T, lc)
    xs = jnp.concatenate([ctx, x], axis=1)
    row = lambda a: a.reshape(1, -1)

    pend = None
    for l in range(L):
        w = _layer_weights(l, w_in, rw_mu, rw_w0, rw_w2, rw_a0, rw_a2, rw_g2, rw_k_k, rw_k_a, rw_r_k, rw_ln_g,
                           rw_ln_b, mla_qnorm_g, mla_wuq, mla_kvnorm_g, mla_wukv)
        mod = mod_all[l]
        xs, p_hg, p_rw, p_swa, p_mla = _proj(xs, pend, mod, row(norm1_g[l]), w["proj"], tabs, lc)

        hgo = _hgrn2(p_hg, hg_lb, l, lc)
        rwp = _rwprep(p_rw, w["rw"], lc)
        rwo = _rwscan(rwp, lc)
        sink8 = jnp.zeros((8, 128), F32).at[0:NH, :].set(jnp.broadcast_to(swa_sink[l][:, None], (NH, 128)))
        last = l == L - 1
        swo = _swa(p_swa, sink8, lc, last)
        mq, mk, mv = _mlaprep(p_mla, w["mla"], tabs)
        mlo = _mla(mq, mk, mv, lc, last)

        moe = l % 2 == 1
        router = None
        if moe:
            router = jnp.concatenate([moe_router[l // 2], jnp.zeros((D, 128 - N_EXPERTS), F32)], axis=1)
        outs = _outproj(xs, mod, row(norm2_g[l]), hgo, p_hg, row(hg_norm_g[l]), rwo, rwp, w["rk"], w["lng"],
                        w["lnb"], swo, mlo, w_out[l].astype(BF16), router, lc, last)
        xs, h2 = outs[0], outs[1]
        so = xs.shape[1]
        h2f = h2.reshape(B * so, D)
        if moe:
            e = l // 2
            fin = None
            if last and so == T and T % 1024 == 0:
                fin = (xs.reshape(B * T, D), mod, row(final_norm_g), T)
            y = _moe(h2f, outs[2].reshape(B * so, 128), moe_w1[e].astype(BF16), moe_w3[e].astype(BF16),
                     moe_w2[e].astype(BF16), fin)
            if fin is not None:
                return y.reshape(B, T, D)
        else:
            e = l // 2
            y = _ffn(h2f, ffn_w1[e].astype(BF16), ffn_w3[e].astype(BF16), ffn_w2[e].astype(BF16))
        pend = (y.reshape(B, so, D), mod)

    return _final(xs, pend[0], pend[1], row(final_norm_g), xs.shape[1] - T)
```

```python
import functools

import jax
import jax.numpy as jnp
import numpy as np
from jax import lax
from jax.experimental import pallas as pl
from jax.experimental.pallas import tpu as pltpu

F32 = jnp.float32
BF16 = jnp.bfloat16

D = 1024
GW = 256
HD = 64
NH = 4
GRID_W = 64
NORM_EPS = 1e-6
ROPE_BASE = 10000.0
HG_F_FLOOR = 1e-30
RW_LN_EPS = 64e-5
SWA_W = 128
MLA_Q_RANK = 192
MLA_KV_RANK = 128
MLA_NOPE = 64
MLA_ROPE = 32
D_FF = 2816
N_EXPERTS = 8
D_FF_EXPERT = 1408

TM = 256
CH = 64
SCAN_ROWS = 256
HB = 16
RW_BPS = 2
MLA_HPS = 4
MLA_VROWS = 80
LOG2E = 1.4426950408889634
MLA_KCHUNK = 256
MLA_SKEW = 8
VMEM_LIMIT = 56 * 1024 * 1024
RW_GRAM_MODE = "bf16"
RW_BASE_MODE = "bf16"
RW_MERGE_MODE = "bf16"
RW_STATE_MODE = "bf16"

_NN = (((1,), (0,)), ((), ()))
_NT = (((1,), (1,)), ((), ()))
_TN = (((0,), (0,)), ((), ()))


def _mm(a, b, dims=_NN, mode="bf16"):
    if mode == "bf16":
        return lax.dot_general(a.astype(BF16), b.astype(BF16), dims, preferred_element_type=F32)
    if mode == "x3":
        ah = a.astype(BF16)
        al = (a - ah.astype(F32)).astype(BF16)
        bh = b.astype(BF16)
        bl = (b - bh.astype(F32)).astype(BF16)
        d = lambda x, y: lax.dot_general(x, y, dims, preferred_element_type=F32)
        return d(ah, bh) + (d(ah, bl) + d(al, bh))
    return lax.dot_general(a, b, dims, precision=lax.Precision.HIGHEST, preferred_element_type=F32)


def _sigmoid(x):
    return 1.0 / (1.0 + jnp.exp(-x))


def _silu(x):
    return x / (1.0 + jnp.exp(-x))


def _iota(shape, dim):
    return lax.broadcasted_iota(jnp.int32, shape, dim)


def _head_block_mask(n):
    return (_iota((n, n), 0) // HD) == (_iota((n, n), 1) // HD)


def _running_sum(x, fwd, seg):
    n = x.shape[0]
    pos = _iota(x.shape, 0) % seg
    sh = 1
    while sh < seg:
        if fwd:
            x = x + jnp.where(pos >= sh, pltpu.roll(x, sh, 0), 0.0)
        else:
            x = x + jnp.where(pos < seg - sh, pltpu.roll(x, n - sh, 0), 0.0)
        sh *= 2
    return x


def _alternate(*stages, skew=0):
    live = list(enumerate(stages))
    rnd = 0
    while live:
        for item in list(live):
            i, g = item
            if rnd < i * skew:
                continue
            try:
                next(g)
            except StopIteration:
                live.remove(item)
        rnd += 1


def _cparams(sem):
    return pltpu.CompilerParams(dimension_semantics=sem, vmem_limit_bytes=VMEM_LIMIT)


def _ada_body(c_ref, w_ref, b_ref, o_ref):
    s = _silu(c_ref[...])
    o_ref[0] = _mm(s, w_ref[0]) + b_ref[0]


def _ada(c8, w_ada, b_ada):
    L = w_ada.shape[0]
    r8 = c8.shape[0]
    tn = 1536
    return pl.pallas_call(
        _ada_body,
        grid=(L, 6 * D // tn),
        in_specs=[
            pl.BlockSpec((r8, D), lambda l, j: (0, 0)),
            pl.BlockSpec((1, D, tn), lambda l, j: (l, 0, j)),
            pl.BlockSpec((1, 1, tn), lambda l, j: (l, 0, j)),
        ],
        out_specs=pl.BlockSpec((1, r8, tn), lambda l, j: (l, 0, j)),
        out_shape=jax.ShapeDtypeStruct((L, r8, 6 * D), F32),
        compiler_params=_cparams(("parallel", "parallel")),
        name="adaln",
    )(c8, w_ada, b_ada.reshape(L, 1, 6 * D))


def _proj_body(*refs, pending, lc_tiles, nb):
    if pending:
        (x_ref, y_ref, modp_ref, mod_ref, g_ref, wall, cs_s, sn_s, cs_m, sn_m,
         xo_ref, phg, prw, pswa, pmla) = refs
    else:
        (x_ref, mod_ref, g_ref, wall, cs_s, sn_s, cs_m, sn_m,
         phg, prw, pswa, pmla) = refs
    b = pl.program_id(0)
    i = pl.program_id(1)
    row = jnp.where(i < lc_tiles, nb, b)
    x = x_ref[0]
    if pending:
        gate = modp_ref[pl.ds(row, 1), 5 * D:6 * D]
        x = x + gate * y_ref[0]
        xo_ref[0] = x
    sh = mod_ref[pl.ds(row, 1), 0:D]
    sc = mod_ref[pl.ds(row, 1), D:2 * D]
    ms = jnp.mean(x * x, axis=-1, keepdims=True)
    h = (x * lax.rsqrt(ms + NORM_EPS) * g_ref[...]) * (1.0 + sc) + sh
    hb = h.astype(BF16)
    n1 = 5 * GW
    phg[0] = _mm(hb, wall[:, 0:n1])
    prw[0] = _mm(hb, wall[:, n1:2 * n1])
    s = _mm(hb, wall[:, 2 * n1:3 * n1])
    qk = s[:, 0:2 * GW] * cs_s[...] + s[:, 3 * GW:5 * GW] * sn_s[...]
    pswa[0, :, 0:2 * GW] = qk.astype(BF16)
    pswa[0, :, 2 * GW:3 * GW] = s[:, 2 * GW:3 * GW].astype(BF16)
    m = _mm(hb, wall[:, 3 * n1:3 * n1 + 640])
    pmla[0, :, 0:384] = m[:, 0:384]
    pmla[0, :, 384:512] = m[:, 384:512] * cs_m[...] + m[:, 512:640] * sn_m[...]


def _proj(x, pend, mod, g, w, tabs, lc):
    B, S, _ = x.shape
    nt = S // TM
    pending = pend is not None
    row_spec = lambda n: pl.BlockSpec((1, TM, n), lambda b, i: (b, i, 0))
    full = lambda a: pl.BlockSpec(a.shape, lambda b, i: (0,) * a.ndim)
    tab_spec = lambda a: pl.BlockSpec((TM, a.shape[1]), lambda b, i: (i, 0))
    ins = [x]
    in_specs = [row_spec(D)]
    if pending:
        ins += [pend[0], pend[1]]
        in_specs += [row_spec(D), full(pend[1])]
    ins += [mod, g, w, tabs["cs_s"], tabs["sn_s"], tabs["cs_m"], tabs["sn_m"]]
    in_specs += [full(mod), full(g), full(w),
                 tab_spec(tabs["cs_s"]), tab_spec(tabs["sn_s"]), tab_spec(tabs["cs_m"]), tab_spec(tabs["sn_m"])]
    out_shape = [jax.ShapeDtypeStruct((B, S, 5 * GW), F32), jax.ShapeDtypeStruct((B, S, 5 * GW), F32),
                 jax.ShapeDtypeStruct((B, S, 3 * GW), BF16), jax.ShapeDtypeStruct((B, S, 512), F32)]
    out_specs = [row_spec(5 * GW), row_spec(5 * GW), row_spec(3 * GW), row_spec(512)]
    if pending:
        out_shape = [jax.ShapeDtypeStruct((B, S, D), F32)] + out_shape
        out_specs = [row_spec(D)] + out_specs
    outs = pl.pallas_call(
        functools.partial(_proj_body, pending=pending, lc_tiles=lc // TM, nb=B),
        grid=(B, nt), in_specs=in_specs, out_specs=out_specs, out_shape=out_shape,
        compiler_params=_cparams(("parallel", "parallel")),
        name="proj",
    )(*ins)
    if pending:
        return tuple(outs)
    return (x,) + tuple(outs)


def _chunk_of_step(d, s, nc, nc_ctx):
    bwd = jnp.where(s < nc_ctx, nc_ctx - 1 - s, nc - 1 - (s - nc_ctx))
    return jnp.where(d == 0, s, bwd)


def _hg_direction(p_ref, lb, o_ref, st_ref, fwd, c0):
    p = p_ref[0, c0:c0 + CH, :]
    z = p[:, GW:2 * GW] if fwd else p[:, 2 * GW:3 * GW]
    sig = _sigmoid(z)
    f = lb + (1.0 - lb) * sig
    logf = jnp.log(jnp.maximum(f, HG_F_FLOOR))
    key = (1.0 - lb) * (1.0 - sig)
    qs = _silu(p[:, 0:GW])
    iv = p[:, 3 * GW:4 * GW]

    bcum = _running_sum(logf, fwd, HB)

    bdf = _head_block_mask(GW).astype(BF16)
    hm = (_iota((NH * HB, GW), 0) // HB) == (_iota((NH * HB, GW), 1) // HD)
    expand = lambda a: jnp.where(hm, jnp.concatenate([a] * NH, axis=0), 0.0)
    half = HB // 2
    nblk = CH // HB
    order = [(j if fwd else nblk - 1 - j) * HB for j in range(nblk)]
    blocks = [tuple(a[r0:r0 + HB] for a in (qs, key, iv, bcum)) for r0 in order]
    es = []
    spans = []
    for qb, kb, ib, bb in blocks:
        for sidx in range(HB):
            if fwd:
                lo, hi = (0, HB) if sidx < half else (half, HB)
            else:
                lo, hi = (0, half) if sidx < half else (0, HB)
            rows = _iota((hi - lo, 1), 0) + lo
            m = (rows >= sidx) if fwd else (rows <= sidx)
            dec = jnp.exp(jnp.where(m, bb[lo:hi] - bb[sidx:sidx + 1], -jnp.inf))
            es.append(qb[lo:hi] * kb[sidx:sidx + 1] * dec)
            spans.append((lo, hi))
        yield
    att = _mm(jnp.concatenate(es, axis=0), bdf)
    yield
    inblock = []
    outers = []
    off = 0
    for j, (qb, kb, ib, bb) in enumerate(blocks):
        top = jnp.zeros((half, GW), F32)
        bot = jnp.zeros((half, GW), F32)
        for sidx in range(HB):
            lo, hi = spans[j * HB + sidx]
            a = att[off:off + hi - lo] * ib[sidx:sidx + 1]
            off += hi - lo
            if hi - lo == HB:
                top = top + a[0:half]
                bot = bot + a[half:HB]
            elif lo == 0:
                top = top + a
            else:
                bot = bot + a
        inblock.append((top, bot))
        tot = bb[HB - 1:HB] if fwd else bb[0:1]
        outers.append((_mm(expand(ib), expand(kb * jnp.exp(tot - bb)), _TN), jnp.exp(tot)))
        yield
    for r0, (qb, kb, ib, bb), (top, bot), (outer, dtot) in zip(order, blocks, inblock, outers):
        st = st_ref[...]
        o_blk = _mm(qb * jnp.exp(bb), st, _NT)
        o_ref[0, c0 + r0:c0 + r0 + half, :] = o_blk[0:half] + top
        o_ref[0, c0 + r0 + half:c0 + r0 + HB, :] = o_blk[half:HB] + bot
        st_ref[...] = st * dtot + outer
        yield


def _hg_body(pf_ref, pb_ref, lbraw_ref, of_ref, ob_ref, stf_ref, stb_ref, *, layer):
    @pl.when(pl.program_id(1) == 0)
    def _():
        stf_ref[...] = jnp.zeros_like(stf_ref)
        stb_ref[...] = jnp.zeros_like(stb_ref)

    raw = lbraw_ref[...]
    nl = raw.shape[0]
    mx = raw[0]
    for j in range(1, nl):
        mx = jnp.maximum(mx, raw[j])
    ex = [jnp.exp(raw[j] - mx) for j in range(nl)]
    den = ex[0]
    for j in range(1, nl):
        den = den + ex[j]
    lb2 = jnp.zeros_like(mx)
    for j in range(1, layer + 1):
        lb2 = lb2 + ex[j] / den
    def scan(p_ref, lb, o_ref, st_ref, fwd):
        chunks = range(SCAN_ROWS // CH)
        for c in (chunks if fwd else reversed(chunks)):
            yield from _hg_direction(p_ref, lb, o_ref, st_ref, fwd, c * CH)

    _alternate(scan(pf_ref, lb2[0:1], of_ref, stf_ref, True), scan(pb_ref, lb2[1:2], ob_ref, stb_ref, False))


def _hgrn2(p_hg, hg_lb, layer, lc):
    B, S, _ = p_hg.shape
    rs = SCAN_ROWS
    nc = S // rs
    nc_ctx = lc // rs
    fmap = lambda b, s: (b, s, 0)
    bmap = lambda b, s: (b, _chunk_of_step(1, s, nc, nc_ctx), 0)
    return pl.pallas_call(
        functools.partial(_hg_body, layer=layer),
        grid=(B, nc),
        in_specs=[pl.BlockSpec((1, rs, 5 * GW), fmap), pl.BlockSpec((1, rs, 5 * GW), bmap),
                  pl.BlockSpec(hg_lb.shape, lambda b, s: (0, 0, 0))],
        out_specs=[pl.BlockSpec((1, rs, GW), fmap), pl.BlockSpec((1, rs, GW), bmap)],
        out_shape=[jax.ShapeDtypeStruct((B, S, GW), F32)] * 2,
        scratch_shapes=[pltpu.VMEM((GW, GW), F32)] * 2,
        compiler_params=_cparams(("parallel", "arbitrary")),
        name="hgrn2",
    )(p_hg, p_hg, hg_lb)


def _rwprep_body(cur_ref, prv_ref, nxt_ref, mu_ref, w0_ref, w2_ref, a0_ref, a2_ref, g2_ref, kk_ref, ka_ref,
                 o_ref, *, lc_tiles):
    i = pl.program_id(1)
    nt = pl.num_programs(1)
    first = (i == 0) | (i == lc_tiles)
    lastt = (i == lc_tiles - 1) | (i == nt - 1)
    cur = cur_ref[0]
    rowi = _iota((TM, 1), 0)
    before = jnp.where(first, 0.0, prv_ref[0, 7:8, :])
    after = jnp.where(lastt, 0.0, nxt_ref[0, 0:1, :])
    prev = jnp.where(rowi == 0, before, pltpu.roll(cur, 1, 0))
    nxt = jnp.where(rowi == TM - 1, after, pltpu.roll(cur, TM - 1, 0))
    sft = cur + mu_ref[0:1, :] * (prev - cur) + mu_ref[1:2, :] * (nxt - cur)
    r = sft[:, 0:GW]
    k = sft[:, GW:2 * GW]
    v = sft[:, 2 * GW:3 * GW]
    ywf = sft[:, 768:896]
    ywb = sft[:, 896:1024]
    ya = sft[:, 1024:1152]
    yg = sft[:, 1152:1280]
    c = float(np.exp(-0.5))
    lwf = -c * _sigmoid(w0_ref[0:1, :] + _mm(jnp.tanh(ywf), w2_ref[0]))
    lwb = -c * _sigmoid(w0_ref[1:2, :] + _mm(jnp.tanh(ywb), w2_ref[1]))
    a = _sigmoid(a0_ref[...] + _mm(ya, a2_ref[...]))
    kk0 = k * kk_ref[...]
    ss = _mm(kk0 * kk0, _head_block_mask(GW).astype(F32), mode="x3")
    kk = kk0 / jnp.maximum(jnp.sqrt(ss), 1e-12)
    kp = k * (1.0 + (a - 1.0) * ka_ref[...])
    g = _mm(_sigmoid(yg), g2_ref[...])
    o_ref[0, :, 0:GW] = r
    o_ref[0, :, GW:2 * GW] = lwf
    o_ref[0, :, 2 * GW:3 * GW] = lwb
    o_ref[0, :, 3 * GW:4 * GW] = kp
    o_ref[0, :, 4 * GW:5 * GW] = v
    o_ref[0, :, 5 * GW:6 * GW] = kk
    o_ref[0, :, 6 * GW:7 * GW] = kk * a
    o_ref[0, :, 7 * GW:8 * GW] = g


def _rwprep(p_rw, w, lc):
    B, S, n = p_rw.shape
    nt = S // TM
    t8 = TM // 8
    full = lambda a: pl.BlockSpec(a.shape, lambda b, i: (0,) * a.ndim)
    small = [w["mu"], w["w0"], w["w2"], w["a0"], w["a2"], w["g2"], w["k_k"], w["k_a"]]
    return pl.pallas_call(
        functools.partial(_rwprep_body, lc_tiles=lc // TM),
        grid=(B, nt),
        in_specs=[pl.BlockSpec((1, TM, n), lambda b, i: (b, i, 0)),
                  pl.BlockSpec((1, 8, n), lambda b, i: (b, jnp.maximum(i * t8 - 1, 0), 0)),
                  pl.BlockSpec((1, 8, n), lambda b, i: (b, jnp.minimum((i + 1) * t8, S // 8 - 1), 0))]
                 + [full(a) for a in small],
        out_specs=pl.BlockSpec((1, TM, 8 * GW), lambda b, i: (b, i, 0)),
        out_shape=jax.ShapeDtypeStruct((B, S, 8 * GW), F32),
        compiler_params=_cparams(("parallel", "parallel")),
        name="rwkv_prep",
    )(p_rw, p_rw, p_rw, *small)


def _rw_direction(x_ref, o_ref, h_ref, fwd, bi, c0):
    x = x_ref[bi, c0:c0 + CH, :]
    r = x[:, 0:GW]
    lw = x[:, GW:2 * GW] if fwd else x[:, 2 * GW:3 * GW]
    k = x[:, 3 * GW:4 * GW]
    v = x[:, 4 * GW:5 * GW]
    kk = x[:, 5 * GW:6 * GW]
    bb = x[:, 6 * GW:7 * GW]

    incl = _running_sum(lw, fwd, CH)
    tot = incl[CH - 1:CH] if fwd else incl[0:1]
    e_in = jnp.exp(incl)
    e_ni = jnp.exp(-incl)
    e_tt = jnp.exp(tot - incl)
    km = kk * jnp.exp(incl - lw)
    rt = r * e_in
    bh = bb * e_ni
    kh = k * e_ni
    bt = bb * e_tt
    kt = k * e_tt

    n = NH * CH
    hm = (_iota((n, GW), 0) // CH) == (_iota((n, GW), 1) // HD)
    tile4 = lambda a: jnp.concatenate([a] * NH, axis=0)
    expand = lambda a: jnp.where(hm, tile4(a), 0.0)
    km_e = expand(km)
    rt_e = expand(rt)
    v_e = expand(v)
    bt_e = expand(bt)
    kt_e = expand(kt)
    bh_t = tile4(bh)
    kh_t = tile4(kh)

    rr = _iota((n, n), 0)
    cc = _iota((n, n), 1)
    same = (rr // CH) == (cc // CH)
    rt_i = rr % CH
    ct_i = cc % CH
    before = same & ((ct_i < rt_i) if fwd else (ct_i > rt_i))
    upto = same & ((ct_i <= rt_i) if fwd else (ct_i >= rt_i))
    eye = rr == cc

    pg = RW_GRAM_MODE
    a_m = jnp.where(before, _mm(km_e, bh_t, _NT, pg), 0.0)
    yield

    blk = lambda m: (rr // m) == (cc // m)
    pb = RW_BASE_MODE
    nm = jnp.where(blk(8), -a_m, 0.0)
    x2 = _mm(nm, nm, mode=pb)
    b_m = jnp.where(before, _mm(km_e, kh_t, _NT, pg), 0.0)
    yield
    tm = jnp.where(eye, 1.0, 0.0) + nm
    tm = tm + _mm(tm, x2, mode=pb)
    x4 = _mm(x2, x2, mode=pb)
    yield
    tm = tm + _mm(tm, x4, mode=pb)
    mrb = jnp.where(upto, _mm(rt_e, bh_t, _NT, pg), 0.0)
    yield
    pm = RW_MERGE_MODE
    side = [lambda: jnp.where(upto, _mm(rt_e, kh_t, _NT, pg), 0.0), lambda: _mm(b_m, v_e, mode=pg),
            lambda: _mm(kt_e, v_e, _TN, pg)]
    done = []
    for m in (8, 16, 32):
        low = jnp.where(blk(2 * m) & (rr // m != cc // m), a_m, 0.0)
        y = _mm(low, tm, mode=pm)
        done.append(side.pop(0)())
        yield
        tm = tm - _mm(tm, y, mode=pm)
        yield
    mrk, bmv, ktv = done

    p1 = _mm(tm, km_e, mode=pg)
    q1 = _mm(tm, bmv, mode=pg)
    yield
    p2e = rt_e - _mm(mrb, p1, mode=pg)
    q2e = _mm(mrk, v_e, mode=pg) - _mm(mrb, q1, mode=pg)
    g_m = jnp.where(eye, jnp.exp(tot), 0.0) - _mm(bt_e, p1, _TN, pg)
    f_m = ktv - _mm(bt_e, q1, _TN, pg)
    yield
    fold = lambda a: a[0:CH] + a[CH:2 * CH] + a[2 * CH:3 * CH] + a[3 * CH:4 * CH]
    p2 = fold(p2e)
    q2 = fold(q2e)
    h = h_ref[bi]
    ps = RW_STATE_MODE
    o_ref[bi, c0:c0 + CH, :] = _mm(p2, h, mode=ps) + q2
    h_ref[bi] = _mm(g_m, h, mode=ps) + f_m


def _rw_body(xf_ref, xb_ref, of_ref, ob_ref, hf_ref, hb_ref):
    @pl.when(pl.program_id(1) == 0)
    def _():
        hf_ref[...] = jnp.zeros_like(hf_ref)
        hb_ref[...] = jnp.zeros_like(hb_ref)

    def scan(x_ref, o_ref, h_ref, fwd, bi):
        chunks = range(SCAN_ROWS // CH)
        for c in (chunks if fwd else reversed(chunks)):
            yield from _rw_direction(x_ref, o_ref, h_ref, fwd, bi, c * CH)

    chains = []
    for bi in range(xf_ref.shape[0]):
        chains.append(scan(xf_ref, of_ref, hf_ref, True, bi))
        chains.append(scan(xb_ref, ob_ref, hb_ref, False, bi))
    _alternate(*chains)


def _rwscan(prep, lc):
    B, S, n = prep.shape
    rs = SCAN_ROWS
    nc = S // rs
    nc_ctx = lc // rs
    bps = RW_BPS if B % RW_BPS == 0 else 1
    fmap = lambda b, s: (b, s, 0)
    bmap = lambda b, s: (b, _chunk_of_step(1, s, nc, nc_ctx), 0)
    return pl.pallas_call(
        _rw_body,
        grid=(B // bps, nc),
        in_specs=[pl.BlockSpec((bps, rs, n), fmap), pl.BlockSpec((bps, rs, n), bmap)],
        out_specs=[pl.BlockSpec((bps, rs, GW), fmap), pl.BlockSpec((bps, rs, GW), bmap)],
        out_shape=[jax.ShapeDtypeStruct((B, S, GW), F32)] * 2,
        scratch_shapes=[pltpu.VMEM((bps, GW, GW), F32)] * 2,
        compiler_params=_cparams(("parallel", "arbitrary")),
        name="rwkv_scan",
    )(prep, prep)


def _swa_block(q, kloc, vloc, kc, vc, sink_ref, o_ref, r0, j, lc_blocks, nbl):
    w = SWA_W
    is_lat = j >= lc_blocks
    nblk = j - lc_blocks
    lane_h = _iota((w, GW), 1) // HD
    qs = jnp.concatenate([jnp.where(lane_h == h, q, jnp.zeros_like(q)) for h in range(NH)], axis=0)
    scale = HD ** -0.5
    s_loc = _mm(qs, kloc, _NT) * scale
    s_ctx = _mm(qs, kc, _NT) * scale
    yield
    qi = _iota((NH * w, 3 * w), 0) % w
    col = _iota((NH * w, 3 * w), 1)
    seg = col // w
    kj = col % w
    valid = ((seg == 0) & (nblk >= 1) & (kj >= qi)) | (seg == 1) | ((seg == 2) & (nblk < nbl - 1) & (kj <= qi))
    valid = valid & is_lat
    s_loc = jnp.where(valid, s_loc, -jnp.inf)
    hrow = _iota((NH * w, 1), 0) // w
    sink = jnp.zeros((NH * w, 1), F32)
    for h in range(NH):
        sink = jnp.where(hrow == h, sink_ref[h:h + 1, 0:1], sink)
    m = jnp.maximum(jnp.maximum(jnp.max(s_loc, axis=-1, keepdims=True), jnp.max(s_ctx, axis=-1, keepdims=True)), sink)
    p_loc = jnp.exp(s_loc - m)
    p_ctx = jnp.exp(s_ctx - m)
    den = jnp.sum(p_loc, axis=-1, keepdims=True) + jnp.sum(p_ctx, axis=-1, keepdims=True) + jnp.exp(sink - m)
    yield
    of = (_mm(p_loc, vloc) + _mm(p_ctx, vc)) / den
    out = jnp.zeros((w, GW), F32)
    for h in range(NH):
        out = out + jnp.where(lane_h == h, of[h * w:(h + 1) * w], 0.0)
    o_ref[0, r0:r0 + w, :] = out
    yield


def _swa_body(pair_ref, prv_ref, nxt_ref, ctx_ref, sink_ref, o_ref, *, lc_blocks, nbl, i0):
    i = pl.program_id(1) + i0
    w = SWA_W
    kcol = slice(GW, 2 * GW)
    vcol = slice(2 * GW, 3 * GW)
    ka, kb = pair_ref[0, 0:w, kcol], pair_ref[0, w:2 * w, kcol]
    va, vb = pair_ref[0, 0:w, vcol], pair_ref[0, w:2 * w, vcol]
    kc = ctx_ref[0, :, kcol]
    vc = ctx_ref[0, :, vcol]
    cat = lambda *a: jnp.concatenate(a, axis=0)
    first = _swa_block(pair_ref[0, 0:w, 0:GW], cat(prv_ref[0, :, kcol], ka, kb), cat(prv_ref[0, :, vcol], va, vb),
                       kc, vc, sink_ref, o_ref, 0, 2 * i, lc_blocks, nbl)
    second = _swa_block(pair_ref[0, w:2 * w, 0:GW], cat(ka, kb, nxt_ref[0, :, kcol]), cat(va, vb, nxt_ref[0, :, vcol]),
                        kc, vc, sink_ref, o_ref, w, 2 * i + 1, lc_blocks, nbl)
    _alternate(first, second, skew=1)


def _swa(p_swa, sink8, lc, skip_ctx):
    B, S, n = p_swa.shape
    w = SWA_W
    nb = S // w
    lcb = lc // w
    assert nb % 2 == 0 and lcb % 2 == 0
    i0 = lcb // 2 if skip_ctx else 0
    return pl.pallas_call(
        functools.partial(_swa_body, lc_blocks=lcb, nbl=nb - lcb, i0=i0),
        grid=(B, nb // 2 - i0),
        in_specs=[pl.BlockSpec((1, 2 * w, n), lambda b, i: (b, i + i0, 0)),
                  pl.BlockSpec((1, w, n), lambda b, i: (b, jnp.maximum(2 * (i + i0) - 1, 0), 0)),
                  pl.BlockSpec((1, w, n), lambda b, i: (b, jnp.minimum(2 * (i + i0) + 2, nb - 1), 0)),
                  pl.BlockSpec((1, lc, n), lambda b, i: (b, 0, 0)),
                  pl.BlockSpec(sink8.shape, lambda b, i: (0, 0))],
        out_specs=pl.BlockSpec((1, 2 * w, GW), lambda b, i: (b, i, 0)),
        out_shape=jax.ShapeDtypeStruct((B, S - 2 * w * i0, GW), F32),
        compiler_params=_cparams(("parallel", "parallel")),
        name="swa",
    )(p_swa, p_swa, p_swa, p_swa, sink8)


def _mlaprep_body(p_ref, qg_ref, kg_ref, wq, wqr, wk, wv, cs_ref, sn_ref, q_ref, k_ref, v_ref):
    p = p_ref[0]
    cq = p[:, 0:256]
    ckv = p[:, 256:384]
    krs = p[:, 384:512]
    qn = cq * lax.rsqrt(jnp.sum(cq * cq, axis=-1, keepdims=True) / MLA_Q_RANK + NORM_EPS) * qg_ref[...]
    kvn = ckv * lax.rsqrt(jnp.mean(ckv * ckv, axis=-1, keepdims=True) + NORM_EPS) * kg_ref[...]
    scale = (MLA_NOPE + MLA_ROPE) ** -0.5 * LOG2E
    cs = cs_ref[...]
    sn = sn_ref[...]
    kvn_t = kvn.T.astype(BF16)
    ones_rows = (_iota((MLA_VROWS - HD, TM), 0) == 0).astype(BF16)
    for h in range(NH):
        qh = _mm(qn, wq[h]) * cs + _mm(qn, wqr[h]) * sn
        q_ref[0, h] = (qh * scale).T.astype(BF16)
        k_ref[0, h] = (_mm(kvn, wk[h]) + krs).astype(BF16)
        v_ref[0, h, 0:HD, :] = _mm(wv[h], kvn_t).astype(BF16)
        v_ref[0, h, HD:MLA_VROWS, :] = ones_rows


def _mlaprep(p_mla, w, tabs):
    B, S, n = p_mla.shape
    nt = S // TM
    full = lambda a: pl.BlockSpec(a.shape, lambda b, i: (0,) * a.ndim)
    ws = [w["qg"], w["kg"], w["wq"], w["wqr"], w["wk"], w["wv"]]
    hspec = lambda c: pl.BlockSpec((1, NH, TM, c), lambda b, i: (b, 0, i, 0))
    return pl.pallas_call(
        _mlaprep_body,
        grid=(B, nt),
        in_specs=[pl.BlockSpec((1, TM, n), lambda b, i: (b, i, 0))] + [full(a) for a in ws]
                 + [pl.BlockSpec((TM, 128), lambda b, i: (i, 0))] * 2,
        out_specs=[pl.BlockSpec((1, NH, 128, TM), lambda b, i: (b, 0, 0, i)), hspec(128),
                   pl.BlockSpec((1, NH, MLA_VROWS, TM), lambda b, i: (b, 0, 0, i))],
        out_shape=[jax.ShapeDtypeStruct((B, NH, 128, S), BF16), jax.ShapeDtypeStruct((B, NH, S, 128), BF16),
                   jax.ShapeDtypeStruct((B, NH, MLA_VROWS, S), BF16)],
        compiler_params=_cparams(("parallel", "parallel")),
        name="mla_prep",
    )(p_mla, *ws, tabs["cs_q"], tabs["sn_q"])


def _mla_body(q_ref, k_ref, v_ref, o_ref, *, lc, t0):
    i = pl.program_id(2) + t0
    ctx_q = i * TM < lc
    kc = MLA_KCHUNK
    nk = k_ref.shape[2] // kc

    def head(h):
        q = q_ref[0, h]
        ss = []
        m = None
        for c in range(nk):
            s = _mm(k_ref[0, h, c * kc:(c + 1) * kc, :], q)
            if c * kc >= lc and t0 * TM < lc:
                s = jnp.where(ctx_q, -jnp.inf, s)
            ss.append(s)
            cm = jnp.max(s, axis=0, keepdims=True)
            m = cm if m is None else jnp.maximum(m, cm)
            yield
        acc = None
        for c in range(nk):
            p = jnp.exp2(ss[c] - m)
            pv = _mm(v_ref[0, h, :, c * kc:(c + 1) * kc], p)
            acc = pv if acc is None else acc + pv
            yield
        o_ref[0, h] = acc[0:HD] / acc[HD:HD + 1]

    _alternate(*[head(h) for h in range(MLA_HPS)], skew=MLA_SKEW)


def _mla(q, k, v, lc, skip_ctx):
    B, H, S, _ = k.shape
    nt = S // TM
    hp = MLA_HPS
    t0 = lc // TM if skip_ctx else 0
    return pl.pallas_call(
        functools.partial(_mla_body, lc=lc, t0=t0),
        grid=(B, H // hp, nt - t0),
        in_specs=[pl.BlockSpec((1, hp, 128, TM), lambda b, h, i: (b, h, 0, i + t0)),
                  pl.BlockSpec((1, hp, S, 128), lambda b, h, i: (b, h, 0, 0)),
                  pl.BlockSpec((1, hp, MLA_VROWS, S), lambda b, h, i: (b, h, 0, 0))],
        out_specs=pl.BlockSpec((1, hp, HD, TM), lambda b, h, i: (b, h, 0, i)),
        out_shape=jax.ShapeDtypeStruct((B, H, HD, S - t0 * TM), F32),
        compiler_params=_cparams(("parallel", "parallel", "parallel")),
        name="mla_attn",
    )(q, k, v)


def _outproj_body(*refs, lc_tiles, nb, moe, t0):
    if moe:
        (x_ref, mod_ref, n2_ref, hgf_ref, hgb_ref, hgg_ref, hgn_ref, rwf_ref, rwb_ref, rwp_ref, rk_ref, lng_ref, lnb_ref,
         swo_ref, mlo_ref, wout_ref, rt_ref, xo_ref, h2_ref, lg_ref) = refs
    else:
        (x_ref, mod_ref, n2_ref, hgf_ref, hgb_ref, hgg_ref, hgn_ref, rwf_ref, rwb_ref, rwp_ref, rk_ref, lng_ref, lnb_ref,
         swo_ref, mlo_ref, wout_ref, xo_ref, h2_ref) = refs
    b = pl.program_id(0)
    i = pl.program_id(1) + t0
    row = jnp.where(i < lc_tiles, nb, b)
    bd = _head_block_mask(GW).astype(BF16)

    def hsum(a):
        hi = a.astype(BF16)
        lo = (a - hi.astype(F32)).astype(BF16)
        return _mm(hi, bd) + _mm(lo, bd)

    o = hgf_ref[0] + hgb_ref[0]
    hg = o * lax.rsqrt(hsum(o * o) / HD + NORM_EPS) * hgn_ref[...] * _silu(hgg_ref[0])

    o = rwf_ref[0] + rwb_ref[0]
    c = o - hsum(o) / HD
    on = c * lax.rsqrt(hsum(c * c) / HD + RW_LN_EPS) * lng_ref[...] + lnb_ref[...]
    r = rwp_ref[0, :, 0:GW]
    kp = rwp_ref[0, :, 3 * GW:4 * GW]
    v = rwp_ref[0, :, 4 * GW:5 * GW]
    g = rwp_ref[0, :, 7 * GW:8 * GW]
    rw = (on + hsum(r * kp * rk_ref[...]) * v) * g

    acc = _mm(hg, wout_ref[0:GW, :]) + _mm(rw, wout_ref[GW:2 * GW, :]) + _mm(swo_ref[0], wout_ref[2 * GW:3 * GW, :])
    for h in range(NH):
        acc = acc + _mm(mlo_ref[0, h], wout_ref[3 * GW + h * HD:3 * GW + (h + 1) * HD, :], _TN)

    x = x_ref[0] + mod_ref[pl.ds(row, 1), 2 * D:3 * D] * acc
    xo_ref[0] = x
    sh = mod_ref[pl.ds(row, 1), 3 * D:4 * D]
    sc = mod_ref[pl.ds(row, 1), 4 * D:5 * D]
    ms = jnp.mean(x * x, axis=-1, keepdims=True)
    h2 = (x * lax.rsqrt(ms + NORM_EPS) * n2_ref[...]) * (1.0 + sc) + sh
    h2_ref[0] = h2.astype(BF16)
    if moe:
        lg_ref[0] = _mm(h2, rt_ref[...], mode="x3")


def _outproj(x, mod, n2g, hgo, p_hg, hgn, rwo, rwp, rk, lng, lnb, swo, mlo, wout, router, lc, skip_ctx):
    B, S, _ = x.shape
    nt = S // TM
    moe = router is not None
    t0 = lc // TM if skip_ctx else 0
    so = S - t0 * TM
    row = lambda n: pl.BlockSpec((1, TM, n), lambda b, i: (b, i + t0, 0))
    lat = lambda n: pl.BlockSpec((1, TM, n), lambda b, i: (b, i, 0))
    full = lambda a: pl.BlockSpec(a.shape, lambda b, i: (0,) * a.ndim)
    ins = [x, mod, n2g, hgo[0], hgo[1], p_hg, hgn, rwo[0], rwo[1], rwp, rk, lng, lnb, swo, mlo, wout]
    in_specs = [row(D), full(mod), full(n2g), row(GW), row(GW),
                pl.BlockSpec((1, TM, GW), lambda b, i: (b, i + t0, 4)), full(hgn),
                row(GW), row(GW), row(8 * GW), full(rk), full(lng), full(lnb), lat(GW),
                pl.BlockSpec((1, NH, HD, TM), lambda b, i: (b, 0, 0, i)), full(wout)]
    out_shape = [jax.ShapeDtypeStruct((B, so, D), F32), jax.ShapeDtypeStruct((B, so, D), BF16)]
    out_specs = [lat(D), lat(D)]
    if moe:
        ins.append(router)
        in_specs.append(full(router))
        out_shape.append(jax.ShapeDtypeStruct((B, so, 128), F32))
        out_specs.append(lat(128))
    return pl.pallas_call(
        functools.partial(_outproj_body, lc_tiles=lc // TM, nb=B, moe=moe, t0=t0),
        grid=(B, nt - t0), in_specs=in_specs, out_specs=out_specs, out_shape=out_shape,
        compiler_params=_cparams(("parallel", "parallel")),
        name="outproj",
    )(*ins)


def _ffn_body(h_ref, w1_ref, w3_ref, w2_ref, o_ref, acc_ref):
    j = pl.program_id(1)

    @pl.when(j == 0)
    def _():
        acc_ref[...] = jnp.zeros_like(acc_ref)

    h = h_ref[...]
    a = _mm(h, w1_ref[...])
    z = _silu(a) * _mm(h, w3_ref[...])
    acc_ref[...] += _mm(z, w2_ref[...])

    @pl.when(j == pl.num_programs(1) - 1)
    def _():
        o_ref[...] = acc_ref[...]


def _ffn(h2, w1, w3, w2):
    M = h2.shape[0]
    tm = 1024
    tf = D_FF // 2
    return pl.pallas_call(
        _ffn_body,
        grid=(M // tm, D_FF // tf),
        in_specs=[pl.BlockSpec((tm, D), lambda i, j: (i, 0)),
                  pl.BlockSpec((D, tf), lambda i, j: (0, j)),
                  pl.BlockSpec((D, tf), lambda i, j: (0, j)),
                  pl.BlockSpec((tf, D), lambda i, j: (j, 0))],
        out_specs=pl.BlockSpec((tm, D), lambda i, j: (i, 0)),
        out_shape=jax.ShapeDtypeStruct((M, D), F32),
        scratch_shapes=[pltpu.VMEM((tm, D), F32)],
        compiler_params=_cparams(("parallel", "arbitrary")),
        name="ffn",
    )(h2, w1, w3, w2)


def _moe_body(*refs, final_rows):
    if final_rows:
        h_ref, lg_ref, w1_ref, w3_ref, w2_ref, x_ref, mod_ref, g_ref, o_ref, acc_ref, comb_ref = refs
    else:
        h_ref, lg_ref, w1_ref, w3_ref, w2_ref, o_ref, acc_ref, comb_ref = refs
    e = pl.program_id(1)
    tile = pl.program_id(0)
    lane = _iota(comb_ref.shape, 1)

    @pl.when(e == 0)
    def _():
        acc_ref[...] = jnp.zeros_like(acc_ref)
        lg = jnp.where(lane < N_EXPERTS, lg_ref[...], -jnp.inf)
        v1 = jnp.max(lg, axis=-1, keepdims=True)
        i1 = jnp.min(jnp.where(lg == v1, lane, 128), axis=-1, keepdims=True)
        lg2 = jnp.where(lane == i1, -jnp.inf, lg)
        v2 = jnp.max(lg2, axis=-1, keepdims=True)
        i2 = jnp.min(jnp.where(lg2 == v2, lane, 128), axis=-1, keepdims=True)
        ex = jnp.exp(v2 - v1)
        comb_ref[...] = jnp.where(lane == i1, 1.0 / (1.0 + ex), 0.0) + jnp.where(lane == i2, ex / (1.0 + ex), 0.0)

    ce = jnp.sum(jnp.where(lane == e, comb_ref[...], 0.0), axis=-1, keepdims=True)
    h = h_ref[...]
    a = _mm(h, w1_ref[0])
    z = _silu(a) * _mm(h, w3_ref[0])
    acc_ref[...] += ce * _mm(z, w2_ref[0])

    @pl.when(e == pl.num_programs(1) - 1)
    def _():
        if final_rows:
            b = tile // (final_rows // acc_ref.shape[0])
            x = x_ref[...] + mod_ref[pl.ds(b, 1), 5 * D:6 * D] * acc_ref[...]
            ms = jnp.mean(x * x, axis=-1, keepdims=True)
            o_ref[...] = x * lax.rsqrt(ms + NORM_EPS) * g_ref[...]
        else:
            o_ref[...] = acc_ref[...]


def _moe(h2, logits, w1, w3, w2, fin=None):
    M = h2.shape[0]
    tm = 1024
    f = w1.shape[2]
    ins = [h2, logits, w1, w3, w2]
    in_specs = [pl.BlockSpec((tm, D), lambda i, e: (i, 0)),
                pl.BlockSpec((tm, 128), lambda i, e: (i, 0)),
                pl.BlockSpec((1, D, f), lambda i, e: (e, 0, 0)),
                pl.BlockSpec((1, D, f), lambda i, e: (e, 0, 0)),
                pl.BlockSpec((1, f, D), lambda i, e: (e, 0, 0))]
    final_rows = 0
    if fin is not None:
        x, mod, g, final_rows = fin
        assert final_rows % tm == 0
        ins += [x, mod, g]
        in_specs += [pl.BlockSpec((tm, D), lambda i, e: (i, 0)), pl.BlockSpec(mod.shape, lambda i, e: (0, 0)),
                     pl.BlockSpec(g.shape, lambda i, e: (0, 0))]
    return pl.pallas_call(
        functools.partial(_moe_body, final_rows=final_rows),
        grid=(M // tm, N_EXPERTS),
        in_specs=in_specs,
        out_specs=pl.BlockSpec((tm, D), lambda i, e: (i, 0)),
        out_shape=jax.ShapeDtypeStruct((M, D), F32),
        scratch_shapes=[pltpu.VMEM((tm, D), F32), pltpu.VMEM((tm, 128), F32)],
        compiler_params=_cparams(("parallel", "arbitrary")),
        name="moe",
    )(*ins)


def _final_body(x_ref, y_ref, mod_ref, g_ref, o_ref):
    b = pl.program_id(0)
    x = x_ref[0] + mod_ref[pl.ds(b, 1), 5 * D:6 * D] * y_ref[0]
    ms = jnp.mean(x * x, axis=-1, keepdims=True)
    o_ref[0] = x * lax.rsqrt(ms + NORM_EPS) * g_ref[...]


def _final(x, y, mod, g, lc):
    B, S, _ = x.shape
    T = S - lc
    off = lc // TM
    return pl.pallas_call(
        _final_body,
        grid=(B, T // TM),
        in_specs=[pl.BlockSpec((1, TM, D), lambda b, i: (b, i + off, 0)),
                  pl.BlockSpec((1, TM, D), lambda b, i: (b, i + off, 0)),
                  pl.BlockSpec(mod.shape, lambda b, i: (0, 0)),
                  pl.BlockSpec(g.shape, lambda b, i: (0, 0))],
        out_specs=pl.BlockSpec((1, TM, D), lambda b, i: (b, i, 0)),
        out_shape=jax.ShapeDtypeStruct((B, T, D), F32),
        compiler_params=_cparams(("parallel", "parallel")),
        name="final_norm",
    )(x, y, mod, g)


def _swap_halves(a, group):
    n = a.shape[-1]
    return a.reshape(a.shape[:-1] + (n // group, 2, group // 2))[..., ::-1, :].reshape(a.shape)


def _layer_weights(l, w_in, rw_mu, rw_w0, rw_w2, rw_a0, rw_a2, rw_g2, rw_k_k, rw_k_a, rw_r_k, rw_ln_g, rw_ln_b,
                   mla_qnorm_g, mla_wuq, mla_kvnorm_g, mla_wukv):
    wi = w_in[l]
    zc = lambda n: jnp.zeros((D, n), F32)
    o = 0
    w_hg = wi[:, o:o + 5 * GW]
    o += 5 * GW
    rwc = wi[:, o:o + 1088]
    o += 1088
    swc = wi[:, o:o + 512]
    o += 512
    mlc = wi[:, o:o + 352]

    def pad_rw(a, axis):
        z = lambda n: jnp.zeros(a.shape[:axis] + (n,) + a.shape[axis + 1:], a.dtype)
        sl = lambda s, e: lax.slice_in_dim(a, s, e, axis=axis)
        return jnp.concatenate([sl(0, 768), sl(768, 832), z(64), sl(832, 896), z(64), sl(896, 960), z(64),
                                sl(960, 1088)], axis=axis)

    w_rw = pad_rw(rwc, 1)
    mu = pad_rw(rw_mu[l], 1)
    pad_rows = lambda a: jnp.concatenate([a, jnp.zeros((64, a.shape[1]), a.dtype)], axis=0)
    w2 = jnp.stack([pad_rows(rw_w2[l, 0]), pad_rows(rw_w2[l, 1])]).astype(BF16)
    a2 = pad_rows(rw_a2[l]).astype(BF16)

    q = swc[:, 0:GW]
    kx = jnp.concatenate([swc[:, 256:320], swc[:, 256:320], swc[:, 320:384], swc[:, 320:384]], axis=1)
    vx = jnp.concatenate([swc[:, 384:448], swc[:, 384:448], swc[:, 448:512], swc[:, 448:512]], axis=1)
    w_swa = jnp.concatenate([q, kx, vx, _swap_halves(q, 32), _swap_halves(kx, 32)], axis=1)

    kr = mlc[:, 320:352]
    krp = _swap_halves(kr, 16)
    w_mla = jnp.concatenate([mlc[:, 0:192], zc(64), mlc[:, 192:320], zc(64), kr, zc(32), zc(64), krp, zc(32)], axis=1)

    wuq = mla_wuq[l]
    wukv = mla_wukv[l]
    wq, wqr, wk, wv = [], [], [], []
    zq = lambda r, n: jnp.zeros((r, n), F32)
    for h in range(NH):
        qh = wuq[:, 96 * h:96 * (h + 1)]
        a = jnp.concatenate([qh, zq(192, 32)], axis=1)
        ar = jnp.concatenate([zq(192, 64), _swap_halves(qh[:, 64:96], 16), zq(192, 32)], axis=1)
        wq.append(jnp.concatenate([a, zq(64, 128)], axis=0))
        wqr.append(jnp.concatenate([ar, zq(64, 128)], axis=0))
        wk.append(jnp.concatenate([wukv[:, 128 * h:128 * h + 64], zq(128, 64)], axis=1))
        wv.append(wukv[:, 128 * h + 64:128 * (h + 1)].T)
    row = lambda a: a.reshape(1, -1)
    return {
        "proj": jnp.concatenate([w_hg, w_rw, w_swa, w_mla], axis=1).astype(BF16),
        "rw": {"mu": mu, "w0": rw_w0[l], "w2": w2, "a0": row(rw_a0[l]), "a2": a2, "g2": rw_g2[l].astype(BF16),
               "k_k": row(rw_k_k[l]), "k_a": row(rw_k_a[l])},
        "rk": row(rw_r_k[l]), "lng": row(rw_ln_g[l]), "lnb": row(rw_ln_b[l]),
        "mla": {"qg": jnp.concatenate([row(mla_qnorm_g[l]), jnp.zeros((1, 64), F32)], axis=1),
                "kg": row(mla_kvnorm_g[l]),
                "wq": jnp.stack(wq).astype(BF16), "wqr": jnp.stack(wqr).astype(BF16),
                "wk": jnp.stack(wk).astype(BF16), "wv": jnp.stack(wv).astype(BF16)},
    }


def _rope_tables(T, lc):
    rows = T // GRID_W
    rowp = jnp.repeat(jnp.arange(rows, dtype=F32), GRID_W)
    colp = jnp.tile(jnp.arange(GRID_W, dtype=F32), rows)

    def table(rot_dim, width, lane0):
        nf = rot_dim // 4
        inv = ROPE_BASE ** (-jnp.arange(nf, dtype=F32) / nf)
        j = np.arange(rot_dim)
        ax = j // (2 * nf)
        fr = j % nf
        sign = np.where(j % (2 * nf) < nf, -1.0, 1.0).astype(np.float32)
        pos = jnp.stack([rowp, colp], axis=1)
        ang = pos[:, ax] * inv[fr][None, :]
        cos = jnp.concatenate([jnp.ones((lc, rot_dim), F32), jnp.cos(ang)], axis=0)
        sin = jnp.concatenate([jnp.zeros((lc, rot_dim), F32), jnp.sin(ang) * sign[None, :]], axis=0)
        reps = (width - lane0) // rot_dim if lane0 == 0 else 1
        cos = jnp.tile(cos, (1, reps))
        sin = jnp.tile(sin, (1, reps))
        return cos, sin

    cs_s, sn_s = table(HD, 2 * GW, 0)
    c32, s32 = table(MLA_ROPE, MLA_ROPE, 0)
    S = lc + T
    z = lambda n: jnp.zeros((S, n), F32)
    o = lambda n: jnp.ones((S, n), F32)
    cs_m = jnp.concatenate([z(64), c32, z(32)], axis=1)
    sn_m = jnp.concatenate([z(64), s32, z(32)], axis=1)
    cs_q = jnp.concatenate([o(64), c32, z(32)], axis=1)
    sn_q = jnp.concatenate([z(64), s32, z(32)], axis=1)
    return {"cs_s": cs_s, "sn_s": sn_s, "cs_m": cs_m, "sn_m": sn_m, "cs_q": cs_q, "sn_q": sn_q}


def kernel(x, c, ctx, c_ctx, w_ada, b_ada, norm1_g, norm2_g, w_in, hg_lb, hg_norm_g, rw_mu, rw_w0, rw_w2, rw_a0,
           rw_a2, rw_g2, rw_k_k, rw_k_a, rw_r_k, rw_ln_g, rw_ln_b, swa_sink, mla_qnorm_g, mla_wuq, mla_kvnorm_g,
           mla_wukv, w_out, ffn_w1, ffn_w3, ffn_w2, moe_router, moe_w1, moe_w3, moe_w2, final_norm_g):
    B, T, _ = x.shape
    lc = ctx.shape[1]
    S = lc + T
    L = w_in.shape[0]
    assert lc % TM == 0 and T % TM == 0 and lc % SWA_W == 0 and T % GRID_W == 0 and (B * S) % 1024 == 0 and (B * T) % 1024 == 0

    r8 = -(-(B + 1) // 8) * 8
    c8 = jnp.concatenate([c, c_ctx[None, :], jnp.zeros((r8 - B - 1, D), F32)], axis=0)
    mod_all = _ada(c8, w_ada, b_ada)
    tabs = _rope_tables(T, lc)
    xs = jnp.concatenate([ctx, x], axis=1)
    row = lambda a: a.reshape(1, -1)

    pend = None
    for l in range(L):
        w = _layer_weights(l, w_in, rw_mu, rw_w0, rw_w2, rw_a0, rw_a2, rw_g2, rw_k_k, rw_k_a, rw_r_k, rw_ln_g,
                           rw_ln_b, mla_qnorm_g, mla_wuq, mla_kvnorm_g, mla_wukv)
        mod = mod_all[l]
        xs, p_hg, p_rw, p_swa, p_mla = _proj(xs, pend, mod, row(norm1_g[l]), w["proj"], tabs, lc)

        hgo = _hgrn2(p_hg, hg_lb, l, lc)
        rwp = _rwprep(p_rw, w["rw"], lc)
        rwo = _rwscan(rwp, lc)
        sink8 = jnp.zeros((8, 128), F32).at[0:NH, :].set(jnp.broadcast_to(swa_sink[l][:, None], (NH, 128)))
        last = l == L - 1
        swo = _swa(p_swa, sink8, lc, last)
        mq, mk, mv = _mlaprep(p_mla, w["mla"], tabs)
        mlo = _mla(mq, mk, mv, lc, last)

        moe = l % 2 == 1
        router = None
        if moe:
            router = jnp.concatenate([moe_router[l // 2], jnp.zeros((D, 128 - N_EXPERTS), F32)], axis=1)
        outs = _outproj(xs, mod, row(norm2_g[l]), hgo, p_hg, row(hg_norm_g[l]), rwo, rwp, w["rk"], w["lng"],
                        w["lnb"], swo, mlo, w_out[l].astype(BF16), router, lc, last)
        xs, h2 = outs[0], outs[1]
        so = xs.shape[1]
        h2f = h2.reshape(B * so, D)
        if moe:
            e = l // 2
            fin = None
            if last and so == T and T % 1024 == 0:
                fin = (xs.reshape(B * T, D), mod, row(final_norm_g), T)
            y = _moe(h2f, outs[2].reshape(B * so, 128), moe_w1[e].astype(BF16), moe_w3[e].astype(BF16),
                     moe_w2[e].astype(BF16), fin)
            if fin is not None:
                return y.reshape(B, T, D)
        else:
            e = l // 2
            y = _ffn(h2f, ffn_w1[e].astype(BF16), ffn_w3[e].astype(BF16), ffn_w2[e].astype(BF16))
        pend = (y.reshape(B, so, D), mod)

    return _final(xs, pend[0], pend[1], row(final_norm_g), xs.shape[1] - T)
```

```python
import functools

import jax
import jax.numpy as jnp
import numpy as np
from jax import lax
from jax.experimental import pallas as pl
from jax.experimental.pallas import tpu as pltpu

F32 = jnp.float32
BF16 = jnp.bfloat16

D = 1024
GW = 256
HD = 64
NH = 4
GRID_W = 64
NORM_EPS = 1e-6
ROPE_BASE = 10000.0
HG_F_FLOOR = 1e-30
RW_LN_EPS = 64e-5
SWA_W = 128
MLA_Q_RANK = 192
MLA_KV_RANK = 128
MLA_NOPE = 64
MLA_ROPE = 32
D_FF = 2816
N_EXPERTS = 8
D_FF_EXPERT = 1408

TM = 256
CH = 64
SCAN_ROWS = 256
HB = 16
RW_BPS = 2
MLA_HPS = 4
MLA_VROWS = 80
LOG2E = 1.4426950408889634
MLA_KCHUNK = 256
MLA_SKEW = 8
VMEM_LIMIT = 56 * 1024 * 1024
RW_GRAM_MODE = "bf16"
RW_BASE_MODE = "bf16"
RW_MERGE_MODE = "bf16"
RW_STATE_MODE = "bf16"

_NN = (((1,), (0,)), ((), ()))
_NT = (((1,), (1,)), ((), ()))
_TN = (((0,), (0,)), ((), ()))


def _mm(a, b, dims=_NN, mode="bf16"):
    if mode == "bf16":
        return lax.dot_general(a.astype(BF16), b.astype(BF16), dims, preferred_element_type=F32)
    if mode == "x3":
        ah = a.astype(BF16)
        al = (a - ah.astype(F32)).astype(BF16)
        bh = b.astype(BF16)
        bl = (b - bh.astype(F32)).astype(BF16)
        d = lambda x, y: lax.dot_general(x, y, dims, preferred_element_type=F32)
        return d(ah, bh) + (d(ah, bl) + d(al, bh))
    return lax.dot_general(a, b, dims, precision=lax.Precision.HIGHEST, preferred_element_type=F32)


def _sigmoid(x):
    return 1.0 / (1.0 + jnp.exp(-x))


def _silu(x):
    return x / (1.0 + jnp.exp(-x))


def _iota(shape, dim):
    return lax.broadcasted_iota(jnp.int32, shape, dim)


def _head_block_mask(n):
    return (_iota((n, n), 0) // HD) == (_iota((n, n), 1) // HD)


def _running_sum(x, fwd, seg):
    n = x.shape[0]
    pos = _iota(x.shape, 0) % seg
    sh = 1
    while sh < seg:
        if fwd:
            x = x + jnp.where(pos >= sh, pltpu.roll(x, sh, 0), 0.0)
        else:
            x = x + jnp.where(pos < seg - sh, pltpu.roll(x, n - sh, 0), 0.0)
        sh *= 2
    return x


def _alternate(*stages, skew=0):
    live = list(enumerate(stages))
    rnd = 0
    while live:
        for item in list(live):
            i, g = item
            if rnd < i * skew:
                continue
            try:
                next(g)
            except StopIteration:
                live.remove(item)
        rnd += 1


def _cparams(sem):
    return pltpu.CompilerParams(dimension_semantics=sem, vmem_limit_bytes=VMEM_LIMIT)


def _ada_body(c_ref, w_ref, b_ref, o_ref):
    s = _silu(c_ref[...])
    o_ref[0] = _mm(s, w_ref[0]) + b_ref[0]


def _ada(c8, w_ada, b_ada):
    L = w_ada.shape[0]
    r8 = c8.shape[0]
    tn = 1536
    return pl.pallas_call(
        _ada_body,
        grid=(L, 6 * D // tn),
        in_specs=[
            pl.BlockSpec((r8, D), lambda l, j: (0, 0)),
            pl.BlockSpec((1, D, tn), lambda l, j: (l, 0, j)),
            pl.BlockSpec((1, 1, tn), lambda l, j: (l, 0, j)),
        ],
        out_specs=pl.BlockSpec((1, r8, tn), lambda l, j: (l, 0, j)),
        out_shape=jax.ShapeDtypeStruct((L, r8, 6 * D), F32),
        compiler_params=_cparams(("parallel", "parallel")),
        name="adaln",
    )(c8, w_ada, b_ada.reshape(L, 1, 6 * D))


def _proj_body(*refs, pending, lc_tiles, nb):
    if pending:
        (x_ref, y_ref, modp_ref, mod_ref, g_ref, wall, cs_s, sn_s, cs_m, sn_m,
         xo_ref, phg, prw, pswa, pmla) = refs
    else:
        (x_ref, mod_ref, g_ref, wall, cs_s, sn_s, cs_m, sn_m,
         phg, prw, pswa, pmla) = refs
    b = pl.program_id(0)
    i = pl.program_id(1)
    row = jnp.where(i < lc_tiles, nb, b)
    x = x_ref[0]
    if pending:
        gate = modp_ref[pl.ds(row, 1), 5 * D:6 * D]
        x = x + gate * y_ref[0]
        xo_ref[0] = x
    sh = mod_ref[pl.ds(row, 1), 0:D]
    sc = mod_ref[pl.ds(row, 1), D:2 * D]
    ms = jnp.mean(x * x, axis=-1, keepdims=True)
    h = (x * lax.rsqrt(ms + NORM_EPS) * g_ref[...]) * (1.0 + sc) + sh
    hb = h.astype(BF16)
    n1 = 5 * GW
    phg[0] = _mm(hb, wall[:, 0:n1])
    prw[0] = _mm(hb, wall[:, n1:2 * n1])
    s = _mm(hb, wall[:, 2 * n1:3 * n1])
    qk = s[:, 0:2 * GW] * cs_s[...] + s[:, 3 * GW:5 * GW] * sn_s[...]
    pswa[0, :, 0:2 * GW] = qk.astype(BF16)
    pswa[0, :, 2 * GW:3 * GW] = s[:, 2 * GW:3 * GW].astype(BF16)
    m = _mm(hb, wall[:, 3 * n1:3 * n1 + 640])
    pmla[0, :, 0:384] = m[:, 0:384]
    pmla[0, :, 384:512] = m[:, 384:512] * cs_m[...] + m[:, 512:640] * sn_m[...]


def _proj(x, pend, mod, g, w, tabs, lc):
    B, S, _ = x.shape
    nt = S // TM
    pending = pend is not None
    row_spec = lambda n: pl.BlockSpec((1, TM, n), lambda b, i: (b, i, 0))
    full = lambda a: pl.BlockSpec(a.shape, lambda b, i: (0,) * a.ndim)
    tab_spec = lambda a: pl.BlockSpec((TM, a.shape[1]), lambda b, i: (i, 0))
    ins = [x]
    in_specs = [row_spec(D)]
    if pending:
        ins += [pend[0], pend[1]]
        in_specs += [row_spec(D), full(pend[1])]
    ins += [mod, g, w, tabs["cs_s"], tabs["sn_s"], tabs["cs_m"], tabs["sn_m"]]
    in_specs += [full(mod), full(g), full(w),
                 tab_spec(tabs["cs_s"]), tab_spec(tabs["sn_s"]), tab_spec(tabs["cs_m"]), tab_spec(tabs["sn_m"])]
    out_shape = [jax.ShapeDtypeStruct((B, S, 5 * GW), F32), jax.ShapeDtypeStruct((B, S, 5 * GW), F32),
                 jax.ShapeDtypeStruct((B, S, 3 * GW), BF16), jax.ShapeDtypeStruct((B, S, 512), F32)]
    out_specs = [row_spec(5 * GW), row_spec(5 * GW), row_spec(3 * GW), row_spec(512)]
    if pending:
        out_shape = [jax.ShapeDtypeStruct((B, S, D), F32)] + out_shape
        out_specs = [row_spec(D)] + out_specs
    outs = pl.pallas_call(
        functools.partial(_proj_body, pending=pending, lc_tiles=lc // TM, nb=B),
        grid=(B, nt), in_specs=in_specs, out_specs=out_specs, out_shape=out_shape,
        compiler_params=_cparams(("parallel", "parallel")),
        name="proj",
    )(*ins)
    if pending:
        return tuple(outs)
    return (x,) + tuple(outs)


def _chunk_of_step(d, s, nc, nc_ctx):
    bwd = jnp.where(s < nc_ctx, nc_ctx - 1 - s, nc - 1 - (s - nc_ctx))
    return jnp.where(d == 0, s, bwd)


def _hg_direction(p_ref, lb, o_ref, st_ref, fwd, c0):
    p = p_ref[0, c0:c0 + CH, :]
    z = p[:, GW:2 * GW] if fwd else p[:, 2 * GW:3 * GW]
    sig = _sigmoid(z)
    f = lb + (1.0 - lb) * sig
    logf = jnp.log(jnp.maximum(f, HG_F_FLOOR))
    key = (1.0 - lb) * (1.0 - sig)
    qs = _silu(p[:, 0:GW])
    iv = p[:, 3 * GW:4 * GW]

    bcum = _running_sum(logf, fwd, HB)

    bdf = _head_block_mask(GW).astype(BF16)
    hm = (_iota((NH * HB, GW), 0) // HB) == (_iota((NH * HB, GW), 1) // HD)
    expand = lambda a: jnp.where(hm, jnp.concatenate([a] * NH, axis=0), 0.0)
    half = HB // 2
    nblk = CH // HB
    order = [(j if fwd else nblk - 1 - j) * HB for j in range(nblk)]
    blocks = [tuple(a[r0:r0 + HB] for a in (qs, key, iv, bcum)) for r0 in order]
    es = []
    spans = []
    for qb, kb, ib, bb in blocks:
        for sidx in range(HB):
            if fwd:
                lo, hi = (0, HB) if sidx < half else (half, HB)
            else:
                lo, hi = (0, half) if sidx < half else (0, HB)
            rows = _iota((hi - lo, 1), 0) + lo
            m = (rows >= sidx) if fwd else (rows <= sidx)
            dec = jnp.exp(jnp.where(m, bb[lo:hi] - bb[sidx:sidx + 1], -jnp.inf))
            es.append(qb[lo:hi] * kb[sidx:sidx + 1] * dec)
            spans.append((lo, hi))
        yield
    att = _mm(jnp.concatenate(es, axis=0), bdf)
    yield
    inblock = []
    outers = []
    off = 0
    for j, (qb, kb, ib, bb) in enumerate(blocks):
        top = jnp.zeros((half, GW), F32)
        bot = jnp.zeros((half, GW), F32)
        for sidx in range(HB):
            lo, hi = spans[j * HB + sidx]
            a = att[off:off + hi - lo] * ib[sidx:sidx + 1]
            off += hi - lo
            if hi - lo == HB:
                top = top + a[0:half]
                bot = bot + a[half:HB]
            elif lo == 0:
                top = top + a
            else:
                bot = bot + a
        inblock.append((top, bot))
        tot = bb[HB - 1:HB] if fwd else bb[0:1]
        outers.append((_mm(expand(ib), expand(kb * jnp.exp(tot - bb)), _TN), jnp.exp(tot)))
        yield
    for r0, (qb, kb, ib, bb), (top, bot), (outer, dtot) in zip(order, blocks, inblock, outers):
        st = st_ref[...]
        o_blk = _mm(qb * jnp.exp(bb), st, _NT)
        o_ref[0, c0 + r0:c0 + r0 + half, :] = o_blk[0:half] + top
        o_ref[0, c0 + r0 + half:c0 + r0 + HB, :] = o_blk[half:HB] + bot
        st_ref[...] = st * dtot + outer
        yield


def _hg_body(pf_ref, pb_ref, lbraw_ref, of_ref, ob_ref, stf_ref, stb_ref, *, layer):
    @pl.when(pl.program_id(1) == 0)
    def _():
        stf_ref[...] = jnp.zeros_like(stf_ref)
        stb_ref[...] = jnp.zeros_like(stb_ref)

    raw = lbraw_ref[...]
    nl = raw.shape[0]
    mx = raw[0]
    for j in range(1, nl):
        mx = jnp.maximum(mx, raw[j])
    ex = [jnp.exp(raw[j] - mx) for j in range(nl)]
    den = ex[0]
    for j in range(1, nl):
        den = den + ex[j]
    lb2 = jnp.zeros_like(mx)
    for j in range(1, layer + 1):
        lb2 = lb2 + ex[j] / den
    def scan(p_ref, lb, o_ref, st_ref, fwd):
        chunks = range(SCAN_ROWS // CH)
        for c in (chunks if fwd else reversed(chunks)):
            yield from _hg_direction(p_ref, lb, o_ref, st_ref, fwd, c * CH)

    _alternate(scan(pf_ref, lb2[0:1], of_ref, stf_ref, True), scan(pb_ref, lb2[1:2], ob_ref, stb_ref, False))


def _hgrn2(p_hg, hg_lb, layer, lc):
    B, S, _ = p_hg.shape
    rs = SCAN_ROWS
    nc = S // rs
    nc_ctx = lc // rs
    fmap = lambda b, s: (b, s, 0)
    bmap = lambda b, s: (b, _chunk_of_step(1, s, nc, nc_ctx), 0)
    return pl.pallas_call(
        functools.partial(_hg_body, layer=layer),
        grid=(B, nc),
        in_specs=[pl.BlockSpec((1, rs, 5 * GW), fmap), pl.BlockSpec((1, rs, 5 * GW), bmap),
                  pl.BlockSpec(hg_lb.shape, lambda b, s: (0, 0, 0))],
        out_specs=[pl.BlockSpec((1, rs, GW), fmap), pl.BlockSpec((1, rs, GW), bmap)],
        out_shape=[jax.ShapeDtypeStruct((B, S, GW), F32)] * 2,
        scratch_shapes=[pltpu.VMEM((GW, GW), F32)] * 2,
        compiler_params=_cparams(("parallel", "arbitrary")),
        name="hgrn2",
    )(p_hg, p_hg, hg_lb)


def _rwprep_body(cur_ref, prv_ref, nxt_ref, mu_ref, w0_ref, w2_ref, a0_ref, a2_ref, g2_ref, kk_ref, ka_ref,
                 o_ref, *, lc_tiles):
    i = pl.program_id(1)
    nt = pl.num_programs(1)
    first = (i == 0) | (i == lc_tiles)
    lastt = (i == lc_tiles - 1) | (i == nt - 1)
    cur = cur_ref[0]
    rowi = _iota((TM, 1), 0)
    before = jnp.where(first, 0.0, prv_ref[0, 7:8, :])
    after = jnp.where(lastt, 0.0, nxt_ref[0, 0:1, :])
    prev = jnp.where(rowi == 0, before, pltpu.roll(cur, 1, 0))
    nxt = jnp.where(rowi == TM - 1, after, pltpu.roll(cur, TM - 1, 0))
    sft = cur + mu_ref[0:1, :] * (prev - cur) + mu_ref[1:2, :] * (nxt - cur)
    r = sft[:, 0:GW]
    k = sft[:, GW:2 * GW]
    v = sft[:, 2 * GW:3 * GW]
    ywf = sft[:, 768:896]
    ywb = sft[:, 896:1024]
    ya = sft[:, 1024:1152]
    yg = sft[:, 1152:1280]
    c = float(np.exp(-0.5))
    lwf = -c * _sigmoid(w0_ref[0:1, :] + _mm(jnp.tanh(ywf), w2_ref[0]))
    lwb = -c * _sigmoid(w0_ref[1:2, :] + _mm(jnp.tanh(ywb), w2_ref[1]))
    a = _sigmoid(a0_ref[...] + _mm(ya, a2_ref[...]))
    kk0 = k * kk_ref[...]
    ss = _mm(kk0 * kk0, _head_block_mask(GW).astype(F32), mode="x3")
    kk = kk0 / jnp.maximum(jnp.sqrt(ss), 1e-12)
    kp = k * (1.0 + (a - 1.0) * ka_ref[...])
    g = _mm(_sigmoid(yg), g2_ref[...])
    o_ref[0, :, 0:GW] = r
    o_ref[0, :, GW:2 * GW] = lwf
    o_ref[0, :, 2 * GW:3 * GW] = lwb
    o_ref[0, :, 3 * GW:4 * GW] = kp
    o_ref[0, :, 4 * GW:5 * GW] = v
    o_ref[0, :, 5 * GW:6 * GW] = kk
    o_ref[0, :, 6 * GW:7 * GW] = kk * a
    o_ref[0, :, 7 * GW:8 * GW] = g


def _rwprep(p_rw, w, lc):
    B, S, n = p_rw.shape
    nt = S // TM
    t8 = TM // 8
    full = lambda a: pl.BlockSpec(a.shape, lambda b, i: (0,) * a.ndim)
    small = [w["mu"], w["w0"], w["w2"], w["a0"], w["a2"], w["g2"], w["k_k"], w["k_a"]]
    return pl.pallas_call(
        functools.partial(_rwprep_body, lc_tiles=lc // TM),
        grid=(B, nt),
        in_specs=[pl.BlockSpec((1, TM, n), lambda b, i: (b, i, 0)),
                  pl.BlockSpec((1, 8, n), lambda b, i: (b, jnp.maximum(i * t8 - 1, 0), 0)),
                  pl.BlockSpec((1, 8, n), lambda b, i: (b, jnp.minimum((i + 1) * t8, S // 8 - 1), 0))]
                 + [full(a) for a in small],
        out_specs=pl.BlockSpec((1, TM, 8 * GW), lambda b, i: (b, i, 0)),
        out_shape=jax.ShapeDtypeStruct((B, S, 8 * GW), F32),
        compiler_params=_cparams(("parallel", "parallel")),
        name="rwkv_prep",
    )(p_rw, p_rw, p_rw, *small)


def _rw_direction(x_ref, o_ref, h_ref, fwd, bi, c0):
    x = x_ref[bi, c0:c0 + CH, :]
    r = x[:, 0:GW]
    lw = x[:, GW:2 * GW] if fwd else x[:, 2 * GW:3 * GW]
    k = x[:, 3 * GW:4 * GW]
    v = x[:, 4 * GW:5 * GW]
    kk = x[:, 5 * GW:6 * GW]
    bb = x[:, 6 * GW:7 * GW]

    incl = _running_sum(lw, fwd, CH)
    tot = incl[CH - 1:CH] if fwd else incl[0:1]
    e_in = jnp.exp(incl)
    e_ni = jnp.exp(-incl)
    e_tt = jnp.exp(tot - incl)
    km = kk * jnp.exp(incl - lw)
    rt = r * e_in
    bh = bb * e_ni
    kh = k * e_ni
    bt = bb * e_tt
    kt = k * e_tt

    n = NH * CH
    hm = (_iota((n, GW), 0) // CH) == (_iota((n, GW), 1) // HD)
    tile4 = lambda a: jnp.concatenate([a] * NH, axis=0)
    expand = lambda a: jnp.where(hm, tile4(a), 0.0)
    km_e = expand(km)
    rt_e = expand(rt)
    v_e = expand(v)
    bt_e = expand(bt)
    kt_e = expand(kt)
    bh_t = tile4(bh)
    kh_t = tile4(kh)

    rr = _iota((n, n), 0)
    cc = _iota((n, n), 1)
    same = (rr // CH) == (cc // CH)
    rt_i = rr % CH
    ct_i = cc % CH
    before = same & ((ct_i < rt_i) if fwd else (ct_i > rt_i))
    upto = same & ((ct_i <= rt_i) if fwd else (ct_i >= rt_i))
    eye = rr == cc

    pg = RW_GRAM_MODE
    a_m = jnp.where(before, _mm(km_e, bh_t, _NT, pg), 0.0)
    yield

    blk = lambda m: (rr // m) == (cc // m)
    pb = RW_BASE_MODE
    nm = jnp.where(blk(8), -a_m, 0.0)
    x2 = _mm(nm, nm, mode=pb)
    b_m = jnp.where(before, _mm(km_e, kh_t, _NT, pg), 0.0)
    yield
    tm = jnp.where(eye, 1.0, 0.0) + nm
    tm = tm + _mm(tm, x2, mode=pb)
    x4 = _mm(x2, x2, mode=pb)
    yield
    tm = tm + _mm(tm, x4, mode=pb)
    mrb = jnp.where(upto, _mm(rt_e, bh_t, _NT, pg), 0.0)
    yield
    pm = RW_MERGE_MODE
    side = [lambda: jnp.where(upto, _mm(rt_e, kh_t, _NT, pg), 0.0), lambda: _mm(b_m, v_e, mode=pg),
            lambda: _mm(kt_e, v_e, _TN, pg)]
    done = []
    for m in (8, 16, 32):
        low = jnp.where(blk(2 * m) & (rr // m != cc // m), a_m, 0.0)
        y = _mm(low, tm, mode=pm)
        done.append(side.pop(0)())
        yield
        tm = tm - _mm(tm, y, mode=pm)
        yield
    mrk, bmv, ktv = done

    p1 = _mm(tm, km_e, mode=pg)
    q1 = _mm(tm, bmv, mode=pg)
    yield
    p2e = rt_e - _mm(mrb, p1, mode=pg)
    q2e = _mm(mrk, v_e, mode=pg) - _mm(mrb, q1, mode=pg)
    g_m = jnp.where(eye, jnp.exp(tot), 0.0) - _mm(bt_e, p1, _TN, pg)
    f_m = ktv - _mm(bt_e, q1, _TN, pg)
    yield
    fold = lambda a: a[0:CH] + a[CH:2 * CH] + a[2 * CH:3 * CH] + a[3 * CH:4 * CH]
    p2 = fold(p2e)
    q2 = fold(q2e)
    h = h_ref[bi]
    ps = RW_STATE_MODE
    o_ref[bi, c0:c0 + CH, :] = _mm(p2, h, mode=ps) + q2
    h_ref[bi] = _mm(g_m, h, mode=ps) + f_m


def _rw_body(xf_ref, xb_ref, of_ref, ob_ref, hf_ref, hb_ref):
    @pl.when(pl.program_id(1) == 0)
    def _():
        hf_ref[...] = jnp.zeros_like(hf_ref)
        hb_ref[...] = jnp.zeros_like(hb_ref)

    def scan(x_ref, o_ref, h_ref, fwd, bi):
        chunks = range(SCAN_ROWS // CH)
        for c in (chunks if fwd else reversed(chunks)):
            yield from _rw_direction(x_ref, o_ref, h_ref, fwd, bi, c * CH)

    chains = []
    for bi in range(xf_ref.shape[0]):
        chains.append(scan(xf_ref, of_ref, hf_ref, True, bi))
        chains.append(scan(xb_ref, ob_ref, hb_ref, False, bi))
    _alternate(*chains)


def _rwscan(prep, lc):
    B, S, n = prep.shape
    rs = SCAN_ROWS
    nc = S // rs
    nc_ctx = lc // rs
    bps = RW_BPS if B % RW_BPS == 0 else 1
    fmap = lambda b, s: (b, s, 0)
    bmap = lambda b, s: (b, _chunk_of_step(1, s, nc, nc_ctx), 0)
    return pl.pallas_call(
        _rw_body,
        grid=(B // bps, nc),
        in_specs=[pl.BlockSpec((bps, rs, n), fmap), pl.BlockSpec((bps, rs, n), bmap)],
        out_specs=[pl.BlockSpec((bps, rs, GW), fmap), pl.BlockSpec((bps, rs, GW), bmap)],
        out_shape=[jax.ShapeDtypeStruct((B, S, GW), F32)] * 2,
        scratch_shapes=[pltpu.VMEM((bps, GW, GW), F32)] * 2,
        compiler_params=_cparams(("parallel", "arbitrary")),
        name="rwkv_scan",
    )(prep, prep)


def _swa_block(q, kloc, vloc, kc, vc, sink_ref, o_ref, r0, j, lc_blocks, nbl):
    w = SWA_W
    is_lat = j >= lc_blocks
    nblk = j - lc_blocks
    lane_h = _iota((w, GW), 1) // HD
    qs = jnp.concatenate([jnp.where(lane_h == h, q, jnp.zeros_like(q)) for h in range(NH)], axis=0)
    scale = HD ** -0.5
    s_loc = _mm(qs, kloc, _NT) * scale
    s_ctx = _mm(qs, kc, _NT) * scale
    yield
    qi = _iota((NH * w, 3 * w), 0) % w
    col = _iota((NH * w, 3 * w), 1)
    seg = col // w
    kj = col % w
    valid = ((seg == 0) & (nblk >= 1) & (kj >= qi)) | (seg == 1) | ((seg == 2) & (nblk < nbl - 1) & (kj <= qi))
    valid = valid & is_lat
    s_loc = jnp.where(valid, s_loc, -jnp.inf)
    hrow = _iota((NH * w, 1), 0) // w
    sink = jnp.zeros((NH * w, 1), F32)
    for h in range(NH):
        sink = jnp.where(hrow == h, sink_ref[h:h + 1, 0:1], sink)
    m = jnp.maximum(jnp.maximum(jnp.max(s_loc, axis=-1, keepdims=True), jnp.max(s_ctx, axis=-1, keepdims=True)), sink)
    p_loc = jnp.exp(s_loc - m)
    p_ctx = jnp.exp(s_ctx - m)
    den = jnp.sum(p_loc, axis=-1, keepdims=True) + jnp.sum(p_ctx, axis=-1, keepdims=True) + jnp.exp(sink - m)
    yield
    of = (_mm(p_loc, vloc) + _mm(p_ctx, vc)) / den
    out = jnp.zeros((w, GW), F32)
    for h in range(NH):
        out = out + jnp.where(lane_h == h, of[h * w:(h + 1) * w], 0.0)
    o_ref[0, r0:r0 + w, :] = out
    yield


def _swa_body(pair_ref, prv_ref, nxt_ref, ctx_ref, sink_ref, o_ref, *, lc_blocks, nbl, i0):
    i = pl.program_id(1) + i0
    w = SWA_W
    kcol = slice(GW, 2 * GW)
    vcol = slice(2 * GW, 3 * GW)
    ka, kb = pair_ref[0, 0:w, kcol], pair_ref[0, w:2 * w, kcol]
    va, vb = pair_ref[0, 0:w, vcol], pair_ref[0, w:2 * w, vcol]
    kc = ctx_ref[0, :, kcol]
    vc = ctx_ref[0, :, vcol]
    cat = lambda *a: jnp.concatenate(a, axis=0)
    first = _swa_block(pair_ref[0, 0:w, 0:GW], cat(prv_ref[0, :, kcol], ka, kb), cat(prv_ref[0, :, vcol], va, vb),
                       kc, vc, sink_ref, o_ref, 0, 2 * i, lc_blocks, nbl)
    second = _swa_block(pair_ref[0, w:2 * w, 0:GW], cat(ka, kb, nxt_ref[0, :, kcol]), cat(va, vb, nxt_ref[0, :, vcol]),
                        kc, vc, sink_ref, o_ref, w, 2 * i + 1, lc_blocks, nbl)
    _alternate(first, second, skew=1)


def _swa(p_swa, sink8, lc, skip_ctx):
    B, S, n = p_swa.shape
    w = SWA_W
    nb = S // w
    lcb = lc // w
    assert nb % 2 == 0 and lcb % 2 == 0
    i0 = lcb // 2 if skip_ctx else 0
    return pl.pallas_call(
        functools.partial(_swa_body, lc_blocks=lcb, nbl=nb - lcb, i0=i0),
        grid=(B, nb // 2 - i0),
        in_specs=[pl.BlockSpec((1, 2 * w, n), lambda b, i: (b, i + i0, 0)),
                  pl.BlockSpec((1, w, n), lambda b, i: (b, jnp.maximum(2 * (i + i0) - 1, 0), 0)),
                  pl.BlockSpec((1, w, n), lambda b, i: (b, jnp.minimum(2 * (i + i0) + 2, nb - 1), 0)),
                  pl.BlockSpec((1, lc, n), lambda b, i: (b, 0, 0)),
                  pl.BlockSpec(sink8.shape, lambda b, i: (0, 0))],
        out_specs=pl.BlockSpec((1, 2 * w, GW), lambda b, i: (b, i, 0)),
        out_shape=jax.ShapeDtypeStruct((B, S - 2 * w * i0, GW), F32),
        compiler_params=_cparams(("parallel", "parallel")),
        name="swa",
    )(p_swa, p_swa, p_swa, p_swa, sink8)


def _mlaprep_body(p_ref, qg_ref, kg_ref, wq, wqr, wk, wv, cs_ref, sn_ref, q_ref, k_ref, v_ref):
    p = p_ref[0]
    cq = p[:, 0:256]
    ckv = p[:, 256:384]
    krs = p[:, 384:512]
    qn = cq * lax.rsqrt(jnp.sum(cq * cq, axis=-1, keepdims=True) / MLA_Q_RANK + NORM_EPS) * qg_ref[...]
    kvn = ckv * lax.rsqrt(jnp.mean(ckv * ckv, axis=-1, keepdims=True) + NORM_EPS) * kg_ref[...]
    scale = (MLA_NOPE + MLA_ROPE) ** -0.5 * LOG2E
    cs = cs_ref[...]
    sn = sn_ref[...]
    kvn_t = kvn.T.astype(BF16)
    ones_rows = (_iota((MLA_VROWS - HD, TM), 0) == 0).astype(BF16)
    for h in range(NH):
        qh = _mm(qn, wq[h]) * cs + _mm(qn, wqr[h]) * sn
        q_ref[0, h] = (qh * scale).T.astype(BF16)
        k_ref[0, h] = (_mm(kvn, wk[h]) + krs).astype(BF16)
        v_ref[0, h, 0:HD, :] = _mm(wv[h], kvn_t).astype(BF16)
        v_ref[0, h, HD:MLA_VROWS, :] = ones_rows


def _mlaprep(p_mla, w, tabs):
    B, S, n = p_mla.shape
    nt = S // TM
    full = lambda a: pl.BlockSpec(a.shape, lambda b, i: (0,) * a.ndim)
    ws = [w["qg"], w["kg"], w["wq"], w["wqr"], w["wk"], w["wv"]]
    hspec = lambda c: pl.BlockSpec((1, NH, TM, c), lambda b, i: (b, 0, i, 0))
    return pl.pallas_call(
        _mlaprep_body,
        grid=(B, nt),
        in_specs=[pl.BlockSpec((1, TM, n), lambda b, i: (b, i, 0))] + [full(a) for a in ws]
                 + [pl.BlockSpec((TM, 128), lambda b, i: (i, 0))] * 2,
        out_specs=[pl.BlockSpec((1, NH, 128, TM), lambda b, i: (b, 0, 0, i)), hspec(128),
                   pl.BlockSpec((1, NH, MLA_VROWS, TM), lambda b, i: (b, 0, 0, i))],
        out_shape=[jax.ShapeDtypeStruct((B, NH, 128, S), BF16), jax.ShapeDtypeStruct((B, NH, S, 128), BF16),
                   jax.ShapeDtypeStruct((B, NH, MLA_VROWS, S), BF16)],
        compiler_params=_cparams(("parallel", "parallel")),
        name="mla_prep",
    )(p_mla, *ws, tabs["cs_q"], tabs["sn_q"])


def _mla_body(q_ref, k_ref, v_ref, o_ref, *, lc, t0):
    i = pl.program_id(2) + t0
    ctx_q = i * TM < lc
    kc = MLA_KCHUNK
    nk = k_ref.shape[2] // kc

    def head(h):
        q = q_ref[0, h]
        ss = []
        m = None
        for c in range(nk):
            s = _mm(k_ref[0, h, c * kc:(c + 1) * kc, :], q)
            if c * kc >= lc and t0 * TM < lc:
                s = jnp.where(ctx_q, -jnp.inf, s)
            ss.append(s)
            cm = jnp.max(s, axis=0, keepdims=True)
            m = cm if m is None else jnp.maximum(m, cm)
            yield
        acc = None
        for c in range(nk):
            p = jnp.exp2(ss[c] - m)
            pv = _mm(v_ref[0, h, :, c * kc:(c + 1) * kc], p)
            acc = pv if acc is None else acc + pv
            yield
        o_ref[0, h] = acc[0:HD] / acc[HD:HD + 1]

    _alternate(*[head(h) for h in range(MLA_HPS)], skew=MLA_SKEW)


def _mla(q, k, v, lc, skip_ctx):
    B, H, S, _ = k.shape
    nt = S // TM
    hp = MLA_HPS
    t0 = lc // TM if skip_ctx else 0
    return pl.pallas_call(
        functools.partial(_mla_body, lc=lc, t0=t0),
        grid=(B, H // hp, nt - t0),
        in_specs=[pl.BlockSpec((1, hp, 128, TM), lambda b, h, i: (b, h, 0, i + t0)),
                  pl.BlockSpec((1, hp, S, 128), lambda b, h, i: (b, h, 0, 0)),
                  pl.BlockSpec((1, hp, MLA_VROWS, S), lambda b, h, i: (b, h, 0, 0))],
        out_specs=pl.BlockSpec((1, hp, HD, TM), lambda b, h, i: (b, h, 0, i)),
        out_shape=jax.ShapeDtypeStruct((B, H, HD, S - t0 * TM), F32),
        compiler_params=_cparams(("parallel", "parallel", "parallel")),
        name="mla_attn",
    )(q, k, v)


def _outproj_body(*refs, lc_tiles, nb, moe, t0):
    if moe:
        (x_ref, mod_ref, n2_ref, hgf_ref, hgb_ref, hgg_ref, hgn_ref, rwf_ref, rwb_ref, rwp_ref, rk_ref, lng_ref, lnb_ref,
         swo_ref, mlo_ref, wout_ref, rt_ref, xo_ref, h2_ref, lg_ref) = refs
    else:
        (x_ref, mod_ref, n2_ref, hgf_ref, hgb_ref, hgg_ref, hgn_ref, rwf_ref, rwb_ref, rwp_ref, rk_ref, lng_ref, lnb_ref,
         swo_ref, mlo_ref, wout_ref, xo_ref, h2_ref) = refs
    b = pl.program_id(0)
    i = pl.program_id(1) + t0
    row = jnp.where(i < lc_tiles, nb, b)
    bd = _head_block_mask(GW).astype(BF16)

    def hsum(a):
        hi = a.astype(BF16)
        lo = (a - hi.astype(F32)).astype(BF16)
        return _mm(hi, bd) + _mm(lo, bd)

    o = hgf_ref[0] + hgb_ref[0]
    hg = o * lax.rsqrt(hsum(o * o) / HD + NORM_EPS) * hgn_ref[...] * _silu(hgg_ref[0])

    o = rwf_ref[0] + rwb_ref[0]
    c = o - hsum(o) / HD
    on = c * lax.rsqrt(hsum(c * c) / HD + RW_LN_EPS) * lng_ref[...] + lnb_ref[...]
    r = rwp_ref[0, :, 0:GW]
    kp = rwp_ref[0, :, 3 * GW:4 * GW]
    v = rwp_ref[0, :, 4 * GW:5 * GW]
    g = rwp_ref[0, :, 7 * GW:8 * GW]
    rw = (on + hsum(r * kp * rk_ref[...]) * v) * g

    acc = _mm(hg, wout_ref[0:GW, :]) + _mm(rw, wout_ref[GW:2 * GW, :]) + _mm(swo_ref[0], wout_ref[2 * GW:3 * GW, :])
    for h in range(NH):
        acc = acc + _mm(mlo_ref[0, h], wout_ref[3 * GW + h * HD:3 * GW + (h + 1) * HD, :], _TN)

    x = x_ref[0] + mod_ref[pl.ds(row, 1), 2 * D:3 * D] * acc
    xo_ref[0] = x
    sh = mod_ref[pl.ds(row, 1), 3 * D:4 * D]
    sc = mod_ref[pl.ds(row, 1), 4 * D:5 * D]
    ms = jnp.mean(x * x, axis=-1, keepdims=True)
    h2 = (x * lax.rsqrt(ms + NORM_EPS) * n2_ref[...]) * (1.0 + sc) + sh
    h2_ref[0] = h2.astype(BF16)
    if moe:
        lg_ref[0] = _mm(h2, rt_ref[...], mode="x3")


def _outproj(x, mod, n2g, hgo, p_hg, hgn, rwo, rwp, rk, lng, lnb, swo, mlo, wout, router, lc, skip_ctx):
    B, S, _ = x.shape
    nt = S // TM
    moe = router is not None
    t0 = lc // TM if skip_ctx else 0
    so = S - t0 * TM
    row = lambda n: pl.BlockSpec((1, TM, n), lambda b, i: (b, i + t0, 0))
    lat = lambda n: pl.BlockSpec((1, TM, n), lambda b, i: (b, i, 0))
    full = lambda a: pl.BlockSpec(a.shape, lambda b, i: (0,) * a.ndim)
    ins = [x, mod, n2g, hgo[0], hgo[1], p_hg, hgn, rwo[0], rwo[1], rwp, rk, lng, lnb, swo, mlo, wout]
    in_specs = [row(D), full(mod), full(n2g), row(GW), row(GW),
                pl.BlockSpec((1, TM, GW), lambda b, i: (b, i + t0, 4)), full(hgn),
                row(GW), row(GW), row(8 * GW), full(rk), full(lng), full(lnb), lat(GW),
                pl.BlockSpec((1, NH, HD, TM), lambda b, i: (b, 0, 0, i)), full(wout)]
    out_shape = [jax.ShapeDtypeStruct((B, so, D), F32), jax.ShapeDtypeStruct((B, so, D), BF16)]
    out_specs = [lat(D), lat(D)]
    if moe:
        ins.append(router)
        in_specs.append(full(router))
        out_shape.append(jax.ShapeDtypeStruct((B, so, 128), F32))
        out_specs.append(lat(128))
    return pl.pallas_call(
        functools.partial(_outproj_body, lc_tiles=lc // TM, nb=B, moe=moe, t0=t0),
        grid=(B, nt - t0), in_specs=in_specs, out_specs=out_specs, out_shape=out_shape,
        compiler_params=_cparams(("parallel", "parallel")),
        name="outproj",
    )(*ins)


def _ffn_body(h_ref, w1_ref, w3_ref, w2_ref, o_ref, acc_ref):
    j = pl.program_id(1)

    @pl.when(j == 0)
    def _():
        acc_ref[...] = jnp.zeros_like(acc_ref)

    h = h_ref[...]
    a = _mm(h, w1_ref[...])
    z = _silu(a) * _mm(h, w3_ref[...])
    acc_ref[...] += _mm(z, w2_ref[...])

    @pl.when(j == pl.num_programs(1) - 1)
    def _():
        o_ref[...] = acc_ref[...]


def _ffn(h2, w1, w3, w2):
    M = h2.shape[0]
    tm = 1024
    tf = D_FF // 2
    return pl.pallas_call(
        _ffn_body,
        grid=(M // tm, D_FF // tf),
        in_specs=[pl.BlockSpec((tm, D), lambda i, j: (i, 0)),
                  pl.BlockSpec((D, tf), lambda i, j: (0, j)),
                  pl.BlockSpec((D, tf), lambda i, j: (0, j)),
                  pl.BlockSpec((tf, D), lambda i, j: (j, 0))],
        out_specs=pl.BlockSpec((tm, D), lambda i, j: (i, 0)),
        out_shape=jax.ShapeDtypeStruct((M, D), F32),
        scratch_shapes=[pltpu.VMEM((tm, D), F32)],
        compiler_params=_cparams(("parallel", "arbitrary")),
        name="ffn",
    )(h2, w1, w3, w2)


def _moe_body(*refs, final_rows):
    if final_rows:
        h_ref, lg_ref, w1_ref, w3_ref, w2_ref, x_ref, mod_ref, g_ref, o_ref, acc_ref, comb_ref = refs
    else:
        h_ref, lg_ref, w1_ref, w3_ref, w2_ref, o_ref, acc_ref, comb_ref = refs
    e = pl.program_id(1)
    tile = pl.program_id(0)
    lane = _iota(comb_ref.shape, 1)

    @pl.when(e == 0)
    def _():
        acc_ref[...] = jnp.zeros_like(acc_ref)
        lg = jnp.where(lane < N_EXPERTS, lg_ref[...], -jnp.inf)
        v1 = jnp.max(lg, axis=-1, keepdims=True)
        i1 = jnp.min(jnp.where(lg == v1, lane, 128), axis=-1, keepdims=True)
        lg2 = jnp.where(lane == i1, -jnp.inf, lg)
        v2 = jnp.max(lg2, axis=-1, keepdims=True)
        i2 = jnp.min(jnp.where(lg2 == v2, lane, 128), axis=-1, keepdims=True)
        ex = jnp.exp(v2 - v1)
        comb_ref[...] = jnp.where(lane == i1, 1.0 / (1.0 + ex), 0.0) + jnp.where(lane == i2, ex / (1.0 + ex), 0.0)

    ce = jnp.sum(jnp.where(lane == e, comb_ref[...], 0.0), axis=-1, keepdims=True)
    h = h_ref[...]
    f = w1_ref.shape[2]
    ab = _mm(h, jnp.concatenate([w1_ref[0], w3_ref[0]], axis=1))
    z = _silu(ab[:, 0:f]) * ab[:, f:2 * f]
    acc_ref[...] += ce * _mm(z, w2_ref[0])

    @pl.when(e == pl.num_programs(1) - 1)
    def _():
        if final_rows:
            b = tile // (final_rows // acc_ref.shape[0])
            x = x_ref[...] + mod_ref[pl.ds(b, 1), 5 * D:6 * D] * acc_ref[...]
            ms = jnp.mean(x * x, axis=-1, keepdims=True)
            o_ref[...] = x * lax.rsqrt(ms + NORM_EPS) * g_ref[...]
        else:
            o_ref[...] = acc_ref[...]


def _moe(h2, logits, w1, w3, w2, fin=None):
    M = h2.shape[0]
    tm = 1024
    f = w1.shape[2]
    ins = [h2, logits, w1, w3, w2]
    in_specs = [pl.BlockSpec((tm, D), lambda i, e: (i, 0)),
                pl.BlockSpec((tm, 128), lambda i, e: (i, 0)),
                pl.BlockSpec((1, D, f), lambda i, e: (e, 0, 0)),
                pl.BlockSpec((1, D, f), lambda i, e: (e, 0, 0)),
                pl.BlockSpec((1, f, D), lambda i, e: (e, 0, 0))]
    final_rows = 0
    if fin is not None:
        x, mod, g, final_rows = fin
        assert final_rows % tm == 0
        ins += [x, mod, g]
        in_specs += [pl.BlockSpec((tm, D), lambda i, e: (i, 0)), pl.BlockSpec(mod.shape, lambda i, e: (0, 0)),
                     pl.BlockSpec(g.shape, lambda i, e: (0, 0))]
    return pl.pallas_call(
        functools.partial(_moe_body, final_rows=final_rows),
        grid=(M // tm, N_EXPERTS),
        in_specs=in_specs,
        out_specs=pl.BlockSpec((tm, D), lambda i, e: (i, 0)),
        out_shape=jax.ShapeDtypeStruct((M, D), F32),
        scratch_shapes=[pltpu.VMEM((tm, D), F32), pltpu.VMEM((tm, 128), F32)],
        compiler_params=_cparams(("parallel", "arbitrary")),
        name="moe",
    )(*ins)


def _final_body(x_ref, y_ref, mod_ref, g_ref, o_ref):
    b = pl.program_id(0)
    x = x_ref[0] + mod_ref[pl.ds(b, 1), 5 * D:6 * D] * y_ref[0]
    ms = jnp.mean(x * x, axis=-1, keepdims=True)
    o_ref[0] = x * lax.rsqrt(ms + NORM_EPS) * g_ref[...]


def _final(x, y, mod, g, lc):
    B, S, _ = x.shape
    T = S - lc
    off = lc // TM
    return pl.pallas_call(
        _final_body,
        grid=(B, T // TM),
        in_specs=[pl.BlockSpec((1, TM, D), lambda b, i: (b, i + off, 0)),
                  pl.BlockSpec((1, TM, D), lambda b, i: (b, i + off, 0)),
                  pl.BlockSpec(mod.shape, lambda b, i: (0, 0)),
                  pl.BlockSpec(g.shape, lambda b, i: (0, 0))],
        out_specs=pl.BlockSpec((1, TM, D), lambda b, i: (b, i, 0)),
        out_shape=jax.ShapeDtypeStruct((B, T, D), F32),
        compiler_params=_cparams(("parallel", "parallel")),
        name="final_norm",
    )(x, y, mod, g)


def _swap_halves(a, group):
    n = a.shape[-1]
    return a.reshape(a.shape[:-1] + (n // group, 2, group // 2))[..., ::-1, :].reshape(a.shape)


def _layer_weights(l, w_in, rw_mu, rw_w0, rw_w2, rw_a0, rw_a2, rw_g2, rw_k_k, rw_k_a, rw_r_k, rw_ln_g, rw_ln_b,
                   mla_qnorm_g, mla_wuq, mla_kvnorm_g, mla_wukv):
    wi = w_in[l]
    zc = lambda n: jnp.zeros((D, n), F32)
    o = 0
    w_hg = wi[:, o:o + 5 * GW]
    o += 5 * GW
    rwc = wi[:, o:o + 1088]
    o += 1088
    swc = wi[:, o:o + 512]
    o += 512
    mlc = wi[:, o:o + 352]

    def pad_rw(a, axis):
        z = lambda n: jnp.zeros(a.shape[:axis] + (n,) + a.shape[axis + 1:], a.dtype)
        sl = lambda s, e: lax.slice_in_dim(a, s, e, axis=axis)
        return jnp.concatenate([sl(0, 768), sl(768, 832), z(64), sl(832, 896), z(64), sl(896, 960), z(64),
                                sl(960, 1088)], axis=axis)

    w_rw = pad_rw(rwc, 1)
    mu = pad_rw(rw_mu[l], 1)
    pad_rows = lambda a: jnp.concatenate([a, jnp.zeros((64, a.shape[1]), a.dtype)], axis=0)
    w2 = jnp.stack([pad_rows(rw_w2[l, 0]), pad_rows(rw_w2[l, 1])]).astype(BF16)
    a2 = pad_rows(rw_a2[l]).astype(BF16)

    q = swc[:, 0:GW]
    kx = jnp.concatenate([swc[:, 256:320], swc[:, 256:320], swc[:, 320:384], swc[:, 320:384]], axis=1)
    vx = jnp.concatenate([swc[:, 384:448], swc[:, 384:448], swc[:, 448:512], swc[:, 448:512]], axis=1)
    w_swa = jnp.concatenate([q, kx, vx, _swap_halves(q, 32), _swap_halves(kx, 32)], axis=1)

    kr = mlc[:, 320:352]
    krp = _swap_halves(kr, 16)
    w_mla = jnp.concatenate([mlc[:, 0:192], zc(64), mlc[:, 192:320], zc(64), kr, zc(32), zc(64), krp, zc(32)], axis=1)

    wuq = mla_wuq[l]
    wukv = mla_wukv[l]
    wq, wqr, wk, wv = [], [], [], []
    zq = lambda r, n: jnp.zeros((r, n), F32)
    for h in range(NH):
        qh = wuq[:, 96 * h:96 * (h + 1)]
        a = jnp.concatenate([qh, zq(192, 32)], axis=1)
        ar = jnp.concatenate([zq(192, 64), _swap_halves(qh[:, 64:96], 16), zq(192, 32)], axis=1)
        wq.append(jnp.concatenate([a, zq(64, 128)], axis=0))
        wqr.append(jnp.concatenate([ar, zq(64, 128)], axis=0))
        wk.append(jnp.concatenate([wukv[:, 128 * h:128 * h + 64], zq(128, 64)], axis=1))
        wv.append(wukv[:, 128 * h + 64:128 * (h + 1)].T)
    row = lambda a: a.reshape(1, -1)
    return {
        "proj": jnp.concatenate([w_hg, w_rw, w_swa, w_mla], axis=1).astype(BF16),
        "rw": {"mu": mu, "w0": rw_w0[l], "w2": w2, "a0": row(rw_a0[l]), "a2": a2, "g2": rw_g2[l].astype(BF16),
               "k_k": row(rw_k_k[l]), "k_a": row(rw_k_a[l])},
        "rk": row(rw_r_k[l]), "lng": row(rw_ln_g[l]), "lnb": row(rw_ln_b[l]),
        "mla": {"qg": jnp.concatenate([row(mla_qnorm_g[l]), jnp.zeros((1, 64), F32)], axis=1),
                "kg": row(mla_kvnorm_g[l]),
                "wq": jnp.stack(wq).astype(BF16), "wqr": jnp.stack(wqr).astype(BF16),
                "wk": jnp.stack(wk).astype(BF16), "wv": jnp.stack(wv).astype(BF16)},
    }


def _rope_tables(T, lc):
    rows = T // GRID_W
    rowp = jnp.repeat(jnp.arange(rows, dtype=F32), GRID_W)
    colp = jnp.tile(jnp.arange(GRID_W, dtype=F32), rows)

    def table(rot_dim, width, lane0):
        nf = rot_dim // 4
        inv = ROPE_BASE ** (-jnp.arange(nf, dtype=F32) / nf)
        j = np.arange(rot_dim)
        ax = j // (2 * nf)
        fr = j % nf
        sign = np.where(j % (2 * nf) < nf, -1.0, 1.0).astype(np.float32)
        pos = jnp.stack([rowp, colp], axis=1)
        ang = pos[:, ax] * inv[fr][None, :]
        cos = jnp.concatenate([jnp.ones((lc, rot_dim), F32), jnp.cos(ang)], axis=0)
        sin = jnp.concatenate([jnp.zeros((lc, rot_dim), F32), jnp.sin(ang) * sign[None, :]], axis=0)
        reps = (width - lane0) // rot_dim if lane0 == 0 else 1
        cos = jnp.tile(cos, (1, reps))
        sin = jnp.tile(sin, (1, reps))
        return cos, sin

    cs_s, sn_s = table(HD, 2 * GW, 0)
    c32, s32 = table(MLA_ROPE, MLA_ROPE, 0)
    S = lc + T
    z = lambda n: jnp.zeros((S, n), F32)
    o = lambda n: jnp.ones((S, n), F32)
    cs_m = jnp.concatenate([z(64), c32, z(32)], axis=1)
    sn_m = jnp.concatenate([z(64), s32, z(32)], axis=1)
    cs_q = jnp.concatenate([o(64), c32, z(32)], axis=1)
    sn_q = jnp.concatenate([z(64), s32, z(32)], axis=1)
    return {"cs_s": cs_s, "sn_s": sn_s, "cs_m": cs_m, "sn_m": sn_m, "cs_q": cs_q, "sn_q": sn_q}


def kernel(x, c, ctx, c_ctx, w_ada, b_ada, norm1_g, norm2_g, w_in, hg_lb, hg_norm_g, rw_mu, rw_w0, rw_w2, rw_a0,
           rw_a2, rw_g2, rw_k_k, rw_k_a, rw_r_k, rw_ln_g, rw_ln_b, swa_sink, mla_qnorm_g, mla_wuq, mla_kvnorm_g,
           mla_wukv, w_out, ffn_w1, ffn_w3, ffn_w2, moe_router, moe_w1, moe_w3, moe_w2, final_norm_g):
    B, T, _ = x.shape
    lc = ctx.shape[1]
    S = lc + T
    L = w_in.shape[0]
    assert lc % TM == 0 and T % TM == 0 and lc % SWA_W == 0 and T % GRID_W == 0 and (B * S) % 1024 == 0 and (B * T) % 1024 == 0

    r8 = -(-(B + 1) // 8) * 8
    c8 = jnp.concatenate([c, c_ctx[None, :], jnp.zeros((r8 - B - 1, D), F32)], axis=0)
    mod_all = _ada(c8, w_ada, b_ada)
    tabs = _rope_tables(T, lc)
    xs = jnp.concatenate([ctx, x], axis=1)
    row = lambda a: a.reshape(1, -1)

    pend = None
    for l in range(L):
        w = _layer_weights(l, w_in, rw_mu, rw_w0, rw_w2, rw_a0, rw_a2, rw_g2, rw_k_k, rw_k_a, rw_r_k, rw_ln_g,
                           rw_ln_b, mla_qnorm_g, mla_wuq, mla_kvnorm_g, mla_wukv)
        mod = mod_all[l]
        xs, p_hg, p_rw, p_swa, p_mla = _proj(xs, pend, mod, row(norm1_g[l]), w["proj"], tabs, lc)

        hgo = _hgrn2(p_hg, hg_lb, l, lc)
        rwp = _rwprep(p_rw, w["rw"], lc)
        rwo = _rwscan(rwp, lc)
        sink8 = jnp.zeros((8, 128), F32).at[0:NH, :].set(jnp.broadcast_to(swa_sink[l][:, None], (NH, 128)))
        last = l == L - 1
        swo = _swa(p_swa, sink8, lc, last)
        mq, mk, mv = _mlaprep(p_mla, w["mla"], tabs)
        mlo = _mla(mq, mk, mv, lc, last)

        moe = l % 2 == 1
        router = None
        if moe:
            router = jnp.concatenate([moe_router[l // 2], jnp.zeros((D, 128 - N_EXPERTS), F32)], axis=1)
        outs = _outproj(xs, mod, row(norm2_g[l]), hgo, p_hg, row(hg_norm_g[l]), rwo, rwp, w["rk"], w["lng"],
                        w["lnb"], swo, mlo, w_out[l].astype(BF16), router, lc, last)
        xs, h2 = outs[0], outs[1]
        so = xs.shape[1]
        h2f = h2.reshape(B * so, D)
        if moe:
            e = l // 2
            fin = None
            if last and so == T and T % 1024 == 0:
                fin = (xs.reshape(B * T, D), mod, row(final_norm_g), T)
            y = _moe(h2f, outs[2].reshape(B * so, 128), moe_w1[e].astype(BF16), moe_w3[e].astype(BF16),
                     moe_w2[e].astype(BF16), fin)
            if fin is not None:
                return y.reshape(B, T, D)
        else:
            e = l // 2
            y = _ffn(h2f, ffn_w1[e].astype(BF16), ffn_w3[e].astype(BF16), ffn_w2[e].astype(BF16))
        pend = (y.reshape(B, so, D), mod)

    return _final(xs, pend[0], pend[1], row(final_norm_g), xs.shape[1] - T)
```
